```python
import math
import jax
import jax.numpy as jnp
from jax import lax
import numpy as np

D_MODEL = 2048
BATCH = 16
SEQ = 2048
DEPTH = 2

CHUNK = 64
NEG_INF = -1e30
NORM_EPS = 1e-6

HEAD_DIM = 128
A_HEADS = 8
A_WIDTH = A_HEADS * HEAD_DIM
A_LEFT_CHUNKS = 8
A_BAND = (A_LEFT_CHUNKS + 1) * CHUNK
A_MAX_REL = 128
B_HEADS = 4
B_QK_DIM = HEAD_DIM
B_V_DIM = 2 * HEAD_DIM
B_QK_WIDTH = B_HEADS * 2 * B_QK_DIM
B_WIDTH = B_HEADS * B_V_DIM
Q_BLOCK = 128
ATTN_SPLITS = (A_WIDTH, A_WIDTH, A_WIDTH, A_WIDTH, B_QK_WIDTH, B_QK_WIDTH, B_WIDTH, B_WIDTH)
ATTN_IN = sum(ATTN_SPLITS)
ATTN_MIX = A_WIDTH + B_WIDTH

LRU_BLOCKS = 6
LRU_BLOCK_W = 256
LRU_WIDTH = LRU_BLOCKS * LRU_BLOCK_W
CONV_W = 4
LRU_C = 8.0
SSM_GROUP = 16
SSM_GROUPS = 32
SSM_WIDTH = SSM_GROUPS * SSM_GROUP
SSM_STATE = 64
REC_SPLITS = (LRU_WIDTH, LRU_WIDTH, SSM_WIDTH, SSM_WIDTH)
REC_IN = sum(REC_SPLITS)
REC_MIX = LRU_WIDTH + SSM_WIDTH

N_EVEN = (DEPTH + 1) // 2
N_ODD = DEPTH // 2

kernel_name = 'chunk_causal_hybrid_relpos_diffattn_rglru_s5'


def rms_norm(x, gain):
    xf = x.astype(jnp.float32)
    y = xf * lax.rsqrt(jnp.mean(xf * xf, axis=-1, keepdims=True) + NORM_EPS)
    return (y * gain.astype(jnp.float32)).astype(x.dtype)


def split_cols(t, sizes):
    outs, start = [], 0
    for n in sizes:
        outs.append(t[..., start:start + n])
        start += n
    return outs


def chunked_relpos_attention(q, k, v, rel_bias):
    b, s, h, dh = q.shape
    n_chunks = s // CHUNK
    pad = A_LEFT_CHUNKS * CHUNK
    kp = jnp.pad(k, ((0, 0), (pad, 0), (0, 0), (0, 0)))
    vp = jnp.pad(v, ((0, 0), (pad, 0), (0, 0), (0, 0)))
    qi = jnp.arange(CHUNK)[:, None]
    kj = jnp.arange(A_BAND)[None, :]
    rel = jnp.clip(kj - pad - qi, -A_MAX_REL, A_MAX_REL) + A_MAX_REL
    bias = rel_bias.astype(jnp.float32)[:, rel]
    q_chunks = q.reshape(b, n_chunks, CHUNK, h, dh).transpose(1, 0, 2, 3, 4)
    scale = dh ** -0.5

    def one_chunk(args):
        qc, c = args
        start = c * CHUNK
        kc = lax.dynamic_slice_in_dim(kp, start, A_BAND, axis=1)
        vc = lax.dynamic_slice_in_dim(vp, start, A_BAND, axis=1)
        logits = jnp.einsum('bqhd,bkhd->bhqk', qc, kc).astype(jnp.float32) * scale + bias
        valid = (start - pad + jnp.arange(A_BAND)) >= 0
        logits = jnp.where(valid, logits, NEG_INF)
        p = jax.nn.softmax(logits, axis=-1).astype(vc.dtype)
        return jnp.einsum('bhqk,bkhd->bqhd', p, vc)

    out = lax.map(one_chunk, (q_chunks, jnp.arange(n_chunks)))
    return out.transpose(1, 0, 2, 3, 4).reshape(b, s, h, dh)


def differential_attention(q, k, v, lam):
    b, s, h, _, dk = q.shape
    n_blocks = s // Q_BLOCK
    q_blocks = q.reshape(b, n_blocks, Q_BLOCK, h, 2, dk).transpose(1, 0, 2, 3, 4, 5)
    k_pos = jnp.arange(s)
    slopes = 2.0 ** (-8.0 * jnp.arange(1, h + 1, dtype=jnp.float32) / h)
    scale = dk ** -0.5

    def one_block(args):
        qb, blk = args
        q_pos = blk * Q_BLOCK + jnp.arange(Q_BLOCK)
        allowed = (k_pos[None, :] // CHUNK) <= (q_pos[:, None] // CHUNK)
        dist = jnp.abs(q_pos[:, None] - k_pos[None, :]).astype(jnp.float32)
        bias = -slopes[:, None, None] * dist[None]
        logits = jnp.einsum('bqhcd,bkhcd->bchqk', qb, k).astype(jnp.float32) * scale + bias
        logits = jnp.where(allowed, logits, NEG_INF)
        p = jax.nn.softmax(logits, axis=-1)
        w = (p[:, 0] - lam * p[:, 1]).astype(v.dtype)
        return jnp.einsum('bhqk,bkhd->bqhd', w, v)

    out = lax.map(one_block, (q_blocks, jnp.arange(n_blocks)))
    return out.transpose(1, 0, 2, 3, 4).reshape(b, s, h, v.shape[-1])


def causal_depthwise_conv(u, w, bias):
    s = u.shape[1]
    up = jnp.pad(u, ((0, 0), (CONV_W - 1, 0), (0, 0)))
    return sum(w[tap] * up[:, tap:tap + s] for tap in range(CONV_W)) + bias


def linear_scan(a, b):
    def combine(left, right):
        a_l, b_l = left
        a_r, b_r = right
        return a_l * a_r, a_r * b_l + b_r
    _, h = lax.associative_scan(combine, (a, b), axis=1)
    return h


def rg_lru(xr, w_a, b_a, w_x, b_x, lam):
    b, s, _ = xr.shape
    f32 = jnp.float32
    xf = xr.astype(f32)
    xb = xf.reshape(b, s, LRU_BLOCKS, LRU_BLOCK_W)
    r = jax.nn.sigmoid(jnp.einsum('bsni,nij->bsnj', xb, w_a.astype(f32)) + b_a.astype(f32))
    i = jax.nn.sigmoid(jnp.einsum('bsni,nij->bsnj', xb, w_x.astype(f32)) + b_x.astype(f32))
    r = r.reshape(b, s, LRU_WIDTH)
    i = i.reshape(b, s, LRU_WIDTH)
    log_a = -LRU_C * r * jax.nn.softplus(-lam.astype(f32))
    a = jnp.exp(log_a)
    mult = jnp.sqrt(-jnp.expm1(2.0 * log_a))
    h = linear_scan(a, mult * (i * xf))
    return h.astype(xr.dtype)


def s5_ssm(u, a_re, a_im, b_re, b_im, c_re, c_im, d_skip, log_dt):
    b, s, _ = u.shape
    f32 = jnp.float32
    uf = u.astype(f32).reshape(b, s, SSM_GROUPS, SSM_GROUP)
    lam = lax.complex(a_re.astype(f32), a_im.astype(f32))
    dt = jnp.exp(log_dt.astype(f32))[:, None]
    a_bar = jnp.exp(lam * dt)
    b_mat = lax.complex(b_re.astype(f32), b_im.astype(f32))
    b_bar = ((a_bar - 1.0) / lam)[..., None] * b_mat
    bu = jnp.einsum('bsgh,gph->bsgp', uf.astype(jnp.complex64), b_bar)
    a_seq = jnp.broadcast_to(a_bar, (1, s) + a_bar.shape)
    states = linear_scan(a_seq, bu)
    c_mat = lax.complex(c_re.astype(f32), c_im.astype(f32))
    y = jnp.einsum('bsgp,ghp->bsgh', states, c_mat).real
    y = y + d_skip.astype(f32).reshape(SSM_GROUPS, SSM_GROUP) * uf
    return y.reshape(b, s, SSM_WIDTH)


def attention_layer(x, norm_g, w_in, q_g_a, k_g_a, rel_bias, q_g_b, k_g_b,
                    lq1, lk1, lq2, lk2, subln_g, w_out, layer_idx):
    b, s, _ = x.shape
    h = rms_norm(x, norm_g)
    aq, ak, av, ag, bq, bk, bv, bg = split_cols(h @ w_in, ATTN_SPLITS)
    aq = rms_norm(aq.reshape(b, s, A_HEADS, HEAD_DIM), q_g_a)
    ak = rms_norm(ak.reshape(b, s, A_HEADS, HEAD_DIM), k_g_a)
    av = av.reshape(b, s, A_HEADS, HEAD_DIM)
    ya = chunked_relpos_attention(aq, ak, av, rel_bias).reshape(b, s, A_WIDTH) * jax.nn.silu(ag)
    bq = rms_norm(bq.reshape(b, s, B_HEADS, 2, B_QK_DIM), q_g_b)
    bk = rms_norm(bk.reshape(b, s, B_HEADS, 2, B_QK_DIM), k_g_b)
    bv = bv.reshape(b, s, B_HEADS, B_V_DIM)
    f32 = jnp.float32
    lam_init = 0.8 - 0.6 * math.exp(-0.3 * layer_idx)
    lam = (jnp.exp(jnp.sum(lq1.astype(f32) * lk1.astype(f32)))
           - jnp.exp(jnp.sum(lq2.astype(f32) * lk2.astype(f32))) + lam_init)
    yb = differential_attention(bq, bk, bv, lam)
    yb = (rms_norm(yb, subln_g) * (1.0 - lam_init)).reshape(b, s, B_WIDTH) * jax.nn.silu(bg)
    return x + jnp.concatenate([ya, yb], axis=-1) @ w_out


def recurrent_layer(x, norm_g, w_in, conv_w, conv_b, w_a, b_a, w_x, b_x, lru_lam,
                    a_re, a_im, b_re, b_im, c_re, c_im, d_skip, log_dt, w_glu, b_glu, w_out):
    h = rms_norm(x, norm_g)
    c_in, c_gate, d_in, d_gate = split_cols(h @ w_in, REC_SPLITS)
    yc = rg_lru(causal_depthwise_conv(c_in, conv_w, conv_b), w_a, b_a, w_x, b_x, lru_lam)
    yc = yc * jax.nn.silu(c_gate)
    yd = jax.nn.gelu(s5_ssm(d_in, a_re, a_im, b_re, b_im, c_re, c_im, d_skip, log_dt))
    yd = yd * jax.nn.sigmoid(yd @ w_glu.astype(jnp.float32) + b_glu.astype(jnp.float32))
    yd = yd.astype(x.dtype) * jax.nn.silu(d_gate)
    return x + jnp.concatenate([yc, yd], axis=-1) @ w_out


def setup_inputs(seed: int = 0) -> dict:
    key = jax.random.key(seed)
    ks = list(jax.random.split(key, 48))
    f32 = jnp.float32

    def nrm(shape, std):
        return std * jax.random.normal(ks.pop(), shape, f32)

    def gain(shape):
        return 1.0 + nrm(shape, 0.02)

    ne, no = N_EVEN, N_ODD
    x = nrm((BATCH, SEQ, D_MODEL), 1.0)
    attn_norm_g = gain((ne, D_MODEL))
    attn_w_in = nrm((ne, D_MODEL, ATTN_IN), D_MODEL ** -0.5)
    a_q_g = gain((ne, HEAD_DIM))
    a_k_g = gain((ne, HEAD_DIM))
    a_rel_bias = nrm((ne, A_HEADS, 2 * A_MAX_REL + 1), 0.5)
    b_q_g = gain((ne, 2, B_QK_DIM))
    b_k_g = gain((ne, 2, B_QK_DIM))
    b_lam_q1 = nrm((ne, B_QK_DIM), 0.1)
    b_lam_k1 = nrm((ne, B_QK_DIM), 0.1)
    b_lam_q2 = nrm((ne, B_QK_DIM), 0.1)
    b_lam_k2 = nrm((ne, B_QK_DIM), 0.1)
    b_subln_g = gain((ne, B_V_DIM))
    attn_w_out = nrm((ne, ATTN_MIX, D_MODEL), ATTN_MIX ** -0.5)
    rec_norm_g = gain((no, D_MODEL))
    rec_w_in = nrm((no, D_MODEL, REC_IN), D_MODEL ** -0.5)
    lru_conv_w = nrm((no, CONV_W, LRU_WIDTH), CONV_W ** -0.5)
    lru_conv_b = nrm((no, LRU_WIDTH), 0.02)
    lru_w_a = nrm((no, LRU_BLOCKS, LRU_BLOCK_W, LRU_BLOCK_W), LRU_BLOCK_W ** -0.5)
    lru_b_a = nrm((no, LRU_BLOCKS, LRU_BLOCK_W), 0.02)
    lru_w_x = nrm((no, LRU_BLOCKS, LRU_BLOCK_W, LRU_BLOCK_W), LRU_BLOCK_W ** -0.5)
    lru_b_x = nrm((no, LRU_BLOCKS, LRU_BLOCK_W), 0.02)
    a_pow = jax.random.uniform(ks.pop(), (no, LRU_WIDTH), f32, 0.9, 0.999)
    sig = a_pow ** (1.0 / LRU_C)
    lru_lambda = jnp.log(sig) - jnp.log1p(-sig)
    ssm_a_re = -0.5 + nrm((no, SSM_GROUPS, SSM_STATE), 0.01)
    ssm_a_im = math.pi * jnp.arange(SSM_STATE, dtype=f32) + nrm((no, SSM_GROUPS, SSM_STATE), 0.01)
    ssm_b_re = nrm((no, SSM_GROUPS, SSM_STATE, SSM_GROUP), (0.5 / SSM_GROUP) ** 0.5)
    ssm_b_im = nrm((no, SSM_GROUPS, SSM_STATE, SSM_GROUP), (0.5 / SSM_GROUP) ** 0.5)
    ssm_c_re = nrm((no, SSM_GROUPS, SSM_GROUP, SSM_STATE), (0.5 / SSM_STATE) ** 0.5)
    ssm_c_im = nrm((no, SSM_GROUPS, SSM_GROUP, SSM_STATE), (0.5 / SSM_STATE) ** 0.5)
    ssm_d = nrm((no, SSM_WIDTH), 1.0)
    ssm_log_dt = jax.random.uniform(ks.pop(), (no, SSM_GROUPS), f32, math.log(1e-3), math.log(1e-1))
    ssm_w_glu = nrm((no, SSM_WIDTH, SSM_WIDTH), SSM_WIDTH ** -0.5)
    ssm_b_glu = nrm((no, SSM_WIDTH), 0.02)
    rec_w_out = nrm((no, REC_MIX, D_MODEL), REC_MIX ** -0.5)
    return {'x': x, 'attn_norm_g': attn_norm_g, 'attn_w_in': attn_w_in, 'a_q_g': a_q_g,
            'a_k_g': a_k_g, 'a_rel_bias': a_rel_bias, 'b_q_g': b_q_g, 'b_k_g': b_k_g,
            'b_lam_q1': b_lam_q1, 'b_lam_k1': b_lam_k1, 'b_lam_q2': b_lam_q2, 'b_lam_k2': b_lam_k2,
            'b_subln_g': b_subln_g, 'attn_w_out': attn_w_out, 'rec_norm_g': rec_norm_g,
            'rec_w_in': rec_w_in, 'lru_conv_w': lru_conv_w, 'lru_conv_b': lru_conv_b,
            'lru_w_a': lru_w_a, 'lru_b_a': lru_b_a, 'lru_w_x': lru_w_x, 'lru_b_x': lru_b_x,
            'lru_lambda': lru_lambda, 'ssm_a_re': ssm_a_re, 'ssm_a_im': ssm_a_im,
            'ssm_b_re': ssm_b_re, 'ssm_b_im': ssm_b_im, 'ssm_c_re': ssm_c_re, 'ssm_c_im': ssm_c_im,
            'ssm_d': ssm_d, 'ssm_log_dt': ssm_log_dt, 'ssm_w_glu': ssm_w_glu, 'ssm_b_glu': ssm_b_glu,
            'rec_w_out': rec_w_out}


def reference(x, attn_norm_g, attn_w_in, a_q_g, a_k_g, a_rel_bias, b_q_g, b_k_g,
              b_lam_q1, b_lam_k1, b_lam_q2, b_lam_k2, b_subln_g, attn_w_out,
              rec_norm_g, rec_w_in, lru_conv_w, lru_conv_b, lru_w_a, lru_b_a, lru_w_x, lru_b_x,
              lru_lambda, ssm_a_re, ssm_a_im, ssm_b_re, ssm_b_im, ssm_c_re, ssm_c_im,
              ssm_d, ssm_log_dt, ssm_w_glu, ssm_b_glu, rec_w_out):
    for layer in range(DEPTH):
        j = layer // 2
        if layer % 2 == 0:
            x = attention_layer(x, attn_norm_g[j], attn_w_in[j], a_q_g[j], a_k_g[j], a_rel_bias[j],
                                b_q_g[j], b_k_g[j], b_lam_q1[j], b_lam_k1[j], b_lam_q2[j], b_lam_k2[j],
                                b_subln_g[j], attn_w_out[j], layer)
        else:
            x = recurrent_layer(x, rec_norm_g[j], rec_w_in[j], lru_conv_w[j], lru_conv_b[j],
                                lru_w_a[j], lru_b_a[j], lru_w_x[j], lru_b_x[j], lru_lambda[j],
                                ssm_a_re[j], ssm_a_im[j], ssm_b_re[j], ssm_b_im[j],
                                ssm_c_re[j], ssm_c_im[j], ssm_d[j], ssm_log_dt[j],
                                ssm_w_glu[j], ssm_b_glu[j], rec_w_out[j])
    return x
```

```python
import functools
import math

import jax
import jax.numpy as jnp
from jax import lax
from jax.experimental import pallas as pl
from jax.experimental.pallas import tpu as pltpu

F32 = jnp.float32
BF16 = jnp.bfloat16

D_MODEL = 2048
CHUNK = 64
NEG_INF = -1e30
NORM_EPS = 1e-6

HEAD_DIM = 128
A_HEADS = 8
A_WIDTH = A_HEADS * HEAD_DIM
A_LEFT_CHUNKS = 8
A_MAX_REL = 128
B_HEADS = 4
B_V_DIM = 2 * HEAD_DIM
B_WIDTH = B_HEADS * B_V_DIM
ATTN_IN = 8 * 1024

LRU_BLOCKS = 6
LRU_BLOCK_W = 256
LRU_WIDTH = LRU_BLOCKS * LRU_BLOCK_W
CONV_W = 4
LRU_C = 8.0
SSM_GROUP = 16
SSM_GROUPS = 32
SSM_WIDTH = SSM_GROUPS * SSM_GROUP
SSM_STATE = 64
REC_IN = 2 * LRU_WIDTH + 2 * SSM_WIDTH

LANES = 128
SUBLANES = 8
VMEM_LIMIT = 56 * 1024 * 1024

Q_BLK = 256
A_WIN = A_LEFT_CHUNKS * CHUNK + Q_BLK
SSM_L = 16
SCAN_SEGS = SUBLANES


def _sigmoid(x):
    return 0.5 * jnp.tanh(0.5 * x) + 0.5


def _cparams(sem):
    return pltpu.CompilerParams(dimension_semantics=sem, vmem_limit_bytes=VMEM_LIMIT)


def _norm_proj_body(x_ref, g_ref, w_ref, cg_ref, o_ref, h_ref, *, norm_tiles, tn):
    j = pl.program_id(1)

    @pl.when(j == 0)
    def _():
        x = x_ref[...]
        ms = jnp.mean(x * x, axis=-1, keepdims=True)
        h_ref[...] = (x * lax.rsqrt(ms + NORM_EPS) * g_ref[...]).astype(BF16)

    acc = jnp.dot(h_ref[...], w_ref[...], preferred_element_type=F32)

    def store_plain():
        o_ref[...] = acc.astype(BF16)

    def store_head_normed():
        for hh in range(tn // HEAD_DIM):
            sl = slice(hh * HEAD_DIM, (hh + 1) * HEAD_DIM)
            blk = acc[:, sl]
            ms = jnp.mean(blk * blk, axis=-1, keepdims=True)
            o_ref[:, sl] = (blk * lax.rsqrt(ms + NORM_EPS) * cg_ref[:, sl]).astype(BF16)

    if norm_tiles:
        is_norm = functools.reduce(jnp.logical_or, [j == t for t in norm_tiles])
        pl.when(is_norm)(store_head_normed)
        pl.when(jnp.logical_not(is_norm))(store_plain)
    else:
        store_plain()


def _norm_proj(x2, gain, w_bf16, col_gain, norm_tiles, *, tm=512, tn=1024):
    t, d = x2.shape
    n = w_bf16.shape[1]
    body = functools.partial(_norm_proj_body, norm_tiles=tuple(norm_tiles), tn=tn)
    return pl.pallas_call(
        body,
        grid=(t // tm, n // tn),
        in_specs=[
            pl.BlockSpec((tm, d), lambda i, j: (i, 0)),
            pl.BlockSpec((1, d), lambda i, j: (0, 0)),
            pl.BlockSpec((d, tn), lambda i, j: (0, j)),
            pl.BlockSpec((1, tn), lambda i, j: (0, j)),
        ],
        out_specs=pl.BlockSpec((tm, tn), lambda i, j: (i, j)),
        out_shape=jax.ShapeDtypeStruct((t, n), BF16),
        scratch_shapes=[pltpu.VMEM((tm, d), BF16)],
        compiler_params=_cparams(("parallel", "arbitrary")),
        name="norm_proj",
    )(x2, gain, w_bf16, col_gain)


def _attn_a_body(q_ref, k_ref, v_ref, g_ref, bias_ref, o_ref, *, heads, seq):
    for hh in range(heads):
        cols = slice(hh * HEAD_DIM, (hh + 1) * HEAD_DIM)
        for i in range(seq // Q_BLK):
            rows = slice(i * Q_BLK, (i + 1) * Q_BLK)
            ks = max(0, i * Q_BLK - A_LEFT_CHUNKS * CHUNK)
            win = (i + 1) * Q_BLK - ks
            q = q_ref[rows, cols]
            k = k_ref[ks:ks + win, cols]
            v = v_ref[ks:ks + win, cols]
            s = lax.dot_general(q, k, (((1,), (1,)), ((), ())), preferred_element_type=F32)
            s = s + bias_ref[hh, :, A_WIN - win:]
            m = jnp.max(s, axis=-1, keepdims=True)
            p = jnp.exp(s - m)
            l = jnp.sum(p, axis=-1, keepdims=True)
            o = jnp.dot(p.astype(BF16), v, preferred_element_type=F32) / l
            g = g_ref[rows, cols].astype(F32)
            o_ref[rows, cols] = (o * (g * _sigmoid(g))).astype(BF16)


def _attn_a(u3, bias, *, heads_per_step=2):
    b, s, _ = u3.shape
    w = heads_per_step * HEAD_DIM
    per = A_WIDTH // w
    body = functools.partial(_attn_a_body, heads=heads_per_step, seq=s)

    def col_spec(section):
        return pl.BlockSpec((None, s, w), lambda bi, hg: (bi, 0, section * per + hg))

    return pl.pallas_call(
        body,
        grid=(b, per),
        in_specs=[
            col_spec(0), col_spec(1), col_spec(2), col_spec(3),
            pl.BlockSpec((heads_per_step, Q_BLK, A_WIN), lambda bi, hg: (hg, 0, 0)),
        ],
        out_specs=pl.BlockSpec((None, s, w), lambda bi, hg: (bi, 0, hg)),
        out_shape=jax.ShapeDtypeStruct((b, s, A_WIDTH), BF16),
        compiler_params=_cparams(("parallel", "arbitrary")),
        name="attn_a",
    )(u3, u3, u3, u3, bias)


def _attn_a_bias(rel_bias):
    qi = jnp.arange(Q_BLK)[:, None]
    kj = jnp.arange(A_WIN)[None, :] - A_LEFT_CHUNKS * CHUNK
    rel = jnp.clip(kj - qi, -A_MAX_REL, A_MAX_REL) + A_MAX_REL
    qc = qi // CHUNK
    kc = jnp.floor_divide(kj, CHUNK)
    allowed = (kc <= qc) & (kc >= qc - A_LEFT_CHUNKS)
    return jnp.where(allowed[None], rel_bias.astype(F32)[:, rel], NEG_INF)


def _attn_b_body(q_ref, k_ref, v_ref, g_ref, lq1_ref, lk1_ref, lq2_ref, lk2_ref, sub_ref,
                 o_ref, bias_ref, *, seq, lam_init):
    h = pl.program_id(1)
    n_blk = seq // Q_BLK
    diag0 = seq - Q_BLK

    slope = jnp.exp2((-8.0 / B_HEADS) * jnp.full((1, 1), h + 1, jnp.int32).astype(F32))
    r = lax.broadcasted_iota(jnp.int32, (Q_BLK, seq), 0)
    c = lax.broadcasted_iota(jnp.int32, (Q_BLK, seq), 1) - diag0
    dist = jnp.abs(r - c).astype(F32)
    chunk_bits = CHUNK.bit_length() - 1
    allowed = jnp.logical_or(c < 0, lax.shift_right_arithmetic(c, chunk_bits)
                             <= lax.shift_right_arithmetic(r, chunk_bits))
    bias_ref[...] = jnp.where(allowed, -slope * dist, NEG_INF)

    lam = (jnp.exp(jnp.sum(lq1_ref[...] * lk1_ref[...], axis=-1, keepdims=True))
           - jnp.exp(jnp.sum(lq2_ref[...] * lk2_ref[...], axis=-1, keepdims=True)) + lam_init)

    for i in range(n_blk):
        rows = slice(i * Q_BLK, (i + 1) * Q_BLK)
        win = (i + 1) * Q_BLK
        bias = bias_ref[:, seq - win:]
        probs = []
        for comp in range(2):
            cols = slice(comp * HEAD_DIM, (comp + 1) * HEAD_DIM)
            s = lax.dot_general(q_ref[rows, cols], k_ref[0:win, cols],
                                (((1,), (1,)), ((), ())), preferred_element_type=F32)
            s = s + bias
            m = jnp.max(s, axis=-1, keepdims=True)
            p = jnp.exp(s - m)
            probs.append(p / jnp.sum(p, axis=-1, keepdims=True))
        w = (probs[0] - lam * probs[1]).astype(BF16)
        o = jnp.dot(w, v_ref[0:win, :], preferred_element_type=F32)
        ms = jnp.mean(o * o, axis=-1, keepdims=True)
        o = o * lax.rsqrt(ms + NORM_EPS) * sub_ref[...] * (1.0 - lam_init)
        g = g_ref[rows, :].astype(F32)
        o_ref[rows, :] = (o * (g * _sigmoid(g))).astype(BF16)


def _attn_b(u3, lq1, lk1, lq2, lk2, subln_g, lam_init):
    b, s, _ = u3.shape
    per = B_WIDTH // B_V_DIM
    body = functools.partial(_attn_b_body, seq=s, lam_init=lam_init)

    def col_spec(section):
        return pl.BlockSpec((None, s, B_V_DIM), lambda bi, h: (bi, 0, section * per + h))

    def vec_spec(n):
        return pl.BlockSpec((1, n), lambda bi, h: (0, 0))

    return pl.pallas_call(
        body,
        grid=(b, B_HEADS),
        in_specs=[col_spec(4), col_spec(5), col_spec(6), col_spec(7),
                  vec_spec(HEAD_DIM), vec_spec(HEAD_DIM), vec_spec(HEAD_DIM), vec_spec(HEAD_DIM),
                  vec_spec(B_V_DIM)],
        out_specs=pl.BlockSpec((None, s, B_V_DIM), lambda bi, h: (bi, 0, h)),
        out_shape=jax.ShapeDtypeStruct((b, s, B_WIDTH), BF16),
        scratch_shapes=[pltpu.VMEM((Q_BLK, s), F32)],
        compiler_params=_cparams(("parallel", "arbitrary")),
        name="attn_b",
    )(u3, u3, u3, u3, lq1, lk1, lq2, lk2, subln_g)


def _out_proj_body(ya_ref, yb_ref, w_ref, x_ref, o_ref, *, wa):
    acc = jnp.dot(ya_ref[...], w_ref[0:wa, :], preferred_element_type=F32)
    acc = acc + jnp.dot(yb_ref[...], w_ref[wa:, :], preferred_element_type=F32)
    o_ref[...] = x_ref[...] + acc


def _out_proj(ya, yb, w_bf16, x2, *, tm=512):
    t, d = x2.shape
    wa, wb = ya.shape[1], yb.shape[1]
    return pl.pallas_call(
        functools.partial(_out_proj_body, wa=wa),
        grid=(t // tm,),
        in_specs=[
            pl.BlockSpec((tm, wa), lambda i: (i, 0)),
            pl.BlockSpec((tm, wb), lambda i: (i, 0)),
            pl.BlockSpec((wa + wb, d), lambda i: (0, 0)),
            pl.BlockSpec((tm, d), lambda i: (i, 0)),
        ],
        out_specs=pl.BlockSpec((tm, d), lambda i: (i, 0)),
        out_shape=jax.ShapeDtypeStruct((t, d), F32),
        compiler_params=_cparams(("parallel",)),
        name="out_proj",
    )(ya, yb, w_bf16, x2)


def _scan_pitch(seq):
    seg = -(-seq // SCAN_SEGS)
    return seg + (4 - seg) % SUBLANES


def _lru_body(x_ref, g_ref, cw_ref, cb_ref, wa_ref, ba_ref, wx_ref, bx_ref, lam_ref,
              o_ref, a_s, b_s, *, seq, pitch):
    n_slab = LRU_BLOCK_W // LANES
    x = x_ref[...].astype(F32)
    t_idx = lax.broadcasted_iota(jnp.int32, (seq, 1), 0)
    xc = cw_ref[CONV_W - 1:CONV_W, :] * x + cb_ref[...]
    for d in range(1, CONV_W):
        xs = jnp.where(t_idx >= d, pltpu.roll(x, d, axis=0), 0.0)
        xc = xc + cw_ref[CONV_W - 1 - d:CONV_W - d, :] * xs

    xb = xc.astype(BF16)
    r = _sigmoid(jnp.dot(xb, wa_ref[...], preferred_element_type=F32) + ba_ref[...])
    gi = _sigmoid(jnp.dot(xb, wx_ref[...], preferred_element_type=F32) + bx_ref[...])
    lam = lam_ref[...]
    softplus_neg = jnp.maximum(-lam, 0.0) + jnp.log1p(jnp.exp(-jnp.abs(lam)))
    log_a = (-LRU_C) * r * softplus_neg
    a = jnp.exp(log_a)
    mult = jnp.sqrt(1.0 - a * a)
    bb = mult * (gi * xc)

    pad = SCAN_SEGS * pitch - seq
    for sl in range(n_slab):
        cols = slice(sl * LANES, (sl + 1) * LANES)
        a_s[sl, 0:seq, :] = a[:, cols]
        b_s[sl, 0:seq, :] = bb[:, cols]
        a_s[sl, seq:, :] = jnp.zeros((pad, LANES), F32)
        b_s[sl, seq:, :] = jnp.zeros((pad, LANES), F32)

    def pass1(j, carry):
        new = []
        for sl in range(n_slab):
            hh, pp = carry[2 * sl], carry[2 * sl + 1]
            idx = pl.ds(j, SCAN_SEGS, stride=pitch)
            aj = a_s[sl, idx, :]
            bj = b_s[sl, idx, :]
            hh = aj * hh + bj
            pp = aj * pp
            b_s[sl, idx, :] = hh
            a_s[sl, idx, :] = pp
            new += [hh, pp]
        return tuple(new)

    init = []
    for sl in range(n_slab):
        init += [jnp.zeros((SCAN_SEGS, LANES), F32), jnp.ones((SCAN_SEGS, LANES), F32)]
    ends = lax.fori_loop(0, pitch, pass1, tuple(init))

    seg_idx = lax.broadcasted_iota(jnp.int32, (SCAN_SEGS, LANES), 0)
    carries = []
    for sl in range(n_slab):
        h_end, p_end = ends[2 * sl], ends[2 * sl + 1]
        c = jnp.zeros((SCAN_SEGS, LANES), F32)
        for sgm in range(1, SCAN_SEGS):
            c = jnp.where(seg_idx == sgm, pltpu.roll(h_end + p_end * c, 1, axis=0), c)
        carries.append(c)

    def pass2(j, _):
        for sl in range(n_slab):
            idx = pl.ds(j, SCAN_SEGS, stride=pitch)
            b_s[sl, idx, :] = b_s[sl, idx, :] + a_s[sl, idx, :] * carries[sl]
        return 0

    lax.fori_loop(0, pitch, pass2, 0)

    for sl in range(n_slab):
        cols = slice(sl * LANES, (sl + 1) * LANES)
        g = g_ref[:, cols].astype(F32)
        o_ref[:, cols] = (b_s[sl, 0:seq, :] * (g * _sigmoid(g))).astype(BF16)


def _rg_lru(u3, conv_w, conv_b, w_a, b_a, w_x, b_x, lam):
    b, s, _ = u3.shape
    pitch = _scan_pitch(s)
    n_slab = LRU_BLOCK_W // LANES
    body = functools.partial(_lru_body, seq=s, pitch=pitch)

    def vec_spec(rows):
        return pl.BlockSpec((rows, LRU_BLOCK_W), lambda bi, n: (0, n))

    def blk_spec(rows):
        return pl.BlockSpec((None, rows, LRU_BLOCK_W), lambda bi, n: (n, 0, 0))

    return pl.pallas_call(
        body,
        grid=(b, LRU_BLOCKS),
        in_specs=[
            pl.BlockSpec((None, s, LRU_BLOCK_W), lambda bi, n: (bi, 0, n)),
            pl.BlockSpec((None, s, LRU_BLOCK_W), lambda bi, n: (bi, 0, LRU_BLOCKS + n)),
            vec_spec(CONV_W), vec_spec(1),
            blk_spec(LRU_BLOCK_W), blk_spec(1), blk_spec(LRU_BLOCK_W), blk_spec(1),
            vec_spec(1),
        ],
        out_specs=pl.BlockSpec((None, s, LRU_BLOCK_W), lambda bi, n: (bi, 0, n)),
        out_shape=jax.ShapeDtypeStruct((b, s, LRU_WIDTH), BF16),
        scratch_shapes=[pltpu.VMEM((n_slab, SCAN_SEGS * pitch, LANES), F32),
                        pltpu.VMEM((n_slab, SCAN_SEGS * pitch, LANES), F32)],
        compiler_params=_cparams(("parallel", "arbitrary")),
        name="rg_lru",
    )(u3, u3, conv_w, conv_b, w_a, b_a, w_x, b_x, lam)


def _s5_body(u_ref, m_ref, wst_ref, wout_ref, a_ref, y_ref, e_s, h_s, *, n_chunks, batch):
    u = u_ref[...]
    e_s[...] = jnp.dot(u, wst_ref[...], preferred_element_type=F32)
    a_re = a_ref[:, 0:LANES]
    a_im = a_ref[:, LANES:]

    def step(c, carry):
        h_re, h_im = carry
        rows = pl.ds(pl.multiple_of(c * batch, batch), batch)
        h_s[rows, 0:LANES] = h_re
        h_s[rows, LANES:] = h_im
        e_re = e_s[rows, 0:LANES]
        e_im = e_s[rows, LANES:]
        return (a_re * h_re - a_im * h_im + e_re, a_re * h_im + a_im * h_re + e_im)

    zero = jnp.zeros((batch, LANES), F32)
    lax.fori_loop(0, n_chunks, step, (zero, zero))

    y = jnp.dot(u, m_ref[...], preferred_element_type=F32)
    y = y + jnp.dot(h_s[...].astype(BF16), wout_ref[...], preferred_element_type=F32)
    y_ref[...] = y


def _s5_chunked(u5, m_mat, w_state, w_out, a_pow, *, batch):
    g, rows, width = u5.shape
    body = functools.partial(_s5_body, n_chunks=rows // batch, batch=batch)

    def grp(r, c):
        return pl.BlockSpec((None, r, c), lambda gi: (gi, 0, 0))

    return pl.pallas_call(
        body,
        grid=(g,),
        in_specs=[grp(rows, width), grp(width, width), grp(width, 2 * LANES),
                  grp(2 * LANES, width), grp(1, 2 * LANES)],
        out_specs=grp(rows, width),
        out_shape=jax.ShapeDtypeStruct((g, rows, width), F32),
        scratch_shapes=[pltpu.VMEM((rows, 2 * LANES), F32), pltpu.VMEM((rows, 2 * LANES), F32)],
        compiler_params=_cparams(("parallel",)),
        name="s5_chunked",
    )(u5, m_mat, w_state, w_out, a_pow)


def _s5_operators(a_re, a_im, b_re, b_im, c_re, c_im, log_dt):
    a_re, a_im = a_re.astype(F32), a_im.astype(F32)
    dt = jnp.exp(log_dt.astype(F32))[:, None]
    tau = jnp.arange(SSM_L + 1, dtype=F32)[:, None, None]
    mag = jnp.exp(tau * (a_re * dt)[None])
    ang = tau * (a_im * dt)[None]
    pw_re, pw_im = mag * jnp.cos(ang), mag * jnp.sin(ang)
    ab_re, ab_im = pw_re[1], pw_im[1]
    den = a_re * a_re + a_im * a_im
    f_re = ((ab_re - 1.0) * a_re + ab_im * a_im) / den
    f_im = (ab_im * a_re - (ab_re - 1.0) * a_im) / den
    b_re, b_im = b_re.astype(F32), b_im.astype(F32)
    bb_re = f_re[..., None] * b_re - f_im[..., None] * b_im
    bb_im = f_re[..., None] * b_im + f_im[..., None] * b_re
    c_re, c_im = c_re.astype(F32), c_im.astype(F32)

    hp = lax.Precision.HIGHEST
    ca_re = c_re[None] * pw_re[:, :, None, :] - c_im[None] * pw_im[:, :, None, :]
    ca_im = c_re[None] * pw_im[:, :, None, :] + c_im[None] * pw_re[:, :, None, :]
    kern = (jnp.einsum('tgop,gpi->tgoi', ca_re[:SSM_L], bb_re, precision=hp)
            - jnp.einsum('tgop,gpi->tgoi', ca_im[:SSM_L], bb_im, precision=hp))
    s_idx = jnp.arange(SSM_L)[:, None]
    t_idx = jnp.arange(SSM_L)[None, :]
    lag = t_idx - s_idx
    k_st = kern[jnp.clip(lag, 0, SSM_L - 1)]
    k_st = jnp.where((lag >= 0)[:, :, None, None, None], k_st, 0.0)
    width = SSM_L * SSM_GROUP
    m_mat = k_st.transpose(2, 0, 4, 1, 3).reshape(SSM_GROUPS, width, width)

    rev_re, rev_im = pw_re[SSM_L - 1::-1][:SSM_L], pw_im[SSM_L - 1::-1][:SSM_L]
    ws_re = rev_re[..., None] * bb_re[None] - rev_im[..., None] * bb_im[None]
    ws_im = rev_re[..., None] * bb_im[None] + rev_im[..., None] * bb_re[None]
    zpad = jnp.zeros((SSM_GROUPS, width, LANES - SSM_STATE), F32)
    ws_re = ws_re.transpose(1, 0, 3, 2).reshape(SSM_GROUPS, width, SSM_STATE)
    ws_im = ws_im.transpose(1, 0, 3, 2).reshape(SSM_GROUPS, width, SSM_STATE)
    w_state = jnp.concatenate([ws_re, zpad, ws_im, zpad], axis=-1)

    wo_re = ca_re[1:].transpose(1, 3, 0, 2).reshape(SSM_GROUPS, SSM_STATE, width)
    wo_im = -ca_im[1:].transpose(1, 3, 0, 2).reshape(SSM_GROUPS, SSM_STATE, width)
    zrow = jnp.zeros((SSM_GROUPS, LANES - SSM_STATE, width), F32)
    w_out = jnp.concatenate([wo_re, zrow, wo_im, zrow], axis=1)

    zl = jnp.zeros((SSM_GROUPS, LANES - SSM_STATE), F32)
    a_pow = jnp.concatenate([pw_re[SSM_L], zl, pw_im[SSM_L], zl], axis=-1)[:, None, :]
    return m_mat.astype(BF16), w_state.astype(BF16), w_out.astype(BF16), a_pow


def _rec_out_body(yc_ref, y5_ref, din_ref, dg_ref, dskip_ref, wglu_ref, bglu_ref, w_ref, x_ref,
                  o_ref, *, wc):
    yd = y5_ref[...] + dskip_ref[...] * din_ref[...].astype(F32)
    inner = math.sqrt(2.0 / math.pi) * (yd + 0.044715 * (yd * yd * yd))
    yd = 0.5 * yd * (1.0 + jnp.tanh(inner))
    glu = jnp.dot(yd.astype(BF16), wglu_ref[...], preferred_element_type=F32) + bglu_ref[...]
    yd = yd * _sigmoid(glu)
    g = dg_ref[...].astype(F32)
    yd = (yd * (g * _sigmoid(g))).astype(BF16)
    acc = jnp.dot(yc_ref[...], w_ref[0:wc, :], preferred_element_type=F32)
    acc = acc + jnp.dot(yd, w_ref[wc:, :], preferred_element_type=F32)
    o_ref[...] = x_ref[...] + acc


def _rec_out(yc, y5, u2, d_skip, w_glu, b_glu, w_bf16, x2, *, tm=512):
    t, d = x2.shape
    wc = yc.shape[1]
    din_blk = 2 * LRU_WIDTH // SSM_WIDTH
    return pl.pallas_call(
        functools.partial(_rec_out_body, wc=wc),
        grid=(t // tm,),
        in_specs=[
            pl.BlockSpec((tm, wc), lambda i: (i, 0)),
            pl.BlockSpec((tm, SSM_WIDTH), lambda i: (i, 0)),
            pl.BlockSpec((tm, SSM_WIDTH), lambda i: (i, din_blk)),
            pl.BlockSpec((tm, SSM_WIDTH), lambda i: (i, din_blk + 1)),
            pl.BlockSpec((1, SSM_WIDTH), lambda i: (0, 0)),
            pl.BlockSpec((SSM_WIDTH, SSM_WIDTH), lambda i: (0, 0)),
            pl.BlockSpec((1, SSM_WIDTH), lambda i: (0, 0)),
            pl.BlockSpec((wc + SSM_WIDTH, d), lambda i: (0, 0)),
            pl.BlockSpec((tm, d), lambda i: (i, 0)),
        ],
        out_specs=pl.BlockSpec((tm, d), lambda i: (i, 0)),
        out_shape=jax.ShapeDtypeStruct((t, d), F32),
        compiler_params=_cparams(("parallel",)),
        name="rec_out",
    )(yc, y5, u2, u2, d_skip, w_glu, b_glu, w_bf16, x2)


def _attention_layer(x2, b, s, norm_g, w_in, q_g_a, k_g_a, rel_bias, q_g_b, k_g_b,
                     lq1, lk1, lq2, lk2, subln_g, w_out, layer_idx):
    scale = HEAD_DIM ** -0.5
    ones = jnp.ones((1024,), F32)
    col_gain = jnp.concatenate([
        jnp.tile(q_g_a.astype(F32) * scale, A_HEADS), jnp.tile(k_g_a.astype(F32), A_HEADS), ones, ones,
        jnp.tile(q_g_b.astype(F32).reshape(-1) * scale, B_HEADS),
        jnp.tile(k_g_b.astype(F32).reshape(-1), B_HEADS), ones, ones])[None, :]
    tn = 1024
    u = _norm_proj(x2, norm_g.astype(F32)[None, :], w_in.astype(BF16), col_gain,
                   norm_tiles=(0, 1, 4, 5), tn=tn)
    u3 = u.reshape(b, s, ATTN_IN)
    ya = _attn_a(u3, _attn_a_bias(rel_bias))
    lam_init = 0.8 - 0.6 * math.exp(-0.3 * layer_idx)
    row = lambda v: v.astype(F32)[None, :]
    yb = _attn_b(u3, row(lq1), row(lk1), row(lq2), row(lk2), row(subln_g), lam_init)
    return _out_proj(ya.reshape(b * s, A_WIDTH), yb.reshape(b * s, B_WIDTH), w_out.astype(BF16), x2)


def _recurrent_layer(x2, b, s, norm_g, w_in, conv_w, conv_b, w_a, b_a, w_x, b_x, lru_lam,
                     a_re, a_im, b_re, b_im, c_re, c_im, d_skip, log_dt, w_glu, b_glu, w_out):
    u = _norm_proj(x2, norm_g.astype(F32)[None, :], w_in.astype(BF16),
                   jnp.ones((1, REC_IN), F32), norm_tiles=(), tn=1024)
    u3 = u.reshape(b, s, REC_IN)
    yc = _rg_lru(u3, conv_w.astype(F32), conv_b.astype(F32)[None, :],
                 w_a.astype(BF16), b_a.astype(F32)[:, None, :],
                 w_x.astype(BF16), b_x.astype(F32)[:, None, :], lru_lam.astype(F32)[None, :])

    n_chunks = s // SSM_L
    d_in = u3[:, :, 2 * LRU_WIDTH:2 * LRU_WIDTH + SSM_WIDTH]
    u5 = d_in.reshape(b, n_chunks, SSM_L, SSM_GROUPS, SSM_GROUP).transpose(3, 1, 0, 2, 4)
    u5 = u5.reshape(SSM_GROUPS, n_chunks * b, SSM_L * SSM_GROUP)
    m_mat, w_state, w_o, a_pow = _s5_operators(a_re, a_im, b_re, b_im, c_re, c_im, log_dt)
    y5 = _s5_chunked(u5, m_mat, w_state, w_o, a_pow, batch=b)
    y5 = y5.reshape(SSM_GROUPS, n_chunks, b, SSM_L, SSM_GROUP).transpose(2, 1, 3, 0, 4)
    y5 = y5.reshape(b * s, SSM_WIDTH)

    return _rec_out(yc.reshape(b * s, LRU_WIDTH), y5, u, d_skip.astype(F32)[None, :],
                    w_glu.astype(BF16), b_glu.astype(F32)[None, :], w_out.astype(BF16), x2)


def kernel(x, attn_norm_g, attn_w_in, a_q_g, a_k_g, a_rel_bias, b_q_g, b_k_g, b_lam_q1, b_lam_k1,
           b_lam_q2, b_lam_k2, b_subln_g, attn_w_out, rec_norm_g, rec_w_in, lru_conv_w, lru_conv_b,
           lru_w_a, lru_b_a, lru_w_x, lru_b_x, lru_lambda, ssm_a_re, ssm_a_im, ssm_b_re, ssm_b_im,
           ssm_c_re, ssm_c_im, ssm_d, ssm_log_dt, ssm_w_glu, ssm_b_glu, rec_w_out):
    b, s, d = x.shape
    depth = attn_norm_g.shape[0] + rec_norm_g.shape[0]
    x2 = x.reshape(b * s, d)
    for layer in range(depth):
        j = layer // 2
        if layer % 2 == 0:
            x2 = _attention_layer(x2, b, s, attn_norm_g[j], attn_w_in[j], a_q_g[j], a_k_g[j],
                                  a_rel_bias[j], b_q_g[j], b_k_g[j], b_lam_q1[j], b_lam_k1[j],
                                  b_lam_q2[j], b_lam_k2[j], b_subln_g[j], attn_w_out[j], layer)
        else:
            x2 = _recurrent_layer(x2, b, s, rec_norm_g[j], rec_w_in[j], lru_conv_w[j], lru_conv_b[j],
                                  lru_w_a[j], lru_b_a[j], lru_w_x[j], lru_b_x[j], lru_lambda[j],
                                  ssm_a_re[j], ssm_a_im[j], ssm_b_re[j], ssm_b_im[j],
                                  ssm_c_re[j], ssm_c_im[j], ssm_d[j], ssm_log_dt[j],
                                  ssm_w_glu[j], ssm_b_glu[j], rec_w_out[j])
    return x2.reshape(b, s, d)
```

```python
import functools
import math

import jax
import jax.numpy as jnp
from jax import lax
from jax.experimental import pallas as pl
from jax.experimental.pallas import tpu as pltpu

F32 = jnp.float32
BF16 = jnp.bfloat16

D_MODEL = 2048
CHUNK = 64
NEG_INF = -1e30
NORM_EPS = 1e-6

HEAD_DIM = 128
A_HEADS = 8
A_WIDTH = A_HEADS * HEAD_DIM
A_LEFT_CHUNKS = 8
A_MAX_REL = 128
B_HEADS = 4
B_V_DIM = 2 * HEAD_DIM
B_WIDTH = B_HEADS * B_V_DIM
ATTN_IN = 8 * 1024

LRU_BLOCKS = 6
LRU_BLOCK_W = 256
LRU_WIDTH = LRU_BLOCKS * LRU_BLOCK_W
CONV_W = 4
LRU_C = 8.0
SSM_GROUP = 16
SSM_GROUPS = 32
SSM_WIDTH = SSM_GROUPS * SSM_GROUP
SSM_STATE = 64
REC_IN = 2 * LRU_WIDTH + 2 * SSM_WIDTH

LANES = 128
SUBLANES = 8
MXU_N = 256
VMEM_LIMIT = 56 * 1024 * 1024

Q_BLK = 256
A_WIN = A_LEFT_CHUNKS * CHUNK + Q_BLK
SSM_L = 16
SCAN_SEGS = SUBLANES


def _sigmoid(x):
    return 0.5 * jnp.tanh(0.5 * x) + 0.5


def _cparams(sem):
    return pltpu.CompilerParams(dimension_semantics=sem, vmem_limit_bytes=VMEM_LIMIT)


def _norm_proj_body(x_ref, g_ref, w_ref, cg_ref, o_ref, h_ref, *, norm_tiles, tn):
    j = pl.program_id(1)

    @pl.when(j == 0)
    def _():
        x = x_ref[...]
        ms = jnp.mean(x * x, axis=-1, keepdims=True)
        h_ref[...] = (x * lax.rsqrt(ms + NORM_EPS) * g_ref[...]).astype(BF16)

    def step(head_norm):
        for n in range(tn // MXU_N):
            acc = jnp.dot(h_ref[...], w_ref[:, n * MXU_N:(n + 1) * MXU_N],
                          preferred_element_type=F32)
            if not head_norm:
                o_ref[:, n * MXU_N:(n + 1) * MXU_N] = acc.astype(BF16)
                continue
            for hh in range(MXU_N // HEAD_DIM):
                sl = slice(n * MXU_N + hh * HEAD_DIM, n * MXU_N + (hh + 1) * HEAD_DIM)
                blk = acc[:, hh * HEAD_DIM:(hh + 1) * HEAD_DIM]
                ms = jnp.mean(blk * blk, axis=-1, keepdims=True)
                o_ref[:, sl] = (blk * lax.rsqrt(ms + NORM_EPS) * cg_ref[:, sl]).astype(BF16)

    if norm_tiles:
        is_norm = functools.reduce(jnp.logical_or, [j == t for t in norm_tiles])
        pl.when(is_norm)(functools.partial(step, True))
        pl.when(jnp.logical_not(is_norm))(functools.partial(step, False))
    else:
        step(False)


def _norm_proj(x2, gain, w_bf16, col_gain, norm_tiles, *, tm=1024, tn=1024):
    t, d = x2.shape
    n = w_bf16.shape[1]
    body = functools.partial(_norm_proj_body, norm_tiles=tuple(norm_tiles), tn=tn)
    return pl.pallas_call(
        body,
        grid=(t // tm, n // tn),
        in_specs=[
            pl.BlockSpec((tm, d), lambda i, j: (i, 0)),
            pl.BlockSpec((1, d), lambda i, j: (0, 0)),
            pl.BlockSpec((d, tn), lambda i, j: (0, j)),
            pl.BlockSpec((1, tn), lambda i, j: (0, j)),
        ],
        out_specs=pl.BlockSpec((tm, tn), lambda i, j: (i, j)),
        out_shape=jax.ShapeDtypeStruct((t, n), BF16),
        scratch_shapes=[pltpu.VMEM((tm, d), BF16)],
        compiler_params=_cparams(("parallel", "arbitrary")),
        name="norm_proj",
    )(x2, gain, w_bf16, col_gain)


def _attn_a_body(q_ref, k_ref, v_ref, g_ref, bias_ref, o_ref, *, heads, seq):
    for hh in range(heads):
        cols = slice(hh * HEAD_DIM, (hh + 1) * HEAD_DIM)
        for i in range(seq // Q_BLK):
            rows = slice(i * Q_BLK, (i + 1) * Q_BLK)
            ks = max(0, i * Q_BLK - A_LEFT_CHUNKS * CHUNK)
            win = (i + 1) * Q_BLK - ks
            q = q_ref[rows, cols]
            k = k_ref[ks:ks + win, cols]
            v = v_ref[ks:ks + win, cols]
            s = lax.dot_general(q, k, (((1,), (1,)), ((), ())), preferred_element_type=F32)
            s = s + bias_ref[hh, :, A_WIN - win:]
            m = jnp.max(s, axis=-1, keepdims=True)
            p = jnp.exp(s - m)
            l = jnp.sum(p, axis=-1, keepdims=True)
            o = jnp.dot(p.astype(BF16), v, preferred_element_type=F32) / l
            g = g_ref[rows, cols].astype(F32)
            o_ref[rows, cols] = (o * (g * _sigmoid(g))).astype(BF16)


def _attn_a(u3, bias, *, heads_per_step=2):
    b, s, _ = u3.shape
    w = heads_per_step * HEAD_DIM
    per = A_WIDTH // w
    body = functools.partial(_attn_a_body, heads=heads_per_step, seq=s)

    def col_spec(section):
        return pl.BlockSpec((None, s, w), lambda bi, hg: (bi, 0, section * per + hg))

    return pl.pallas_call(
        body,
        grid=(b, per),
        in_specs=[
            col_spec(0), col_spec(1), col_spec(2), col_spec(3),
            pl.BlockSpec((heads_per_step, Q_BLK, A_WIN), lambda bi, hg: (hg, 0, 0)),
        ],
        out_specs=pl.BlockSpec((None, s, w), lambda bi, hg: (bi, 0, hg)),
        out_shape=jax.ShapeDtypeStruct((b, s, A_WIDTH), BF16),
        compiler_params=_cparams(("parallel", "arbitrary")),
        name="attn_a",
    )(u3, u3, u3, u3, bias)


def _attn_a_bias(rel_bias):
    h = rel_bias.shape[0]
    rb = rel_bias.astype(F32)
    n_v = Q_BLK + A_WIN - 1
    lo = A_LEFT_CHUNKS * CHUNK + Q_BLK - 1 - A_MAX_REL
    v = jnp.concatenate([jnp.broadcast_to(rb[:, :1], (h, lo + 1)), rb[:, 1:2 * A_MAX_REL],
                         jnp.broadcast_to(rb[:, -1:], (h, n_v - lo - 2 * A_MAX_REL))], axis=1)
    flat = jnp.tile(v, (1, Q_BLK + 1))[:, :Q_BLK * (n_v + 1)]
    toep = flat.reshape(h, Q_BLK, n_v + 1)[:, ::-1, :A_WIN]
    qc = jnp.arange(Q_BLK)[:, None] // CHUNK
    kc = jnp.floor_divide(jnp.arange(A_WIN)[None, :] - A_LEFT_CHUNKS * CHUNK, CHUNK)
    allowed = (kc <= qc) & (kc >= qc - A_LEFT_CHUNKS)
    return jnp.where(allowed[None], toep, NEG_INF)


def _attn_b_body(q_ref, k_ref, v_ref, g_ref, lq1_ref, lk1_ref, lq2_ref, lk2_ref, sub_ref,
                 o_ref, bias_ref, *, seq, lam_init):
    h = pl.program_id(1)
    n_blk = seq // Q_BLK
    diag0 = seq - Q_BLK

    slope = jnp.exp2((-8.0 / B_HEADS) * jnp.full((1, 1), h + 1, jnp.int32).astype(F32))
    r = lax.broadcasted_iota(jnp.int32, (Q_BLK, seq), 0)
    c = lax.broadcasted_iota(jnp.int32, (Q_BLK, seq), 1) - diag0
    dist = jnp.abs(r - c).astype(F32)
    chunk_bits = CHUNK.bit_length() - 1
    allowed = jnp.logical_or(c < 0, lax.shift_right_arithmetic(c, chunk_bits)
                             <= lax.shift_right_arithmetic(r, chunk_bits))
    bias_ref[...] = jnp.where(allowed, -slope * dist, NEG_INF)

    lam = (jnp.exp(jnp.sum(lq1_ref[...] * lk1_ref[...], axis=-1, keepdims=True))
           - jnp.exp(jnp.sum(lq2_ref[...] * lk2_ref[...], axis=-1, keepdims=True)) + lam_init)

    for i in range(n_blk):
        rows = slice(i * Q_BLK, (i + 1) * Q_BLK)
        win = (i + 1) * Q_BLK
        bias = bias_ref[:, seq - win:]
        probs = []
        for comp in range(2):
            cols = slice(comp * HEAD_DIM, (comp + 1) * HEAD_DIM)
            s = lax.dot_general(q_ref[rows, cols], k_ref[0:win, cols],
                                (((1,), (1,)), ((), ())), preferred_element_type=F32)
            s = s + bias
            m = jnp.max(s, axis=-1, keepdims=True)
            p = jnp.exp(s - m)
            probs.append(p / jnp.sum(p, axis=-1, keepdims=True))
        w = (probs[0] - lam * probs[1]).astype(BF16)
        o = jnp.dot(w, v_ref[0:win, :], preferred_element_type=F32)
        ms = jnp.mean(o * o, axis=-1, keepdims=True)
        o = o * lax.rsqrt(ms + NORM_EPS) * sub_ref[...] * (1.0 - lam_init)
        g = g_ref[rows, :].astype(F32)
        o_ref[rows, :] = (o * (g * _sigmoid(g))).astype(BF16)


def _attn_b(u3, lq1, lk1, lq2, lk2, subln_g, lam_init):
    b, s, _ = u3.shape
    per = B_WIDTH // B_V_DIM
    body = functools.partial(_attn_b_body, seq=s, lam_init=lam_init)

    def col_spec(section):
        return pl.BlockSpec((None, s, B_V_DIM), lambda bi, h: (bi, 0, section * per + h))

    def vec_spec(n):
        return pl.BlockSpec((1, n), lambda bi, h: (0, 0))

    return pl.pallas_call(
        body,
        grid=(b, B_HEADS),
        in_specs=[col_spec(4), col_spec(5), col_spec(6), col_spec(7),
                  vec_spec(HEAD_DIM), vec_spec(HEAD_DIM), vec_spec(HEAD_DIM), vec_spec(HEAD_DIM),
                  vec_spec(B_V_DIM)],
        out_specs=pl.BlockSpec((None, s, B_V_DIM), lambda bi, h: (bi, 0, h)),
        out_shape=jax.ShapeDtypeStruct((b, s, B_WIDTH), BF16),
        scratch_shapes=[pltpu.VMEM((Q_BLK, s), F32)],
        compiler_params=_cparams(("parallel", "arbitrary")),
        name="attn_b",
    )(u3, u3, u3, u3, lq1, lk1, lq2, lk2, subln_g)


def _out_proj_body(ya_ref, yb_ref, w_ref, x_ref, o_ref, *, wa):
    acc = jnp.dot(ya_ref[...], w_ref[0:wa, :], preferred_element_type=F32)
    acc = acc + jnp.dot(yb_ref[...], w_ref[wa:, :], preferred_element_type=F32)
    o_ref[...] = x_ref[...] + acc


def _out_proj(ya, yb, w_bf16, x2, *, tm=512):
    t, d = x2.shape
    wa, wb = ya.shape[1], yb.shape[1]
    return pl.pallas_call(
        functools.partial(_out_proj_body, wa=wa),
        grid=(t // tm,),
        in_specs=[
            pl.BlockSpec((tm, wa), lambda i: (i, 0)),
            pl.BlockSpec((tm, wb), lambda i: (i, 0)),
            pl.BlockSpec((wa + wb, d), lambda i: (0, 0)),
            pl.BlockSpec((tm, d), lambda i: (i, 0)),
        ],
        out_specs=pl.BlockSpec((tm, d), lambda i: (i, 0)),
        out_shape=jax.ShapeDtypeStruct((t, d), F32),
        compiler_params=_cparams(("parallel",)),
        name="out_proj",
    )(ya, yb, w_bf16, x2)


def _scan_pitch(seq):
    seg = -(-seq // SCAN_SEGS)
    return seg + (4 - seg) % SUBLANES


def _lru_body(x_ref, g_ref, cw_ref, cb_ref, wa_ref, ba_ref, wx_ref, bx_ref, lam_ref,
              o_ref, a_s, b_s, *, seq, pitch):
    n_slab = LRU_BLOCK_W // LANES
    x = x_ref[...].astype(F32)
    t_idx = lax.broadcasted_iota(jnp.int32, (seq, 1), 0)
    xc = cw_ref[CONV_W - 1:CONV_W, :] * x + cb_ref[...]
    for d in range(1, CONV_W):
        xs = jnp.where(t_idx >= d, pltpu.roll(x, d, axis=0), 0.0)
        xc = xc + cw_ref[CONV_W - 1 - d:CONV_W - d, :] * xs

    xb = xc.astype(BF16)
    r = _sigmoid(jnp.dot(xb, wa_ref[...], preferred_element_type=F32) + ba_ref[...])
    gi = _sigmoid(jnp.dot(xb, wx_ref[...], preferred_element_type=F32) + bx_ref[...])
    lam = lam_ref[...]
    softplus_neg = jnp.maximum(-lam, 0.0) + jnp.log1p(jnp.exp(-jnp.abs(lam)))
    log_a = (-LRU_C) * r * softplus_neg
    a = jnp.exp(log_a)
    mult = jnp.sqrt(1.0 - a * a)
    bb = mult * (gi * xc)

    pad = SCAN_SEGS * pitch - seq
    for sl in range(n_slab):
        cols = slice(sl * LANES, (sl + 1) * LANES)
        a_s[sl, 0:seq, :] = a[:, cols]
        b_s[sl, 0:seq, :] = bb[:, cols]
        a_s[sl, seq:, :] = jnp.zeros((pad, LANES), F32)
        b_s[sl, seq:, :] = jnp.zeros((pad, LANES), F32)

    def pass1(j, carry):
        new = []
        for sl in range(n_slab):
            hh, pp = carry[2 * sl], carry[2 * sl + 1]
            idx = pl.ds(j, SCAN_SEGS, stride=pitch)
            aj = a_s[sl, idx, :]
            bj = b_s[sl, idx, :]
            hh = aj * hh + bj
            pp = aj * pp
            b_s[sl, idx, :] = hh
            a_s[sl, idx, :] = pp
            new += [hh, pp]
        return tuple(new)

    init = []
    for sl in range(n_slab):
        init += [jnp.zeros((SCAN_SEGS, LANES), F32), jnp.ones((SCAN_SEGS, LANES), F32)]
    ends = lax.fori_loop(0, pitch, pass1, tuple(init))

    seg_idx = lax.broadcasted_iota(jnp.int32, (SCAN_SEGS, LANES), 0)
    carries = []
    for sl in range(n_slab):
        h_end, p_end = ends[2 * sl], ends[2 * sl + 1]
        c = jnp.zeros((SCAN_SEGS, LANES), F32)
        for sgm in range(1, SCAN_SEGS):
            c = jnp.where(seg_idx == sgm, pltpu.roll(h_end + p_end * c, 1, axis=0), c)
        carries.append(c)

    def pass2(j, _):
        for sl in range(n_slab):
            idx = pl.ds(j, SCAN_SEGS, stride=pitch)
            b_s[sl, idx, :] = b_s[sl, idx, :] + a_s[sl, idx, :] * carries[sl]
        return 0

    lax.fori_loop(0, pitch, pass2, 0)

    for sl in range(n_slab):
        cols = slice(sl * LANES, (sl + 1) * LANES)
        g = g_ref[:, cols].astype(F32)
        o_ref[:, cols] = (b_s[sl, 0:seq, :] * (g * _sigmoid(g))).astype(BF16)


def _rg_lru(u3, conv_w, conv_b, w_a, b_a, w_x, b_x, lam):
    b, s, _ = u3.shape
    pitch = _scan_pitch(s)
    n_slab = LRU_BLOCK_W // LANES
    body = functools.partial(_lru_body, seq=s, pitch=pitch)

    def vec_spec(rows):
        return pl.BlockSpec((rows, LRU_BLOCK_W), lambda bi, n: (0, n))

    def blk_spec(rows):
        return pl.BlockSpec((None, rows, LRU_BLOCK_W), lambda bi, n: (n, 0, 0))

    return pl.pallas_call(
        body,
        grid=(b, LRU_BLOCKS),
        in_specs=[
            pl.BlockSpec((None, s, LRU_BLOCK_W), lambda bi, n: (bi, 0, n)),
            pl.BlockSpec((None, s, LRU_BLOCK_W), lambda bi, n: (bi, 0, LRU_BLOCKS + n)),
            vec_spec(CONV_W), vec_spec(1),
            blk_spec(LRU_BLOCK_W), blk_spec(1), blk_spec(LRU_BLOCK_W), blk_spec(1),
            vec_spec(1),
        ],
        out_specs=pl.BlockSpec((None, s, LRU_BLOCK_W), lambda bi, n: (bi, 0, n)),
        out_shape=jax.ShapeDtypeStruct((b, s, LRU_WIDTH), BF16),
        scratch_shapes=[pltpu.VMEM((n_slab, SCAN_SEGS * pitch, LANES), F32),
                        pltpu.VMEM((n_slab, SCAN_SEGS * pitch, LANES), F32)],
        compiler_params=_cparams(("parallel", "arbitrary")),
        name="rg_lru",
    )(u3, u3, conv_w, conv_b, w_a, b_a, w_x, b_x, lam)


def _s5_body(u_ref, m_ref, wst_ref, wout_ref, a_ref, y_ref, e_s, h_s, *, n_chunks, batch):
    u = u_ref[...]
    e_s[...] = jnp.dot(u, wst_ref[...], preferred_element_type=F32)
    a_re = a_ref[:, 0:LANES]
    a_im = a_ref[:, LANES:]

    def step(c, carry):
        h_re, h_im = carry
        rows = pl.ds(pl.multiple_of(c * batch, batch), batch)
        h_s[rows, 0:LANES] = h_re
        h_s[rows, LANES:] = h_im
        e_re = e_s[rows, 0:LANES]
        e_im = e_s[rows, LANES:]
        return (a_re * h_re - a_im * h_im + e_re, a_re * h_im + a_im * h_re + e_im)

    zero = jnp.zeros((batch, LANES), F32)
    lax.fori_loop(0, n_chunks, step, (zero, zero))

    y = jnp.dot(u, m_ref[...], preferred_element_type=F32)
    y = y + jnp.dot(h_s[...].astype(BF16), wout_ref[...], preferred_element_type=F32)
    y_ref[...] = y


def _s5_chunked(u5, m_mat, w_state, w_out, a_pow, *, batch):
    g, rows, width = u5.shape
    body = functools.partial(_s5_body, n_chunks=rows // batch, batch=batch)

    def grp(r, c):
        return pl.BlockSpec((None, r, c), lambda gi: (gi, 0, 0))

    return pl.pallas_call(
        body,
        grid=(g,),
        in_specs=[grp(rows, width), grp(width, width), grp(width, 2 * LANES),
                  grp(2 * LANES, width), grp(1, 2 * LANES)],
        out_specs=grp(rows, width),
        out_shape=jax.ShapeDtypeStruct((g, rows, width), F32),
        scratch_shapes=[pltpu.VMEM((rows, 2 * LANES), F32), pltpu.VMEM((rows, 2 * LANES), F32)],
        compiler_params=_cparams(("parallel",)),
        name="s5_chunked",
    )(u5, m_mat, w_state, w_out, a_pow)


def _s5_operators(a_re, a_im, b_re, b_im, c_re, c_im, log_dt):
    a_re, a_im = a_re.astype(F32), a_im.astype(F32)
    dt = jnp.exp(log_dt.astype(F32))[:, None]
    tau = jnp.arange(SSM_L + 1, dtype=F32)[:, None, None]
    mag = jnp.exp(tau * (a_re * dt)[None])
    ang = tau * (a_im * dt)[None]
    pw_re, pw_im = mag * jnp.cos(ang), mag * jnp.sin(ang)
    ab_re, ab_im = pw_re[1], pw_im[1]
    den = a_re * a_re + a_im * a_im
    f_re = ((ab_re - 1.0) * a_re + ab_im * a_im) / den
    f_im = (ab_im * a_re - (ab_re - 1.0) * a_im) / den
    b_re, b_im = b_re.astype(F32), b_im.astype(F32)
    bb_re = f_re[..., None] * b_re - f_im[..., None] * b_im
    bb_im = f_re[..., None] * b_im + f_im[..., None] * b_re
    c_re, c_im = c_re.astype(F32), c_im.astype(F32)

    hp = lax.Precision.HIGHEST
    ca_re = c_re[None] * pw_re[:, :, None, :] - c_im[None] * pw_im[:, :, None, :]
    ca_im = c_re[None] * pw_im[:, :, None, :] + c_im[None] * pw_re[:, :, None, :]
    kern = (jnp.einsum('tgop,gpi->tgoi', ca_re[:SSM_L], bb_re, precision=hp)
            - jnp.einsum('tgop,gpi->tgoi', ca_im[:SSM_L], bb_im, precision=hp))
    s_idx = jnp.arange(SSM_L)[:, None]
    t_idx = jnp.arange(SSM_L)[None, :]
    lag = t_idx - s_idx
    k_st = kern[jnp.clip(lag, 0, SSM_L - 1)]
    k_st = jnp.where((lag >= 0)[:, :, None, None, None], k_st, 0.0)
    width = SSM_L * SSM_GROUP
    m_mat = k_st.transpose(2, 0, 4, 1, 3).reshape(SSM_GROUPS, width, width)

    rev_re, rev_im = pw_re[SSM_L - 1::-1][:SSM_L], pw_im[SSM_L - 1::-1][:SSM_L]
    ws_re = rev_re[..., None] * bb_re[None] - rev_im[..., None] * bb_im[None]
    ws_im = rev_re[..., None] * bb_im[None] + rev_im[..., None] * bb_re[None]
    zpad = jnp.zeros((SSM_GROUPS, width, LANES - SSM_STATE), F32)
    ws_re = ws_re.transpose(1, 0, 3, 2).reshape(SSM_GROUPS, width, SSM_STATE)
    ws_im = ws_im.transpose(1, 0, 3, 2).reshape(SSM_GROUPS, width, SSM_STATE)
    w_state = jnp.concatenate([ws_re, zpad, ws_im, zpad], axis=-1)

    wo_re = ca_re[1:].transpose(1, 3, 0, 2).reshape(SSM_GROUPS, SSM_STATE, width)
    wo_im = -ca_im[1:].transpose(1, 3, 0, 2).reshape(SSM_GROUPS, SSM_STATE, width)
    zrow = jnp.zeros((SSM_GROUPS, LANES - SSM_STATE, width), F32)
    w_out = jnp.concatenate([wo_re, zrow, wo_im, zrow], axis=1)

    zl = jnp.zeros((SSM_GROUPS, LANES - SSM_STATE), F32)
    a_pow = jnp.concatenate([pw_re[SSM_L], zl, pw_im[SSM_L], zl], axis=-1)[:, None, :]
    return m_mat.astype(BF16), w_state.astype(BF16), w_out.astype(BF16), a_pow


def _rec_out_body(yc_ref, y5_ref, din_ref, dg_ref, dskip_ref, wglu_ref, bglu_ref, w_ref, x_ref,
                  o_ref, *, wc):
    yd = y5_ref[...] + dskip_ref[...] * din_ref[...].astype(F32)
    inner = math.sqrt(2.0 / math.pi) * (yd + 0.044715 * (yd * yd * yd))
    yd = 0.5 * yd * (1.0 + jnp.tanh(inner))
    glu = jnp.dot(yd.astype(BF16), wglu_ref[...], preferred_element_type=F32) + bglu_ref[...]
    yd = yd * _sigmoid(glu)
    g = dg_ref[...].astype(F32)
    yd = (yd * (g * _sigmoid(g))).astype(BF16)
    acc = jnp.dot(yc_ref[...], w_ref[0:wc, :], preferred_element_type=F32)
    acc = acc + jnp.dot(yd, w_ref[wc:, :], preferred_element_type=F32)
    o_ref[...] = x_ref[...] + acc


def _rec_out(yc, y5, u2, d_skip, w_glu, b_glu, w_bf16, x2, *, tm=512):
    t, d = x2.shape
    wc = yc.shape[1]
    din_blk = 2 * LRU_WIDTH // SSM_WIDTH
    return pl.pallas_call(
        functools.partial(_rec_out_body, wc=wc),
        grid=(t // tm,),
        in_specs=[
            pl.BlockSpec((tm, wc), lambda i: (i, 0)),
            pl.BlockSpec((tm, SSM_WIDTH), lambda i: (i, 0)),
            pl.BlockSpec((tm, SSM_WIDTH), lambda i: (i, din_blk)),
            pl.BlockSpec((tm, SSM_WIDTH), lambda i: (i, din_blk + 1)),
            pl.BlockSpec((1, SSM_WIDTH), lambda i: (0, 0)),
            pl.BlockSpec((SSM_WIDTH, SSM_WIDTH), lambda i: (0, 0)),
            pl.BlockSpec((1, SSM_WIDTH), lambda i: (0, 0)),
            pl.BlockSpec((wc + SSM_WIDTH, d), lambda i: (0, 0)),
            pl.BlockSpec((tm, d), lambda i: (i, 0)),
        ],
        out_specs=pl.BlockSpec((tm, d), lambda i: (i, 0)),
        out_shape=jax.ShapeDtypeStruct((t, d), F32),
        compiler_params=_cparams(("parallel",)),
        name="rec_out",
    )(yc, y5, u2, u2, d_skip, w_glu, b_glu, w_bf16, x2)


def _attention_layer(x2, b, s, norm_g, w_in, q_g_a, k_g_a, rel_bias, q_g_b, k_g_b,
                     lq1, lk1, lq2, lk2, subln_g, w_out, layer_idx):
    scale = HEAD_DIM ** -0.5
    ones = jnp.ones((1024,), F32)
    col_gain = jnp.concatenate([
        jnp.tile(q_g_a.astype(F32) * scale, A_HEADS), jnp.tile(k_g_a.astype(F32), A_HEADS), ones, ones,
        jnp.tile(q_g_b.astype(F32).reshape(-1) * scale, B_HEADS),
        jnp.tile(k_g_b.astype(F32).reshape(-1), B_HEADS), ones, ones])[None, :]
    tn = 1024
    u = _norm_proj(x2, norm_g.astype(F32)[None, :], w_in.astype(BF16), col_gain,
                   norm_tiles=(0, 1, 4, 5), tn=tn)
    u3 = u.reshape(b, s, ATTN_IN)
    ya = _attn_a(u3, _attn_a_bias(rel_bias))
    lam_init = 0.8 - 0.6 * math.exp(-0.3 * layer_idx)
    row = lambda v: v.astype(F32)[None, :]
    yb = _attn_b(u3, row(lq1), row(lk1), row(lq2), row(lk2), row(subln_g), lam_init)
    return _out_proj(ya.reshape(b * s, A_WIDTH), yb.reshape(b * s, B_WIDTH), w_out.astype(BF16), x2)


def _recurrent_layer(x2, b, s, norm_g, w_in, conv_w, conv_b, w_a, b_a, w_x, b_x, lru_lam,
                     a_re, a_im, b_re, b_im, c_re, c_im, d_skip, log_dt, w_glu, b_glu, w_out):
    u = _norm_proj(x2, norm_g.astype(F32)[None, :], w_in.astype(BF16),
                   jnp.ones((1, REC_IN), F32), norm_tiles=(), tn=1024)
    u3 = u.reshape(b, s, REC_IN)
    yc = _rg_lru(u3, conv_w.astype(F32), conv_b.astype(F32)[None, :],
                 w_a.astype(BF16), b_a.astype(F32)[:, None, :],
                 w_x.astype(BF16), b_x.astype(F32)[:, None, :], lru_lam.astype(F32)[None, :])

    n_chunks = s // SSM_L
    d_in = u3[:, :, 2 * LRU_WIDTH:2 * LRU_WIDTH + SSM_WIDTH]
    u5 = d_in.reshape(b, n_chunks, SSM_L, SSM_GROUPS, SSM_GROUP).transpose(3, 1, 0, 2, 4)
    u5 = u5.reshape(SSM_GROUPS, n_chunks * b, SSM_L * SSM_GROUP)
    m_mat, w_state, w_o, a_pow = _s5_operators(a_re, a_im, b_re, b_im, c_re, c_im, log_dt)
    y5 = _s5_chunked(u5, m_mat, w_state, w_o, a_pow, batch=b)
    y5 = y5.reshape(SSM_GROUPS, n_chunks, b, SSM_L, SSM_GROUP).transpose(2, 1, 3, 0, 4)
    y5 = y5.reshape(b * s, SSM_WIDTH)

    return _rec_out(yc.reshape(b * s, LRU_WIDTH), y5, u, d_skip.astype(F32)[None, :],
                    w_glu.astype(BF16), b_glu.astype(F32)[None, :], w_out.astype(BF16), x2)


def kernel(x, attn_norm_g, attn_w_in, a_q_g, a_k_g, a_rel_bias, b_q_g, b_k_g, b_lam_q1, b_lam_k1,
           b_lam_q2, b_lam_k2, b_subln_g, attn_w_out, rec_norm_g, rec_w_in, lru_conv_w, lru_conv_b,
           lru_w_a, lru_b_a, lru_w_x, lru_b_x, lru_lambda, ssm_a_re, ssm_a_im, ssm_b_re, ssm_b_im,
           ssm_c_re, ssm_c_im, ssm_d, ssm_log_dt, ssm_w_glu, ssm_b_glu, rec_w_out):
    b, s, d = x.shape
    depth = attn_norm_g.shape[0] + rec_norm_g.shape[0]
    x2 = x.reshape(b * s, d)
    for layer in range(depth):
        j = layer // 2
        if layer % 2 == 0:
            x2 = _attention_layer(x2, b, s, attn_norm_g[j], attn_w_in[j], a_q_g[j], a_k_g[j],
                                  a_rel_bias[j], b_q_g[j], b_k_g[j], b_lam_q1[j], b_lam_k1[j],
                                  b_lam_q2[j], b_lam_k2[j], b_subln_g[j], attn_w_out[j], layer)
        else:
            x2 = _recurrent_layer(x2, b, s, rec_norm_g[j], rec_w_in[j], lru_conv_w[j], lru_conv_b[j],
                                  lru_w_a[j], lru_b_a[j], lru_w_x[j], lru_b_x[j], lru_lambda[j],
                                  ssm_a_re[j], ssm_a_im[j], ssm_b_re[j], ssm_b_im[j],
                                  ssm_c_re[j], ssm_c_im[j], ssm_d[j], ssm_log_dt[j],
                                  ssm_w_glu[j], ssm_b_glu[j], rec_w_out[j])
    return x2.reshape(b, s, d)
```

```python
import functools
import math

import jax
import jax.numpy as jnp
from jax import lax
from jax.experimental import pallas as pl
from jax.experimental.pallas import tpu as pltpu

F32 = jnp.float32
BF16 = jnp.bfloat16

D_MODEL = 2048
CHUNK = 64
NEG_INF = -1e30
NORM_EPS = 1e-6

HEAD_DIM = 128
A_HEADS = 8
A_WIDTH = A_HEADS * HEAD_DIM
A_LEFT_CHUNKS = 8
A_MAX_REL = 128
B_HEADS = 4
B_V_DIM = 2 * HEAD_DIM
B_WIDTH = B_HEADS * B_V_DIM
ATTN_IN = 8 * 1024

LRU_BLOCKS = 6
LRU_BLOCK_W = 256
LRU_WIDTH = LRU_BLOCKS * LRU_BLOCK_W
CONV_W = 4
LRU_C = 8.0
SSM_GROUP = 16
SSM_GROUPS = 32
SSM_WIDTH = SSM_GROUPS * SSM_GROUP
SSM_STATE = 64
REC_IN = 2 * LRU_WIDTH + 2 * SSM_WIDTH

LANES = 128
SUBLANES = 8
MXU_N = 256
VMEM_LIMIT = 56 * 1024 * 1024

Q_BLK = 256
A_WIN = A_LEFT_CHUNKS * CHUNK + Q_BLK
SCAN_SEGS = SUBLANES


def _sigmoid(x):
    return 0.5 * jnp.tanh(0.5 * x) + 0.5


def _cparams(sem):
    return pltpu.CompilerParams(dimension_semantics=sem, vmem_limit_bytes=VMEM_LIMIT)


def _norm_proj_body(x_ref, g_ref, w_ref, cg_ref, o_ref, h_ref, *, norm_tiles, tn):
    j = pl.program_id(1)

    @pl.when(j == 0)
    def _():
        x = x_ref[...]
        ms = jnp.mean(x * x, axis=-1, keepdims=True)
        h_ref[...] = (x * lax.rsqrt(ms + NORM_EPS) * g_ref[...]).astype(BF16)

    def step(head_norm):
        for n in range(tn // MXU_N):
            acc = jnp.dot(h_ref[...], w_ref[:, n * MXU_N:(n + 1) * MXU_N],
                          preferred_element_type=F32)
            if not head_norm:
                o_ref[:, n * MXU_N:(n + 1) * MXU_N] = acc.astype(BF16)
                continue
            for hh in range(MXU_N // HEAD_DIM):
                sl = slice(n * MXU_N + hh * HEAD_DIM, n * MXU_N + (hh + 1) * HEAD_DIM)
                blk = acc[:, hh * HEAD_DIM:(hh + 1) * HEAD_DIM]
                ms = jnp.mean(blk * blk, axis=-1, keepdims=True)
                o_ref[:, sl] = (blk * lax.rsqrt(ms + NORM_EPS) * cg_ref[:, sl]).astype(BF16)

    if norm_tiles:
        is_norm = functools.reduce(jnp.logical_or, [j == t for t in norm_tiles])
        pl.when(is_norm)(functools.partial(step, True))
        pl.when(jnp.logical_not(is_norm))(functools.partial(step, False))
    else:
        step(False)


def _norm_proj(x2, gain, w_bf16, col_gain, norm_tiles, *, tm=1024, tn=1024):
    t, d = x2.shape
    n = w_bf16.shape[1]
    body = functools.partial(_norm_proj_body, norm_tiles=tuple(norm_tiles), tn=tn)
    return pl.pallas_call(
        body,
        grid=(t // tm, n // tn),
        in_specs=[
            pl.BlockSpec((tm, d), lambda i, j: (i, 0)),
            pl.BlockSpec((1, d), lambda i, j: (0, 0)),
            pl.BlockSpec((d, tn), lambda i, j: (0, j)),
            pl.BlockSpec((1, tn), lambda i, j: (0, j)),
        ],
        out_specs=pl.BlockSpec((tm, tn), lambda i, j: (i, j)),
        out_shape=jax.ShapeDtypeStruct((t, n), BF16),
        scratch_shapes=[pltpu.VMEM((tm, d), BF16)],
        compiler_params=_cparams(("parallel", "arbitrary")),
        name="norm_proj",
    )(x2, gain, w_bf16, col_gain)


def _attn_a_body(q_ref, k_ref, v_ref, g_ref, bias_ref, o_ref, *, heads, seq):
    for hh in range(heads):
        cols = slice(hh * HEAD_DIM, (hh + 1) * HEAD_DIM)
        for i in range(seq // Q_BLK):
            rows = slice(i * Q_BLK, (i + 1) * Q_BLK)
            ks = max(0, i * Q_BLK - A_LEFT_CHUNKS * CHUNK)
            win = (i + 1) * Q_BLK - ks
            q = q_ref[rows, cols]
            k = k_ref[ks:ks + win, cols]
            v = v_ref[ks:ks + win, cols]
            s = lax.dot_general(q, k, (((1,), (1,)), ((), ())), preferred_element_type=F32)
            s = s + bias_ref[hh, :, A_WIN - win:]
            m = jnp.max(s, axis=-1, keepdims=True)
            p = jnp.exp(s - m)
            l = jnp.sum(p, axis=-1, keepdims=True)
            o = jnp.dot(p.astype(BF16), v, preferred_element_type=F32) / l
            g = g_ref[rows, cols].astype(F32)
            o_ref[rows, cols] = (o * (g * _sigmoid(g))).astype(BF16)


def _attn_a(u3, bias, *, heads_per_step=2):
    b, s, _ = u3.shape
    w = heads_per_step * HEAD_DIM
    per = A_WIDTH // w
    body = functools.partial(_attn_a_body, heads=heads_per_step, seq=s)

    def col_spec(section):
        return pl.BlockSpec((None, s, w), lambda bi, hg: (bi, 0, section * per + hg))

    return pl.pallas_call(
        body,
        grid=(b, per),
        in_specs=[
            col_spec(0), col_spec(1), col_spec(2), col_spec(3),
            pl.BlockSpec((heads_per_step, Q_BLK, A_WIN), lambda bi, hg: (hg, 0, 0)),
        ],
        out_specs=pl.BlockSpec((None, s, w), lambda bi, hg: (bi, 0, hg)),
        out_shape=jax.ShapeDtypeStruct((b, s, A_WIDTH), BF16),
        compiler_params=_cparams(("parallel", "arbitrary")),
        name="attn_a",
    )(u3, u3, u3, u3, bias)


def _attn_a_bias(rel_bias):
    h = rel_bias.shape[0]
    rb = rel_bias.astype(F32)
    n_v = Q_BLK + A_WIN - 1
    lo = A_LEFT_CHUNKS * CHUNK + Q_BLK - 1 - A_MAX_REL
    v = jnp.concatenate([jnp.broadcast_to(rb[:, :1], (h, lo + 1)), rb[:, 1:2 * A_MAX_REL],
                         jnp.broadcast_to(rb[:, -1:], (h, n_v - lo - 2 * A_MAX_REL))], axis=1)
    flat = jnp.tile(v, (1, Q_BLK + 1))[:, :Q_BLK * (n_v + 1)]
    toep = flat.reshape(h, Q_BLK, n_v + 1)[:, ::-1, :A_WIN]
    qc = jnp.arange(Q_BLK)[:, None] // CHUNK
    kc = jnp.floor_divide(jnp.arange(A_WIN)[None, :] - A_LEFT_CHUNKS * CHUNK, CHUNK)
    allowed = (kc <= qc) & (kc >= qc - A_LEFT_CHUNKS)
    return jnp.where(allowed[None], toep, NEG_INF)


def _attn_b_body(q_ref, k_ref, v_ref, g_ref, lq1_ref, lk1_ref, lq2_ref, lk2_ref, sub_ref,
                 o_ref, bias_ref, *, seq, lam_init):
    h = pl.program_id(1)
    n_blk = seq // Q_BLK
    diag0 = seq - Q_BLK

    slope = jnp.exp2((-8.0 / B_HEADS) * jnp.full((1, 1), h + 1, jnp.int32).astype(F32))
    r = lax.broadcasted_iota(jnp.int32, (Q_BLK, seq), 0)
    c = lax.broadcasted_iota(jnp.int32, (Q_BLK, seq), 1) - diag0
    dist = jnp.abs(r - c).astype(F32)
    chunk_bits = CHUNK.bit_length() - 1
    allowed = jnp.logical_or(c < 0, lax.shift_right_arithmetic(c, chunk_bits)
                             <= lax.shift_right_arithmetic(r, chunk_bits))
    bias_ref[...] = jnp.where(allowed, -slope * dist, NEG_INF)

    lam = (jnp.exp(jnp.sum(lq1_ref[...] * lk1_ref[...], axis=-1, keepdims=True))
           - jnp.exp(jnp.sum(lq2_ref[...] * lk2_ref[...], axis=-1, keepdims=True)) + lam_init)

    for i in range(n_blk):
        rows = slice(i * Q_BLK, (i + 1) * Q_BLK)
        win = (i + 1) * Q_BLK
        bias = bias_ref[:, seq - win:]
        probs = []
        for comp in range(2):
            cols = slice(comp * HEAD_DIM, (comp + 1) * HEAD_DIM)
            s = lax.dot_general(q_ref[rows, cols], k_ref[0:win, cols],
                                (((1,), (1,)), ((), ())), preferred_element_type=F32)
            s = s + bias
            m = jnp.max(s, axis=-1, keepdims=True)
            p = jnp.exp(s - m)
            probs.append(p / jnp.sum(p, axis=-1, keepdims=True))
        w = (probs[0] - lam * probs[1]).astype(BF16)
        o = jnp.dot(w, v_ref[0:win, :], preferred_element_type=F32)
        ms = jnp.mean(o * o, axis=-1, keepdims=True)
        o = o * lax.rsqrt(ms + NORM_EPS) * sub_ref[...] * (1.0 - lam_init)
        g = g_ref[rows, :].astype(F32)
        o_ref[rows, :] = (o * (g * _sigmoid(g))).astype(BF16)


def _attn_b(u3, lq1, lk1, lq2, lk2, subln_g, lam_init):
    b, s, _ = u3.shape
    per = B_WIDTH // B_V_DIM
    body = functools.partial(_attn_b_body, seq=s, lam_init=lam_init)

    def col_spec(section):
        return pl.BlockSpec((None, s, B_V_DIM), lambda bi, h: (bi, 0, section * per + h))

    def vec_spec(n):
        return pl.BlockSpec((1, n), lambda bi, h: (0, 0))

    return pl.pallas_call(
        body,
        grid=(b, B_HEADS),
        in_specs=[col_spec(4), col_spec(5), col_spec(6), col_spec(7),
                  vec_spec(HEAD_DIM), vec_spec(HEAD_DIM), vec_spec(HEAD_DIM), vec_spec(HEAD_DIM),
                  vec_spec(B_V_DIM)],
        out_specs=pl.BlockSpec((None, s, B_V_DIM), lambda bi, h: (bi, 0, h)),
        out_shape=jax.ShapeDtypeStruct((b, s, B_WIDTH), BF16),
        scratch_shapes=[pltpu.VMEM((Q_BLK, s), F32)],
        compiler_params=_cparams(("parallel", "arbitrary")),
        name="attn_b",
    )(u3, u3, u3, u3, lq1, lk1, lq2, lk2, subln_g)


def _out_proj_body(ya_ref, yb_ref, w_ref, x_ref, o_ref, *, wa):
    acc = jnp.dot(ya_ref[...], w_ref[0:wa, :], preferred_element_type=F32)
    acc = acc + jnp.dot(yb_ref[...], w_ref[wa:, :], preferred_element_type=F32)
    o_ref[...] = x_ref[...] + acc


def _out_proj(ya, yb, w_bf16, x2, *, tm=512):
    t, d = x2.shape
    wa, wb = ya.shape[1], yb.shape[1]
    return pl.pallas_call(
        functools.partial(_out_proj_body, wa=wa),
        grid=(t // tm,),
        in_specs=[
            pl.BlockSpec((tm, wa), lambda i: (i, 0)),
            pl.BlockSpec((tm, wb), lambda i: (i, 0)),
            pl.BlockSpec((wa + wb, d), lambda i: (0, 0)),
            pl.BlockSpec((tm, d), lambda i: (i, 0)),
        ],
        out_specs=pl.BlockSpec((tm, d), lambda i: (i, 0)),
        out_shape=jax.ShapeDtypeStruct((t, d), F32),
        compiler_params=_cparams(("parallel",)),
        name="out_proj",
    )(ya, yb, w_bf16, x2)


def _scan_pitch(seq):
    seg = -(-seq // SCAN_SEGS)
    return seg + (4 - seg) % SUBLANES


def _lru_body(x_ref, g_ref, cw_ref, cb_ref, wa_ref, ba_ref, wx_ref, bx_ref, lam_ref,
              o_ref, a_s, b_s, *, seq, pitch):
    n_slab = LRU_BLOCK_W // LANES
    x = x_ref[...].astype(F32)
    t_idx = lax.broadcasted_iota(jnp.int32, (seq, 1), 0)
    xc = cw_ref[CONV_W - 1:CONV_W, :] * x + cb_ref[...]
    for d in range(1, CONV_W):
        xs = jnp.where(t_idx >= d, pltpu.roll(x, d, axis=0), 0.0)
        xc = xc + cw_ref[CONV_W - 1 - d:CONV_W - d, :] * xs

    xb = xc.astype(BF16)
    r = _sigmoid(jnp.dot(xb, wa_ref[...], preferred_element_type=F32) + ba_ref[...])
    gi = _sigmoid(jnp.dot(xb, wx_ref[...], preferred_element_type=F32) + bx_ref[...])
    lam = lam_ref[...]
    softplus_neg = jnp.maximum(-lam, 0.0) + jnp.log1p(jnp.exp(-jnp.abs(lam)))
    log_a = (-LRU_C) * r * softplus_neg
    a = jnp.exp(log_a)
    mult = jnp.sqrt(1.0 - a * a)
    bb = mult * (gi * xc)

    pad = SCAN_SEGS * pitch - seq
    for sl in range(n_slab):
        cols = slice(sl * LANES, (sl + 1) * LANES)
        a_s[sl, 0:seq, :] = a[:, cols]
        b_s[sl, 0:seq, :] = bb[:, cols]
        a_s[sl, seq:, :] = jnp.zeros((pad, LANES), F32)
        b_s[sl, seq:, :] = jnp.zeros((pad, LANES), F32)

    def pass1(j, carry):
        new = []
        for sl in range(n_slab):
            hh, pp = carry[2 * sl], carry[2 * sl + 1]
            idx = pl.ds(j, SCAN_SEGS, stride=pitch)
            aj = a_s[sl, idx, :]
            bj = b_s[sl, idx, :]
            hh = aj * hh + bj
            pp = aj * pp
            b_s[sl, idx, :] = hh
            a_s[sl, idx, :] = pp
            new += [hh, pp]
        return tuple(new)

    init = []
    for sl in range(n_slab):
        init += [jnp.zeros((SCAN_SEGS, LANES), F32), jnp.ones((SCAN_SEGS, LANES), F32)]
    ends = lax.fori_loop(0, pitch, pass1, tuple(init))

    seg_idx = lax.broadcasted_iota(jnp.int32, (SCAN_SEGS, LANES), 0)
    carries = []
    for sl in range(n_slab):
        h_end, p_end = ends[2 * sl], ends[2 * sl + 1]
        c = jnp.zeros((SCAN_SEGS, LANES), F32)
        for sgm in range(1, SCAN_SEGS):
            c = jnp.where(seg_idx == sgm, pltpu.roll(h_end + p_end * c, 1, axis=0), c)
        carries.append(c)

    def pass2(j, _):
        for sl in range(n_slab):
            idx = pl.ds(j, SCAN_SEGS, stride=pitch)
            b_s[sl, idx, :] = b_s[sl, idx, :] + a_s[sl, idx, :] * carries[sl]
        return 0

    lax.fori_loop(0, pitch, pass2, 0)

    for sl in range(n_slab):
        cols = slice(sl * LANES, (sl + 1) * LANES)
        g = g_ref[:, cols].astype(F32)
        o_ref[:, cols] = (b_s[sl, 0:seq, :] * (g * _sigmoid(g))).astype(BF16)


def _rg_lru(u3, conv_w, conv_b, w_a, b_a, w_x, b_x, lam):
    b, s, _ = u3.shape
    pitch = _scan_pitch(s)
    n_slab = LRU_BLOCK_W // LANES
    body = functools.partial(_lru_body, seq=s, pitch=pitch)

    def vec_spec(rows):
        return pl.BlockSpec((rows, LRU_BLOCK_W), lambda bi, n: (0, n))

    def blk_spec(rows):
        return pl.BlockSpec((None, rows, LRU_BLOCK_W), lambda bi, n: (n, 0, 0))

    return pl.pallas_call(
        body,
        grid=(b, LRU_BLOCKS),
        in_specs=[
            pl.BlockSpec((None, s, LRU_BLOCK_W), lambda bi, n: (bi, 0, n)),
            pl.BlockSpec((None, s, LRU_BLOCK_W), lambda bi, n: (bi, 0, LRU_BLOCKS + n)),
            vec_spec(CONV_W), vec_spec(1),
            blk_spec(LRU_BLOCK_W), blk_spec(1), blk_spec(LRU_BLOCK_W), blk_spec(1),
            vec_spec(1),
        ],
        out_specs=pl.BlockSpec((None, s, LRU_BLOCK_W), lambda bi, n: (bi, 0, n)),
        out_shape=jax.ShapeDtypeStruct((b, s, LRU_WIDTH), BF16),
        scratch_shapes=[pltpu.VMEM((n_slab, SCAN_SEGS * pitch, LANES), F32),
                        pltpu.VMEM((n_slab, SCAN_SEGS * pitch, LANES), F32)],
        compiler_params=_cparams(("parallel", "arbitrary")),
        name="rg_lru",
    )(u3, u3, conv_w, conv_b, w_a, b_a, w_x, b_x, lam)


def _s5_body(u_ref, bm_ref, cm_ref, are_ref, aim_ref, dskip_ref, y_ref,
             su, ut, bre, bim, hre, him, *, tt, pitch):
    nb = SCAN_SEGS
    n_q = SSM_WIDTH // LANES
    n_state = SSM_GROUPS * SSM_STATE
    half_w = SSM_WIDTH // 2
    half_s = n_state // 2

    @pl.when(pl.program_id(1) == 0)
    def _():
        hre[...] = jnp.zeros_like(hre)
        him[...] = jnp.zeros_like(him)

    for b in range(nb):
        ub = u_ref[b].astype(F32)
        for q in range(n_q):
            su[q, b * pitch:b * pitch + tt, :] = ub[:, q * LANES:(q + 1) * LANES]

    def to_time_major(t, _):
        rows = pl.ds(pl.multiple_of(t * nb, nb), nb)
        for q in range(n_q):
            ut[rows, q * LANES:(q + 1) * LANES] = su[q, pl.ds(t, nb, stride=pitch), :]
        return 0

    lax.fori_loop(0, tt, to_time_major, 0)

    u_bf = ut[...].astype(BF16)
    for hf in range(2):
        uh = u_bf[:, hf * half_w:(hf + 1) * half_w]
        cols = slice(hf * half_s, (hf + 1) * half_s)
        bre[:, cols] = jnp.dot(uh, bm_ref[hf, :, 0:half_s], preferred_element_type=F32)
        bim[:, cols] = jnp.dot(uh, bm_ref[hf, :, half_s:], preferred_element_type=F32)

    group = 4
    for k0 in range(0, n_state // LANES, group):
        sl = [slice((k0 + k) * LANES, (k0 + k + 1) * LANES) for k in range(group)]
        a_r = [jnp.broadcast_to(are_ref[:, c], (nb, LANES)) for c in sl]
        a_i = [jnp.broadcast_to(aim_ref[:, c], (nb, LANES)) for c in sl]

        def scan(t, carry):
            rows = pl.ds(pl.multiple_of(t * nb, nb), nb)
            out = []
            for k in range(group):
                x_r, x_i = carry[2 * k], carry[2 * k + 1]
                n_r = a_r[k] * x_r - a_i[k] * x_i + bre[rows, sl[k]]
                n_i = a_r[k] * x_i + a_i[k] * x_r + bim[rows, sl[k]]
                bre[rows, sl[k]] = n_r
                bim[rows, sl[k]] = n_i
                out += [n_r, n_i]
            return tuple(out)

        init = []
        for k in range(group):
            init += [hre[:, sl[k]], him[:, sl[k]]]
        last = lax.fori_loop(0, tt, scan, tuple(init), unroll=2)
        for k in range(group):
            hre[:, sl[k]] = last[2 * k]
            him[:, sl[k]] = last[2 * k + 1]

    for hf in range(2):
        cols = slice(hf * half_s, (hf + 1) * half_s)
        y = jnp.dot(bre[:, cols].astype(BF16), cm_ref[hf, 0:half_s, :], preferred_element_type=F32)
        y = y + jnp.dot(bim[:, cols].astype(BF16), cm_ref[hf, half_s:, :], preferred_element_type=F32)
        oc = slice(hf * half_w, (hf + 1) * half_w)
        ut[:, oc] = y + dskip_ref[:, oc] * ut[:, oc]

    def to_batch_major(t, _):
        rows = pl.ds(pl.multiple_of(t * nb, nb), nb)
        for q in range(n_q):
            su[q, pl.ds(t, nb, stride=pitch), :] = ut[rows, q * LANES:(q + 1) * LANES]
        return 0

    lax.fori_loop(0, tt, to_batch_major, 0)
    for b in range(nb):
        for q in range(n_q):
            y_ref[b, :, q * LANES:(q + 1) * LANES] = su[q, b * pitch:b * pitch + tt, :]


def _s5_scan(u3, b_mat, c_mat, a_re, a_im, d_skip, *, tt=128):
    b, s, _ = u3.shape
    nb = SCAN_SEGS
    pitch = tt + 4
    n_state = SSM_GROUPS * SSM_STATE
    din_blk = 2 * LRU_WIDTH // SSM_WIDTH
    body = functools.partial(_s5_body, tt=tt, pitch=pitch)
    full = lambda shape: pl.BlockSpec(shape, lambda bi, ti: (0,) * len(shape))
    return pl.pallas_call(
        body,
        grid=(b // nb, s // tt),
        in_specs=[
            pl.BlockSpec((nb, tt, SSM_WIDTH), lambda bi, ti: (bi, ti, din_blk)),
            full(b_mat.shape), full(c_mat.shape), full((1, n_state)), full((1, n_state)),
            full((1, SSM_WIDTH)),
        ],
        out_specs=pl.BlockSpec((nb, tt, SSM_WIDTH), lambda bi, ti: (bi, ti, 0)),
        out_shape=jax.ShapeDtypeStruct((b, s, SSM_WIDTH), F32),
        scratch_shapes=[
            pltpu.VMEM((SSM_WIDTH // LANES, nb * pitch, LANES), F32),
            pltpu.VMEM((tt * nb, SSM_WIDTH), F32),
            pltpu.VMEM((tt * nb, n_state), F32),
            pltpu.VMEM((tt * nb, n_state), F32),
            pltpu.VMEM((nb, n_state), F32),
            pltpu.VMEM((nb, n_state), F32),
        ],
        compiler_params=_cparams(("parallel", "arbitrary")),
        name="s5_scan",
    )(u3, b_mat, c_mat, a_re, a_im, d_skip)


def _s5_operators(a_re, a_im, b_re, b_im, c_re, c_im, log_dt):
    a_re, a_im = a_re.astype(F32), a_im.astype(F32)
    dt = jnp.exp(log_dt.astype(F32))[:, None]
    mag = jnp.exp(a_re * dt)
    ab_re, ab_im = mag * jnp.cos(a_im * dt), mag * jnp.sin(a_im * dt)
    den = a_re * a_re + a_im * a_im
    f_re = ((ab_re - 1.0) * a_re + ab_im * a_im) / den
    f_im = (ab_im * a_re - (ab_re - 1.0) * a_im) / den
    b_re, b_im = b_re.astype(F32), b_im.astype(F32)
    bb_re = f_re[..., None] * b_re - f_im[..., None] * b_im
    bb_im = f_re[..., None] * b_im + f_im[..., None] * b_re

    gh = SSM_GROUPS // 2
    eye = jnp.eye(gh, dtype=F32)

    def in_map(bb):
        t = bb.reshape(2, gh, SSM_STATE, SSM_GROUP)
        t = t.transpose(0, 1, 3, 2)[:, :, :, None, :] * eye[None, :, None, :, None]
        return t.reshape(2, gh * SSM_GROUP, gh * SSM_STATE)

    def out_map(cc):
        t = cc.reshape(2, gh, SSM_GROUP, SSM_STATE)
        t = t.transpose(0, 1, 3, 2)[:, :, :, None, :] * eye[None, :, None, :, None]
        return t.reshape(2, gh * SSM_STATE, gh * SSM_GROUP)

    b_mat = jnp.concatenate([in_map(bb_re), in_map(bb_im)], axis=2).astype(BF16)
    c_mat = jnp.concatenate([out_map(c_re.astype(F32)), out_map(-c_im.astype(F32))],
                            axis=1).astype(BF16)
    n_state = SSM_GROUPS * SSM_STATE
    return b_mat, c_mat, ab_re.reshape(1, n_state), ab_im.reshape(1, n_state)


def _rec_out_body(yc_ref, y5_ref, dg_ref, wglu_ref, bglu_ref, w_ref, x_ref, o_ref, *, wc):
    yd = y5_ref[...]
    inner = math.sqrt(2.0 / math.pi) * (yd + 0.044715 * (yd * yd * yd))
    yd = 0.5 * yd * (1.0 + jnp.tanh(inner))
    glu = jnp.dot(yd.astype(BF16), wglu_ref[...], preferred_element_type=F32) + bglu_ref[...]
    yd = yd * _sigmoid(glu)
    g = dg_ref[...].astype(F32)
    yd = (yd * (g * _sigmoid(g))).astype(BF16)
    acc = jnp.dot(yc_ref[...], w_ref[0:wc, :], preferred_element_type=F32)
    acc = acc + jnp.dot(yd, w_ref[wc:, :], preferred_element_type=F32)
    o_ref[...] = x_ref[...] + acc


def _rec_out(yc, y5, u2, w_glu, b_glu, w_bf16, x2, *, tm=512):
    t, d = x2.shape
    wc = yc.shape[1]
    gate_blk = 2 * LRU_WIDTH // SSM_WIDTH + 1
    return pl.pallas_call(
        functools.partial(_rec_out_body, wc=wc),
        grid=(t // tm,),
        in_specs=[
            pl.BlockSpec((tm, wc), lambda i: (i, 0)),
            pl.BlockSpec((tm, SSM_WIDTH), lambda i: (i, 0)),
            pl.BlockSpec((tm, SSM_WIDTH), lambda i: (i, gate_blk)),
            pl.BlockSpec((SSM_WIDTH, SSM_WIDTH), lambda i: (0, 0)),
            pl.BlockSpec((1, SSM_WIDTH), lambda i: (0, 0)),
            pl.BlockSpec((wc + SSM_WIDTH, d), lambda i: (0, 0)),
            pl.BlockSpec((tm, d), lambda i: (i, 0)),
        ],
        out_specs=pl.BlockSpec((tm, d), lambda i: (i, 0)),
        out_shape=jax.ShapeDtypeStruct((t, d), F32),
        compiler_params=_cparams(("parallel",)),
        name="rec_out",
    )(yc, y5, u2, w_glu, b_glu, w_bf16, x2)


def _attention_layer(x2, b, s, norm_g, w_in, q_g_a, k_g_a, rel_bias, q_g_b, k_g_b,
                     lq1, lk1, lq2, lk2, subln_g, w_out, layer_idx):
    scale = HEAD_DIM ** -0.5
    ones = jnp.ones((1024,), F32)
    col_gain = jnp.concatenate([
        jnp.tile(q_g_a.astype(F32) * scale, A_HEADS), jnp.tile(k_g_a.astype(F32), A_HEADS), ones, ones,
        jnp.tile(q_g_b.astype(F32).reshape(-1) * scale, B_HEADS),
        jnp.tile(k_g_b.astype(F32).reshape(-1), B_HEADS), ones, ones])[None, :]
    tn = 1024
    u = _norm_proj(x2, norm_g.astype(F32)[None, :], w_in.astype(BF16), col_gain,
                   norm_tiles=(0, 1, 4, 5), tn=tn)
    u3 = u.reshape(b, s, ATTN_IN)
    ya = _attn_a(u3, _attn_a_bias(rel_bias))
    lam_init = 0.8 - 0.6 * math.exp(-0.3 * layer_idx)
    row = lambda v: v.astype(F32)[None, :]
    yb = _attn_b(u3, row(lq1), row(lk1), row(lq2), row(lk2), row(subln_g), lam_init)
    return _out_proj(ya.reshape(b * s, A_WIDTH), yb.reshape(b * s, B_WIDTH), w_out.astype(BF16), x2)


def _recurrent_layer(x2, b, s, norm_g, w_in, conv_w, conv_b, w_a, b_a, w_x, b_x, lru_lam,
                     a_re, a_im, b_re, b_im, c_re, c_im, d_skip, log_dt, w_glu, b_glu, w_out):
    u = _norm_proj(x2, norm_g.astype(F32)[None, :], w_in.astype(BF16),
                   jnp.ones((1, REC_IN), F32), norm_tiles=(), tn=1024)
    u3 = u.reshape(b, s, REC_IN)
    yc = _rg_lru(u3, conv_w.astype(F32), conv_b.astype(F32)[None, :],
                 w_a.astype(BF16), b_a.astype(F32)[:, None, :],
                 w_x.astype(BF16), b_x.astype(F32)[:, None, :], lru_lam.astype(F32)[None, :])

    b_mat, c_mat, ab_re, ab_im = _s5_operators(a_re, a_im, b_re, b_im, c_re, c_im, log_dt)
    y5 = _s5_scan(u3, b_mat, c_mat, ab_re, ab_im, d_skip.astype(F32)[None, :])
    return _rec_out(yc.reshape(b * s, LRU_WIDTH), y5.reshape(b * s, SSM_WIDTH), u,
                    w_glu.astype(BF16), b_glu.astype(F32)[None, :], w_out.astype(BF16), x2)


def kernel(x, attn_norm_g, attn_w_in, a_q_g, a_k_g, a_rel_bias, b_q_g, b_k_g, b_lam_q1, b_lam_k1,
           b_lam_q2, b_lam_k2, b_subln_g, attn_w_out, rec_norm_g, rec_w_in, lru_conv_w, lru_conv_b,
           lru_w_a, lru_b_a, lru_w_x, lru_b_x, lru_lambda, ssm_a_re, ssm_a_im, ssm_b_re, ssm_b_im,
           ssm_c_re, ssm_c_im, ssm_d, ssm_log_dt, ssm_w_glu, ssm_b_glu, rec_w_out):
    b, s, d = x.shape
    depth = attn_norm_g.shape[0] + rec_norm_g.shape[0]
    x2 = x.reshape(b * s, d)
    for layer in range(depth):
        j = layer // 2
        if layer % 2 == 0:
            x2 = _attention_layer(x2, b, s, attn_norm_g[j], attn_w_in[j], a_q_g[j], a_k_g[j],
                                  a_rel_bias[j], b_q_g[j], b_k_g[j], b_lam_q1[j], b_lam_k1[j],
                                  b_lam_q2[j], b_lam_k2[j], b_subln_g[j], attn_w_out[j], layer)
        else:
            x2 = _recurrent_layer(x2, b, s, rec_norm_g[j], rec_w_in[j], lru_conv_w[j], lru_conv_b[j],
                                  lru_w_a[j], lru_b_a[j], lru_w_x[j], lru_b_x[j], lru_lambda[j],
                                  ssm_a_re[j], ssm_a_im[j], ssm_b_re[j], ssm_b_im[j],
                                  ssm_c_re[j], ssm_c_im[j], ssm_d[j], ssm_log_dt[j],
                                  ssm_w_glu[j], ssm_b_glu[j], rec_w_out[j])
    return x2.reshape(b, s, d)
```

```python
import functools
import math

import jax
import jax.numpy as jnp
from jax import lax
from jax.experimental import pallas as pl
from jax.experimental.pallas import tpu as pltpu

F32 = jnp.float32
BF16 = jnp.bfloat16

D_MODEL = 2048
CHUNK = 64
NEG_INF = -1e30
NORM_EPS = 1e-6

HEAD_DIM = 128
A_HEADS = 8
A_WIDTH = A_HEADS * HEAD_DIM
A_LEFT_CHUNKS = 8
A_MAX_REL = 128
B_HEADS = 4
B_V_DIM = 2 * HEAD_DIM
B_WIDTH = B_HEADS * B_V_DIM
ATTN_IN = 8 * 1024

LRU_BLOCKS = 6
LRU_BLOCK_W = 256
LRU_WIDTH = LRU_BLOCKS * LRU_BLOCK_W
CONV_W = 4
LRU_C = 8.0
SSM_GROUP = 16
SSM_GROUPS = 32
SSM_WIDTH = SSM_GROUPS * SSM_GROUP
SSM_STATE = 64
REC_IN = 2 * LRU_WIDTH + 2 * SSM_WIDTH

LANES = 128
SUBLANES = 8
MXU_N = 256
VMEM_LIMIT = 56 * 1024 * 1024

Q_BLK = 256
K_TILE = 256
LOG2E = 1.0 / math.log(2.0)
A_WIN = A_LEFT_CHUNKS * CHUNK + Q_BLK
SCAN_SEGS = SUBLANES


def _sigmoid(x):
    return 0.5 * jnp.tanh(0.5 * x) + 0.5


def _cparams(sem):
    return pltpu.CompilerParams(dimension_semantics=sem, vmem_limit_bytes=VMEM_LIMIT)


def _norm_proj_body(x_ref, g_ref, w_ref, cg_ref, o_ref, h_ref, *, norm_tiles, tn):
    j = pl.program_id(1)

    @pl.when(j == 0)
    def _():
        x = x_ref[...]
        ms = jnp.mean(x * x, axis=-1, keepdims=True)
        h_ref[...] = (x * lax.rsqrt(ms + NORM_EPS) * g_ref[...]).astype(BF16)

    def step(head_norm):
        for n in range(tn // MXU_N):
            acc = jnp.dot(h_ref[...], w_ref[:, n * MXU_N:(n + 1) * MXU_N],
                          preferred_element_type=F32)
            if not head_norm:
                o_ref[:, n * MXU_N:(n + 1) * MXU_N] = acc.astype(BF16)
                continue
            for hh in range(MXU_N // HEAD_DIM):
                sl = slice(n * MXU_N + hh * HEAD_DIM, n * MXU_N + (hh + 1) * HEAD_DIM)
                blk = acc[:, hh * HEAD_DIM:(hh + 1) * HEAD_DIM]
                ms = jnp.mean(blk * blk, axis=-1, keepdims=True)
                o_ref[:, sl] = (blk * lax.rsqrt(ms + NORM_EPS) * cg_ref[:, sl]).astype(BF16)

    if norm_tiles:
        is_norm = functools.reduce(jnp.logical_or, [j == t for t in norm_tiles])
        pl.when(is_norm)(functools.partial(step, True))
        pl.when(jnp.logical_not(is_norm))(functools.partial(step, False))
    else:
        step(False)


def _norm_proj(x2, gain, w_bf16, col_gain, norm_tiles, *, tm=1024, tn=1024):
    t, d = x2.shape
    n = w_bf16.shape[1]
    body = functools.partial(_norm_proj_body, norm_tiles=tuple(norm_tiles), tn=tn)
    return pl.pallas_call(
        body,
        grid=(t // tm, n // tn),
        in_specs=[
            pl.BlockSpec((tm, d), lambda i, j: (i, 0)),
            pl.BlockSpec((1, d), lambda i, j: (0, 0)),
            pl.BlockSpec((d, tn), lambda i, j: (0, j)),
            pl.BlockSpec((1, tn), lambda i, j: (0, j)),
        ],
        out_specs=pl.BlockSpec((tm, tn), lambda i, j: (i, j)),
        out_shape=jax.ShapeDtypeStruct((t, n), BF16),
        scratch_shapes=[pltpu.VMEM((tm, d), BF16)],
        compiler_params=_cparams(("parallel", "arbitrary")),
        name="norm_proj",
    )(x2, gain, w_bf16, col_gain)


def _softmax_tiles(logits_fn, n_t, s_scr, p_scr, m_scr, l_scr):
    half = K_TILE // 2
    for t in range(n_t):
        s = logits_fn(t)
        s_scr[:, t * K_TILE:(t + 1) * K_TILE] = s
        hm = jnp.maximum(s[:, 0:half], s[:, half:])
        m_scr[...] = hm if t == 0 else jnp.maximum(m_scr[...], hm)
    m = jnp.max(m_scr[...], axis=-1, keepdims=True)
    m_scr[...] = jnp.broadcast_to(m, m_scr.shape)
    for t in range(n_t):
        lo = slice(t * K_TILE, t * K_TILE + half)
        hi = slice(t * K_TILE + half, (t + 1) * K_TILE)
        p_lo = jnp.exp2(s_scr[:, lo] - m_scr[...])
        p_hi = jnp.exp2(s_scr[:, hi] - m_scr[...])
        if l_scr is not None:
            l_scr[...] = (p_lo + p_hi) if t == 0 else l_scr[...] + (p_lo + p_hi)
        p_scr[:, lo] = p_lo.astype(BF16)
        p_scr[:, hi] = p_hi.astype(BF16)
    if l_scr is None:
        return None
    return jnp.sum(l_scr[...], axis=-1, keepdims=True)


def _attn_a_body(q_ref, k_ref, v_ref, g_ref, bias_ref, o_ref, s_scr, p_scr, m_scr, v1_scr, *,
                 heads, seq):
    for hh in range(heads):
        cols = slice(hh * HEAD_DIM, (hh + 1) * HEAD_DIM)
        v1_scr[hh, :, 0:HEAD_DIM] = v_ref[:, cols]
        v1_scr[hh, :, HEAD_DIM:] = jnp.ones((seq, HEAD_DIM), BF16)
        for i in range(seq // Q_BLK):
            rows = slice(i * Q_BLK, (i + 1) * Q_BLK)
            ks = max(0, i * Q_BLK - A_LEFT_CHUNKS * CHUNK)
            win = (i + 1) * Q_BLK - ks
            q = q_ref[rows, cols]

            def logits(t, q=q, ks=ks, win=win, hh=hh, cols=cols):
                kt = k_ref[ks + t * K_TILE:ks + (t + 1) * K_TILE, cols]
                b0 = A_WIN - win + t * K_TILE
                return (lax.dot_general(q, kt, (((1,), (1,)), ((), ())), preferred_element_type=F32)
                        + bias_ref[hh, :, b0:b0 + K_TILE])

            _softmax_tiles(logits, win // K_TILE, s_scr, p_scr, m_scr, None)
            ol = jnp.dot(p_scr[:, 0:win], v1_scr[hh, ks:ks + win, :], preferred_element_type=F32)
            o = ol[:, 0:HEAD_DIM] / ol[:, HEAD_DIM:HEAD_DIM + 1]
            g = g_ref[rows, cols].astype(F32)
            o_ref[rows, cols] = (o * (g * _sigmoid(g))).astype(BF16)


def _attn_a(u3, bias, *, heads_per_step=2):
    b, s, _ = u3.shape
    w = heads_per_step * HEAD_DIM
    per = A_WIDTH // w
    body = functools.partial(_attn_a_body, heads=heads_per_step, seq=s)

    def col_spec(section):
        return pl.BlockSpec((None, s, w), lambda bi, hg: (bi, 0, section * per + hg))

    return pl.pallas_call(
        body,
        grid=(b, per),
        in_specs=[
            col_spec(0), col_spec(1), col_spec(2), col_spec(3),
            pl.BlockSpec((heads_per_step, Q_BLK, A_WIN), lambda bi, hg: (hg, 0, 0)),
        ],
        out_specs=pl.BlockSpec((None, s, w), lambda bi, hg: (bi, 0, hg)),
        out_shape=jax.ShapeDtypeStruct((b, s, A_WIDTH), BF16),
        scratch_shapes=[pltpu.VMEM((Q_BLK, A_WIN), F32), pltpu.VMEM((Q_BLK, A_WIN), BF16),
                        pltpu.VMEM((Q_BLK, K_TILE // 2), F32),
                        pltpu.VMEM((heads_per_step, s, 2 * HEAD_DIM), BF16)],
        compiler_params=_cparams(("parallel", "arbitrary")),
        name="attn_a",
    )(u3, u3, u3, u3, bias)


def _attn_a_bias(rel_bias):
    h = rel_bias.shape[0]
    rb = rel_bias.astype(F32)
    n_v = Q_BLK + A_WIN - 1
    lo = A_LEFT_CHUNKS * CHUNK + Q_BLK - 1 - A_MAX_REL
    v = jnp.concatenate([jnp.broadcast_to(rb[:, :1], (h, lo + 1)), rb[:, 1:2 * A_MAX_REL],
                         jnp.broadcast_to(rb[:, -1:], (h, n_v - lo - 2 * A_MAX_REL))], axis=1)
    flat = jnp.tile(v, (1, Q_BLK + 1))[:, :Q_BLK * (n_v + 1)]
    toep = flat.reshape(h, Q_BLK, n_v + 1)[:, ::-1, :A_WIN]
    qc = jnp.arange(Q_BLK)[:, None] // CHUNK
    kc = jnp.floor_divide(jnp.arange(A_WIN)[None, :] - A_LEFT_CHUNKS * CHUNK, CHUNK)
    allowed = (kc <= qc) & (kc >= qc - A_LEFT_CHUNKS)
    return jnp.where(allowed[None], toep * LOG2E, NEG_INF)


def _attn_b_body(q_ref, k_ref, v_ref, g_ref, lq1_ref, lk1_ref, lq2_ref, lk2_ref, sub_ref,
                 o_ref, bias_ref, s_scr, p_scr, m_scr, l_scr, *, seq, lam_init):
    h = pl.program_id(1)
    n_blk = seq // Q_BLK
    diag0 = seq - Q_BLK

    slope = jnp.exp2((-8.0 / B_HEADS) * jnp.full((1, 1), h + 1, jnp.int32).astype(F32))
    r = lax.broadcasted_iota(jnp.int32, (Q_BLK, seq), 0)
    c = lax.broadcasted_iota(jnp.int32, (Q_BLK, seq), 1) - diag0
    dist = jnp.abs(r - c).astype(F32)
    chunk_bits = CHUNK.bit_length() - 1
    allowed = jnp.logical_or(c < 0, lax.shift_right_arithmetic(c, chunk_bits)
                             <= lax.shift_right_arithmetic(r, chunk_bits))
    bias_ref[...] = jnp.where(allowed, (-LOG2E) * slope * dist, NEG_INF)

    lam = (jnp.exp(jnp.sum(lq1_ref[...] * lk1_ref[...], axis=-1, keepdims=True))
           - jnp.exp(jnp.sum(lq2_ref[...] * lk2_ref[...], axis=-1, keepdims=True)) + lam_init)

    for i in range(n_blk):
        rows = slice(i * Q_BLK, (i + 1) * Q_BLK)
        n_t = (i + 1) * Q_BLK // K_TILE
        win = n_t * K_TILE
        outs = []
        for comp in range(2):
            cols = slice(comp * HEAD_DIM, (comp + 1) * HEAD_DIM)
            q = q_ref[rows, cols]

            def logits(t, q=q, win=win, cols=cols):
                b0 = seq - win + t * K_TILE
                return (lax.dot_general(q, k_ref[t * K_TILE:(t + 1) * K_TILE, cols],
                                        (((1,), (1,)), ((), ())), preferred_element_type=F32)
                        + bias_ref[:, b0:b0 + K_TILE])

            l = _softmax_tiles(logits, n_t, s_scr, p_scr, m_scr, l_scr)
            pv = jnp.dot(p_scr[:, 0:win], v_ref[0:win, :], preferred_element_type=F32)
            outs.append(pv * ((1.0 if comp == 0 else lam) / l))
        o = outs[0] - outs[1]
        ms = jnp.mean(o * o, axis=-1, keepdims=True)
        o = o * lax.rsqrt(ms + NORM_EPS) * sub_ref[...] * (1.0 - lam_init)
        g = g_ref[rows, :].astype(F32)
        o_ref[rows, :] = (o * (g * _sigmoid(g))).astype(BF16)


def _attn_b(u3, lq1, lk1, lq2, lk2, subln_g, lam_init):
    b, s, _ = u3.shape
    per = B_WIDTH // B_V_DIM
    body = functools.partial(_attn_b_body, seq=s, lam_init=lam_init)

    def col_spec(section):
        return pl.BlockSpec((None, s, B_V_DIM), lambda bi, h: (bi, 0, section * per + h))

    def vec_spec(n):
        return pl.BlockSpec((1, n), lambda bi, h: (0, 0))

    return pl.pallas_call(
        body,
        grid=(b, B_HEADS),
        in_specs=[col_spec(4), col_spec(5), col_spec(6), col_spec(7),
                  vec_spec(HEAD_DIM), vec_spec(HEAD_DIM), vec_spec(HEAD_DIM), vec_spec(HEAD_DIM),
                  vec_spec(B_V_DIM)],
        out_specs=pl.BlockSpec((None, s, B_V_DIM), lambda bi, h: (bi, 0, h)),
        out_shape=jax.ShapeDtypeStruct((b, s, B_WIDTH), BF16),
        scratch_shapes=[pltpu.VMEM((Q_BLK, s), F32), pltpu.VMEM((Q_BLK, s), F32),
                        pltpu.VMEM((Q_BLK, s), BF16), pltpu.VMEM((Q_BLK, K_TILE // 2), F32),
                        pltpu.VMEM((Q_BLK, K_TILE // 2), F32)],
        compiler_params=_cparams(("parallel", "arbitrary")),
        name="attn_b",
    )(u3, u3, u3, u3, lq1, lk1, lq2, lk2, subln_g)


def _out_proj_body(ya_ref, yb_ref, w_ref, x_ref, o_ref, *, wa):
    acc = jnp.dot(ya_ref[...], w_ref[0:wa, :], preferred_element_type=F32)
    acc = acc + jnp.dot(yb_ref[...], w_ref[wa:, :], preferred_element_type=F32)
    o_ref[...] = x_ref[...] + acc


def _out_proj(ya, yb, w_bf16, x2, *, tm=512):
    t, d = x2.shape
    wa, wb = ya.shape[1], yb.shape[1]
    return pl.pallas_call(
        functools.partial(_out_proj_body, wa=wa),
        grid=(t // tm,),
        in_specs=[
            pl.BlockSpec((tm, wa), lambda i: (i, 0)),
            pl.BlockSpec((tm, wb), lambda i: (i, 0)),
            pl.BlockSpec((wa + wb, d), lambda i: (0, 0)),
            pl.BlockSpec((tm, d), lambda i: (i, 0)),
        ],
        out_specs=pl.BlockSpec((tm, d), lambda i: (i, 0)),
        out_shape=jax.ShapeDtypeStruct((t, d), F32),
        compiler_params=_cparams(("parallel",)),
        name="out_proj",
    )(ya, yb, w_bf16, x2)


def _scan_pitch(seq):
    seg = -(-seq // SCAN_SEGS)
    return seg + (4 - seg) % SUBLANES


def _lru_body(x_ref, g_ref, cw_ref, cb_ref, wa_ref, ba_ref, wx_ref, bx_ref, lam_ref,
              o_ref, x_s, a_s, b_s, *, seq, pitch, nblk):
    n_slab = nblk * LRU_BLOCK_W // LANES
    slab_per_blk = LRU_BLOCK_W // LANES
    pad = SCAN_SEGS * pitch - seq

    for sl in range(n_slab):
        x_s[sl, 0:SUBLANES, :] = jnp.zeros((SUBLANES, LANES), F32)
        x_s[sl, SUBLANES:, :] = x_ref[:, sl * LANES:(sl + 1) * LANES].astype(F32)

    for blk in range(nblk):
        parts = []
        for half in range(slab_per_blk):
            sl = blk * slab_per_blk + half
            cols = slice(sl * LANES, (sl + 1) * LANES)
            xc = cb_ref[:, cols] + cw_ref[CONV_W - 1:CONV_W, cols] * x_s[sl, SUBLANES:, :]
            for d in range(1, CONV_W):
                xc = xc + (cw_ref[CONV_W - 1 - d:CONV_W - d, cols]
                           * x_s[sl, SUBLANES - d:SUBLANES - d + seq, :])
            parts.append(xc)
        xc = jnp.concatenate(parts, axis=1)
        bcols = slice(blk * LRU_BLOCK_W, (blk + 1) * LRU_BLOCK_W)

        xb = xc.astype(BF16)
        t_r = jnp.tanh(jnp.dot(xb, wa_ref[blk], preferred_element_type=F32) + ba_ref[blk])
        t_i = jnp.tanh(jnp.dot(xb, wx_ref[blk], preferred_element_type=F32) + bx_ref[blk])
        gi = 0.5 * t_i + 0.5
        lam = lam_ref[:, bcols]
        softplus_neg = jnp.maximum(-lam, 0.0) + jnp.log1p(jnp.exp(-jnp.abs(lam)))
        half_rate = (-0.5 * LRU_C * LOG2E) * softplus_neg
        a = jnp.exp2(t_r * half_rate + half_rate)
        y = 1.0 - a * a
        mult = jnp.where(y > 0.0, y * lax.rsqrt(y), 0.0)
        bb = mult * (gi * xc)
        for half in range(slab_per_blk):
            sl = blk * slab_per_blk + half
            cols = slice(half * LANES, (half + 1) * LANES)
            a_s[sl, 0:seq, :] = a[:, cols]
            b_s[sl, 0:seq, :] = bb[:, cols]
            a_s[sl, seq:, :] = jnp.zeros((pad, LANES), F32)
            b_s[sl, seq:, :] = jnp.zeros((pad, LANES), F32)

    def pass1(j, carry):
        new = []
        for sl in range(n_slab):
            hh, pp = carry[2 * sl], carry[2 * sl + 1]
            idx = pl.ds(j, SCAN_SEGS, stride=pitch)
            aj = a_s[sl, idx, :]
            bj = b_s[sl, idx, :]
            hh = aj * hh + bj
            pp = aj * pp
            b_s[sl, idx, :] = hh
            a_s[sl, idx, :] = pp
            new += [hh, pp]
        return tuple(new)

    init = []
    for sl in range(n_slab):
        init += [jnp.zeros((SCAN_SEGS, LANES), F32), jnp.ones((SCAN_SEGS, LANES), F32)]
    ends = lax.fori_loop(0, pitch, pass1, tuple(init), unroll=2)

    seg_idx = lax.broadcasted_iota(jnp.int32, (SCAN_SEGS, LANES), 0)
    carries = []
    for sl in range(n_slab):
        h_end, p_end = ends[2 * sl], ends[2 * sl + 1]
        c = jnp.zeros((SCAN_SEGS, LANES), F32)
        for sgm in range(1, SCAN_SEGS):
            c = jnp.where(seg_idx == sgm, pltpu.roll(h_end + p_end * c, 1, axis=0), c)
        carries.append(c)

    def pass2(j, _):
        for sl in range(n_slab):
            idx = pl.ds(j, SCAN_SEGS, stride=pitch)
            b_s[sl, idx, :] = b_s[sl, idx, :] + a_s[sl, idx, :] * carries[sl]
        return 0

    lax.fori_loop(0, pitch, pass2, 0, unroll=4)

    for sl in range(n_slab):
        cols = slice(sl * LANES, (sl + 1) * LANES)
        hg = 0.5 * g_ref[:, cols].astype(F32)
        silu = hg * jnp.tanh(hg) + hg
        o_ref[:, cols] = (b_s[sl, 0:seq, :] * silu).astype(BF16)


def _rg_lru(u3, conv_w, conv_b, w_a, b_a, w_x, b_x, lam, *, blocks_per_step=2):
    b, s, _ = u3.shape
    pitch = _scan_pitch(s)
    nblk = blocks_per_step
    width = nblk * LRU_BLOCK_W
    n_slab = width // LANES
    n_steps = LRU_BLOCKS // nblk
    body = functools.partial(_lru_body, seq=s, pitch=pitch, nblk=nblk)

    def vec_spec(rows):
        return pl.BlockSpec((rows, width), lambda bi, n: (0, n))

    def blk_spec(rows):
        return pl.BlockSpec((nblk, rows, LRU_BLOCK_W), lambda bi, n: (n, 0, 0))

    return pl.pallas_call(
        body,
        grid=(b, n_steps),
        in_specs=[
            pl.BlockSpec((None, s, width), lambda bi, n: (bi, 0, n)),
            pl.BlockSpec((None, s, width), lambda bi, n: (bi, 0, n_steps + n)),
            vec_spec(CONV_W), vec_spec(1),
            blk_spec(LRU_BLOCK_W), blk_spec(1), blk_spec(LRU_BLOCK_W), blk_spec(1),
            vec_spec(1),
        ],
        out_specs=pl.BlockSpec((None, s, width), lambda bi, n: (bi, 0, n)),
        out_shape=jax.ShapeDtypeStruct((b, s, LRU_WIDTH), BF16),
        scratch_shapes=[pltpu.VMEM((n_slab, s + SUBLANES, LANES), F32),
                        pltpu.VMEM((n_slab, SCAN_SEGS * pitch, LANES), F32),
                        pltpu.VMEM((n_slab, SCAN_SEGS * pitch, LANES), F32)],
        compiler_params=_cparams(("parallel", "arbitrary")),
        name="rg_lru",
    )(u3, u3, conv_w, conv_b, w_a, b_a, w_x, b_x, lam)


def _s5_body(u_ref, bm_ref, cm_ref, are_ref, aim_ref, dskip_ref, y_ref,
             su, ut, bre, bim, hre, him, *, tt, pitch):
    nb = SCAN_SEGS
    n_q = SSM_WIDTH // LANES
    n_state = SSM_GROUPS * SSM_STATE
    half_w = SSM_WIDTH // 2
    half_s = n_state // 2

    @pl.when(pl.program_id(1) == 0)
    def _():
        hre[...] = jnp.zeros_like(hre)
        him[...] = jnp.zeros_like(him)

    for b in range(nb):
        ub = u_ref[b].astype(F32)
        for q in range(n_q):
            su[q, b * pitch:b * pitch + tt, :] = ub[:, q * LANES:(q + 1) * LANES]

    def to_time_major(t, _):
        rows = pl.ds(pl.multiple_of(t * nb, nb), nb)
        for q in range(n_q):
            ut[rows, q * LANES:(q + 1) * LANES] = su[q, pl.ds(t, nb, stride=pitch), :]
        return 0

    lax.fori_loop(0, tt, to_time_major, 0)

    u_bf = ut[...].astype(BF16)
    for hf in range(2):
        uh = u_bf[:, hf * half_w:(hf + 1) * half_w]
        cols = slice(hf * half_s, (hf + 1) * half_s)
        bre[:, cols] = jnp.dot(uh, bm_ref[hf, :, 0:half_s], preferred_element_type=F32)
        bim[:, cols] = jnp.dot(uh, bm_ref[hf, :, half_s:], preferred_element_type=F32)

    group = 4
    for k0 in range(0, n_state // LANES, group):
        sl = [slice((k0 + k) * LANES, (k0 + k + 1) * LANES) for k in range(group)]
        a_r = [jnp.broadcast_to(are_ref[:, c], (nb, LANES)) for c in sl]
        a_i = [jnp.broadcast_to(aim_ref[:, c], (nb, LANES)) for c in sl]

        def scan(t, carry):
            rows = pl.ds(pl.multiple_of(t * nb, nb), nb)
            out = []
            for k in range(group):
                x_r, x_i = carry[2 * k], carry[2 * k + 1]
                n_r = a_r[k] * x_r - a_i[k] * x_i + bre[rows, sl[k]]
                n_i = a_r[k] * x_i + a_i[k] * x_r + bim[rows, sl[k]]
                bre[rows, sl[k]] = n_r
                bim[rows, sl[k]] = n_i
                out += [n_r, n_i]
            return tuple(out)

        init = []
        for k in range(group):
            init += [hre[:, sl[k]], him[:, sl[k]]]
        last = lax.fori_loop(0, tt, scan, tuple(init), unroll=2)
        for k in range(group):
            hre[:, sl[k]] = last[2 * k]
            him[:, sl[k]] = last[2 * k + 1]

    for hf in range(2):
        cols = slice(hf * half_s, (hf + 1) * half_s)
        y = jnp.dot(bre[:, cols].astype(BF16), cm_ref[hf, 0:half_s, :], preferred_element_type=F32)
        y = y + jnp.dot(bim[:, cols].astype(BF16), cm_ref[hf, half_s:, :], preferred_element_type=F32)
        oc = slice(hf * half_w, (hf + 1) * half_w)
        ut[:, oc] = y + dskip_ref[:, oc] * ut[:, oc]

    def to_batch_major(t, _):
        rows = pl.ds(pl.multiple_of(t * nb, nb), nb)
        for q in range(n_q):
            su[q, pl.ds(t, nb, stride=pitch), :] = ut[rows, q * LANES:(q + 1) * LANES]
        return 0

    lax.fori_loop(0, tt, to_batch_major, 0)
    for b in range(nb):
        for q in range(n_q):
            y_ref[b, :, q * LANES:(q + 1) * LANES] = su[q, b * pitch:b * pitch + tt, :]


def _s5_scan(u3, b_mat, c_mat, a_re, a_im, d_skip, *, tt=128):
    b, s, _ = u3.shape
    nb = SCAN_SEGS
    pitch = tt + 4
    n_state = SSM_GROUPS * SSM_STATE
    din_blk = 2 * LRU_WIDTH // SSM_WIDTH
    body = functools.partial(_s5_body, tt=tt, pitch=pitch)
    full = lambda shape: pl.BlockSpec(shape, lambda bi, ti: (0,) * len(shape))
    return pl.pallas_call(
        body,
        grid=(b // nb, s // tt),
        in_specs=[
            pl.BlockSpec((nb, tt, SSM_WIDTH), lambda bi, ti: (bi, ti, din_blk)),
            full(b_mat.shape), full(c_mat.shape), full((1, n_state)), full((1, n_state)),
            full((1, SSM_WIDTH)),
        ],
        out_specs=pl.BlockSpec((nb, tt, SSM_WIDTH), lambda bi, ti: (bi, ti, 0)),
        out_shape=jax.ShapeDtypeStruct((b, s, SSM_WIDTH), F32),
        scratch_shapes=[
            pltpu.VMEM((SSM_WIDTH // LANES, nb * pitch, LANES), F32),
            pltpu.VMEM((tt * nb, SSM_WIDTH), F32),
            pltpu.VMEM((tt * nb, n_state), F32),
            pltpu.VMEM((tt * nb, n_state), F32),
            pltpu.VMEM((nb, n_state), F32),
            pltpu.VMEM((nb, n_state), F32),
        ],
        compiler_params=_cparams(("parallel", "arbitrary")),
        name="s5_scan",
    )(u3, b_mat, c_mat, a_re, a_im, d_skip)


def _s5_operators(a_re, a_im, b_re, b_im, c_re, c_im, log_dt):
    a_re, a_im = a_re.astype(F32), a_im.astype(F32)
    dt = jnp.exp(log_dt.astype(F32))[:, None]
    mag = jnp.exp(a_re * dt)
    ab_re, ab_im = mag * jnp.cos(a_im * dt), mag * jnp.sin(a_im * dt)
    den = a_re * a_re + a_im * a_im
    f_re = ((ab_re - 1.0) * a_re + ab_im * a_im) / den
    f_im = (ab_im * a_re - (ab_re - 1.0) * a_im) / den
    b_re, b_im = b_re.astype(F32), b_im.astype(F32)
    bb_re = f_re[..., None] * b_re - f_im[..., None] * b_im
    bb_im = f_re[..., None] * b_im + f_im[..., None] * b_re

    gh = SSM_GROUPS // 2
    eye = jnp.eye(gh, dtype=F32)

    def in_map(bb):
        t = bb.reshape(2, gh, SSM_STATE, SSM_GROUP)
        t = t.transpose(0, 1, 3, 2)[:, :, :, None, :] * eye[None, :, None, :, None]
        return t.reshape(2, gh * SSM_GROUP, gh * SSM_STATE)

    def out_map(cc):
        t = cc.reshape(2, gh, SSM_GROUP, SSM_STATE)
        t = t.transpose(0, 1, 3, 2)[:, :, :, None, :] * eye[None, :, None, :, None]
        return t.reshape(2, gh * SSM_STATE, gh * SSM_GROUP)

    b_mat = jnp.concatenate([in_map(bb_re), in_map(bb_im)], axis=2).astype(BF16)
    c_mat = jnp.concatenate([out_map(c_re.astype(F32)), out_map(-c_im.astype(F32))],
                            axis=1).astype(BF16)
    n_state = SSM_GROUPS * SSM_STATE
    return b_mat, c_mat, ab_re.reshape(1, n_state), ab_im.reshape(1, n_state)


def _rec_out_body(yc_ref, y5_ref, dg_ref, wglu_ref, bglu_ref, w_ref, x_ref, o_ref, *, wc):
    yd = y5_ref[...]
    inner = math.sqrt(2.0 / math.pi) * (yd + 0.044715 * (yd * yd * yd))
    yd = 0.5 * yd * (1.0 + jnp.tanh(inner))
    glu = jnp.dot(yd.astype(BF16), wglu_ref[...], preferred_element_type=F32) + bglu_ref[...]
    yd = yd * _sigmoid(glu)
    g = dg_ref[...].astype(F32)
    yd = (yd * (g * _sigmoid(g))).astype(BF16)
    acc = jnp.dot(yc_ref[...], w_ref[0:wc, :], preferred_element_type=F32)
    acc = acc + jnp.dot(yd, w_ref[wc:, :], preferred_element_type=F32)
    o_ref[...] = x_ref[...] + acc


def _rec_out(yc, y5, u2, w_glu, b_glu, w_bf16, x2, *, tm=512):
    t, d = x2.shape
    wc = yc.shape[1]
    gate_blk = 2 * LRU_WIDTH // SSM_WIDTH + 1
    return pl.pallas_call(
        functools.partial(_rec_out_body, wc=wc),
        grid=(t // tm,),
        in_specs=[
            pl.BlockSpec((tm, wc), lambda i: (i, 0)),
            pl.BlockSpec((tm, SSM_WIDTH), lambda i: (i, 0)),
            pl.BlockSpec((tm, SSM_WIDTH), lambda i: (i, gate_blk)),
            pl.BlockSpec((SSM_WIDTH, SSM_WIDTH), lambda i: (0, 0)),
            pl.BlockSpec((1, SSM_WIDTH), lambda i: (0, 0)),
            pl.BlockSpec((wc + SSM_WIDTH, d), lambda i: (0, 0)),
            pl.BlockSpec((tm, d), lambda i: (i, 0)),
        ],
        out_specs=pl.BlockSpec((tm, d), lambda i: (i, 0)),
        out_shape=jax.ShapeDtypeStruct((t, d), F32),
        compiler_params=_cparams(("parallel",)),
        name="rec_out",
    )(yc, y5, u2, w_glu, b_glu, w_bf16, x2)


def _attention_layer(x2, b, s, norm_g, w_in, q_g_a, k_g_a, rel_bias, q_g_b, k_g_b,
                     lq1, lk1, lq2, lk2, subln_g, w_out, layer_idx):
    scale = HEAD_DIM ** -0.5 * LOG2E
    ones = jnp.ones((1024,), F32)
    col_gain = jnp.concatenate([
        jnp.tile(q_g_a.astype(F32) * scale, A_HEADS), jnp.tile(k_g_a.astype(F32), A_HEADS), ones, ones,
        jnp.tile(q_g_b.astype(F32).reshape(-1) * scale, B_HEADS),
        jnp.tile(k_g_b.astype(F32).reshape(-1), B_HEADS), ones, ones])[None, :]
    tn = 1024
    u = _norm_proj(x2, norm_g.astype(F32)[None, :], w_in.astype(BF16), col_gain,
                   norm_tiles=(0, 1, 4, 5), tn=tn)
    u3 = u.reshape(b, s, ATTN_IN)
    ya = _attn_a(u3, _attn_a_bias(rel_bias))
    lam_init = 0.8 - 0.6 * math.exp(-0.3 * layer_idx)
    row = lambda v: v.astype(F32)[None, :]
    yb = _attn_b(u3, row(lq1), row(lk1), row(lq2), row(lk2), row(subln_g), lam_init)
    return _out_proj(ya.reshape(b * s, A_WIDTH), yb.reshape(b * s, B_WIDTH), w_out.astype(BF16), x2)


def _recurrent_layer(x2, b, s, norm_g, w_in, conv_w, conv_b, w_a, b_a, w_x, b_x, lru_lam,
                     a_re, a_im, b_re, b_im, c_re, c_im, d_skip, log_dt, w_glu, b_glu, w_out):
    u = _norm_proj(x2, norm_g.astype(F32)[None, :], w_in.astype(BF16),
                   jnp.ones((1, REC_IN), F32), norm_tiles=(), tn=1024)
    u3 = u.reshape(b, s, REC_IN)
    yc = _rg_lru(u3, conv_w.astype(F32), conv_b.astype(F32)[None, :],
                 (0.5 * w_a).astype(BF16), 0.5 * b_a.astype(F32)[:, None, :],
                 (0.5 * w_x).astype(BF16), 0.5 * b_x.astype(F32)[:, None, :],
                 lru_lam.astype(F32)[None, :])

    b_mat, c_mat, ab_re, ab_im = _s5_operators(a_re, a_im, b_re, b_im, c_re, c_im, log_dt)
    y5 = _s5_scan(u3, b_mat, c_mat, ab_re, ab_im, d_skip.astype(F32)[None, :])
    return _rec_out(yc.reshape(b * s, LRU_WIDTH), y5.reshape(b * s, SSM_WIDTH), u,
                    w_glu.astype(BF16), b_glu.astype(F32)[None, :], w_out.astype(BF16), x2)


def kernel(x, attn_norm_g, attn_w_in, a_q_g, a_k_g, a_rel_bias, b_q_g, b_k_g, b_lam_q1, b_lam_k1,
           b_lam_q2, b_lam_k2, b_subln_g, attn_w_out, rec_norm_g, rec_w_in, lru_conv_w, lru_conv_b,
           lru_w_a, lru_b_a, lru_w_x, lru_b_x, lru_lambda, ssm_a_re, ssm_a_im, ssm_b_re, ssm_b_im,
           ssm_c_re, ssm_c_im, ssm_d, ssm_log_dt, ssm_w_glu, ssm_b_glu, rec_w_out):
    b, s, d = x.shape
    depth = attn_norm_g.shape[0] + rec_norm_g.shape[0]
    x2 = x.reshape(b * s, d)
    for layer in range(depth):
        j = layer // 2
        if layer % 2 == 0:
            x2 = _attention_layer(x2, b, s, attn_norm_g[j], attn_w_in[j], a_q_g[j], a_k_g[j],
                                  a_rel_bias[j], b_q_g[j], b_k_g[j], b_lam_q1[j], b_lam_k1[j],
                                  b_lam_q2[j], b_lam_k2[j], b_subln_g[j], attn_w_out[j], layer)
        else:
            x2 = _recurrent_layer(x2, b, s, rec_norm_g[j], rec_w_in[j], lru_conv_w[j], lru_conv_b[j],
                                  lru_w_a[j], lru_b_a[j], lru_w_x[j], lru_b_x[j], lru_lambda[j],
                                  ssm_a_re[j], ssm_a_im[j], ssm_b_re[j], ssm_b_im[j],
                                  ssm_c_re[j], ssm_c_im[j], ssm_d[j], ssm_log_dt[j],
                                  ssm_w_glu[j], ssm_b_glu[j], rec_w_out[j])
    return x2.reshape(b, s, d)
```

```python
import functools
import math

import jax
import jax.numpy as jnp
from jax import lax
from jax.experimental import pallas as pl
from jax.experimental.pallas import tpu as pltpu

F32 = jnp.float32
BF16 = jnp.bfloat16

D_MODEL = 2048
CHUNK = 64
NEG_INF = -1e30
NORM_EPS = 1e-6

HEAD_DIM = 128
A_HEADS = 8
A_WIDTH = A_HEADS * HEAD_DIM
A_LEFT_CHUNKS = 8
A_MAX_REL = 128
B_HEADS = 4
B_V_DIM = 2 * HEAD_DIM
B_WIDTH = B_HEADS * B_V_DIM
ATTN_IN = 8 * 1024

LRU_BLOCKS = 6
LRU_BLOCK_W = 256
LRU_WIDTH = LRU_BLOCKS * LRU_BLOCK_W
CONV_W = 4
LRU_C = 8.0
SSM_GROUP = 16
SSM_GROUPS = 32
SSM_WIDTH = SSM_GROUPS * SSM_GROUP
SSM_STATE = 64
REC_IN = 2 * LRU_WIDTH + 2 * SSM_WIDTH

LANES = 128
SUBLANES = 8
MXU_N = 256
VMEM_LIMIT = 56 * 1024 * 1024

Q_BLK = 256
K_TILE = 256
LOG2E = 1.0 / math.log(2.0)
A_WIN = A_LEFT_CHUNKS * CHUNK + Q_BLK
SCAN_SEGS = SUBLANES


def _sigmoid(x):
    return 0.5 * jnp.tanh(0.5 * x) + 0.5


def _cparams(sem):
    return pltpu.CompilerParams(dimension_semantics=sem, vmem_limit_bytes=VMEM_LIMIT)


def _norm_proj_body(x_ref, g_ref, w_ref, cg_ref, o_ref, h_ref, *, norm_tiles, tn):
    j = pl.program_id(1)

    @pl.when(j == 0)
    def _():
        x = x_ref[...]
        ms = jnp.mean(x * x, axis=-1, keepdims=True)
        h_ref[...] = (x * lax.rsqrt(ms + NORM_EPS) * g_ref[...]).astype(BF16)

    def step(head_norm):
        for n in range(tn // MXU_N):
            acc = jnp.dot(h_ref[...], w_ref[:, n * MXU_N:(n + 1) * MXU_N],
                          preferred_element_type=F32)
            if not head_norm:
                o_ref[:, n * MXU_N:(n + 1) * MXU_N] = acc.astype(BF16)
                continue
            for hh in range(MXU_N // HEAD_DIM):
                sl = slice(n * MXU_N + hh * HEAD_DIM, n * MXU_N + (hh + 1) * HEAD_DIM)
                blk = acc[:, hh * HEAD_DIM:(hh + 1) * HEAD_DIM]
                ms = jnp.mean(blk * blk, axis=-1, keepdims=True)
                o_ref[:, sl] = (blk * lax.rsqrt(ms + NORM_EPS) * cg_ref[:, sl]).astype(BF16)

    if norm_tiles:
        is_norm = functools.reduce(jnp.logical_or, [j == t for t in norm_tiles])
        pl.when(is_norm)(functools.partial(step, True))
        pl.when(jnp.logical_not(is_norm))(functools.partial(step, False))
    else:
        step(False)


def _norm_proj(x2, gain, w_bf16, col_gain, norm_tiles, *, tm=1024, tn=1024):
    t, d = x2.shape
    n = w_bf16.shape[1]
    body = functools.partial(_norm_proj_body, norm_tiles=tuple(norm_tiles), tn=tn)
    return pl.pallas_call(
        body,
        grid=(t // tm, n // tn),
        in_specs=[
            pl.BlockSpec((tm, d), lambda i, j: (i, 0)),
            pl.BlockSpec((1, d), lambda i, j: (0, 0)),
            pl.BlockSpec((d, tn), lambda i, j: (0, j)),
            pl.BlockSpec((1, tn), lambda i, j: (0, j)),
        ],
        out_specs=pl.BlockSpec((tm, tn), lambda i, j: (i, j)),
        out_shape=jax.ShapeDtypeStruct((t, n), BF16),
        scratch_shapes=[pltpu.VMEM((tm, d), BF16)],
        compiler_params=_cparams(("parallel", "arbitrary")),
        name="norm_proj",
    )(x2, gain, w_bf16, col_gain)


def _softmax_tiles(logits_fn, n_t, s_scr, p_scr, m_scr, l_scr):
    half = K_TILE // 2
    for t in range(n_t):
        s = logits_fn(t)
        s_scr[:, t * K_TILE:(t + 1) * K_TILE] = s
        hm = jnp.maximum(s[:, 0:half], s[:, half:])
        m_scr[...] = hm if t == 0 else jnp.maximum(m_scr[...], hm)
    m = jnp.max(m_scr[...], axis=-1, keepdims=True)
    m_scr[...] = jnp.broadcast_to(m, m_scr.shape)
    for t in range(n_t):
        lo = slice(t * K_TILE, t * K_TILE + half)
        hi = slice(t * K_TILE + half, (t + 1) * K_TILE)
        p_lo = jnp.exp2(s_scr[:, lo] - m_scr[...])
        p_hi = jnp.exp2(s_scr[:, hi] - m_scr[...])
        if l_scr is not None:
            l_scr[...] = (p_lo + p_hi) if t == 0 else l_scr[...] + (p_lo + p_hi)
        p_scr[:, lo] = p_lo.astype(BF16)
        p_scr[:, hi] = p_hi.astype(BF16)
    if l_scr is None:
        return None
    return jnp.sum(l_scr[...], axis=-1, keepdims=True)


def _attn_a_body(q_ref, k_ref, v_ref, g_ref, bias_ref, o_ref, s_scr, p_scr, m_scr, v1_scr, *,
                 heads, seq):
    for hh in range(heads):
        cols = slice(hh * HEAD_DIM, (hh + 1) * HEAD_DIM)
        v1_scr[hh, :, 0:HEAD_DIM] = v_ref[:, cols]
        v1_scr[hh, :, HEAD_DIM:] = jnp.ones((seq, HEAD_DIM), BF16)
        for i in range(seq // Q_BLK):
            rows = slice(i * Q_BLK, (i + 1) * Q_BLK)
            ks = max(0, i * Q_BLK - A_LEFT_CHUNKS * CHUNK)
            win = (i + 1) * Q_BLK - ks
            q = q_ref[rows, cols]

            def logits(t, q=q, ks=ks, win=win, hh=hh, cols=cols):
                kt = k_ref[ks + t * K_TILE:ks + (t + 1) * K_TILE, cols]
                b0 = A_WIN - win + t * K_TILE
                return (lax.dot_general(q, kt, (((1,), (1,)), ((), ())), preferred_element_type=F32)
                        + bias_ref[hh, :, b0:b0 + K_TILE])

            _softmax_tiles(logits, win // K_TILE, s_scr, p_scr, m_scr, None)
            ol = jnp.dot(p_scr[:, 0:win], v1_scr[hh, ks:ks + win, :], preferred_element_type=F32)
            o = ol[:, 0:HEAD_DIM] / ol[:, HEAD_DIM:HEAD_DIM + 1]
            g = g_ref[rows, cols].astype(F32)
            o_ref[rows, cols] = (o * (g * _sigmoid(g))).astype(BF16)


def _attn_a(u3, bias, *, heads_per_step=2):
    b, s, _ = u3.shape
    w = heads_per_step * HEAD_DIM
    per = A_WIDTH // w
    body = functools.partial(_attn_a_body, heads=heads_per_step, seq=s)

    def col_spec(section):
        return pl.BlockSpec((None, s, w), lambda bi, hg: (bi, 0, section * per + hg))

    return pl.pallas_call(
        body,
        grid=(b, per),
        in_specs=[
            col_spec(0), col_spec(1), col_spec(2), col_spec(3),
            pl.BlockSpec((heads_per_step, Q_BLK, A_WIN), lambda bi, hg: (hg, 0, 0)),
        ],
        out_specs=pl.BlockSpec((None, s, w), lambda bi, hg: (bi, 0, hg)),
        out_shape=jax.ShapeDtypeStruct((b, s, A_WIDTH), BF16),
        scratch_shapes=[pltpu.VMEM((Q_BLK, A_WIN), F32), pltpu.VMEM((Q_BLK, A_WIN), BF16),
                        pltpu.VMEM((Q_BLK, K_TILE // 2), F32),
                        pltpu.VMEM((heads_per_step, s, 2 * HEAD_DIM), BF16)],
        compiler_params=_cparams(("parallel", "arbitrary")),
        name="attn_a",
    )(u3, u3, u3, u3, bias)


def _attn_a_bias(rel_bias):
    h = rel_bias.shape[0]
    rb = rel_bias.astype(F32)
    n_v = Q_BLK + A_WIN - 1
    lo = A_LEFT_CHUNKS * CHUNK + Q_BLK - 1 - A_MAX_REL
    v = jnp.concatenate([jnp.broadcast_to(rb[:, :1], (h, lo + 1)), rb[:, 1:2 * A_MAX_REL],
                         jnp.broadcast_to(rb[:, -1:], (h, n_v - lo - 2 * A_MAX_REL))], axis=1)
    v = jnp.roll(v, -(Q_BLK - 1), axis=1)
    flat = jnp.tile(v, (1, Q_BLK))[:, :Q_BLK * (n_v - 1)]
    toep = flat.reshape(h, Q_BLK, n_v - 1)[:, :, :A_WIN]
    qc = jnp.arange(Q_BLK)[:, None] // CHUNK
    kc = jnp.floor_divide(jnp.arange(A_WIN)[None, :] - A_LEFT_CHUNKS * CHUNK, CHUNK)
    allowed = (kc <= qc) & (kc >= qc - A_LEFT_CHUNKS)
    return jnp.where(allowed[None], toep * LOG2E, NEG_INF)


def _attn_b_body(q_ref, k_ref, v_ref, g_ref, lq1_ref, lk1_ref, lq2_ref, lk2_ref, sub_ref,
                 o_ref, bias_ref, s_scr, p_scr, m_scr, l_scr, *, seq, lam_init):
    h = pl.program_id(1)
    n_blk = seq // Q_BLK
    diag0 = seq - Q_BLK

    slope = jnp.exp2((-8.0 / B_HEADS) * jnp.full((1, 1), h + 1, jnp.int32).astype(F32))
    r = lax.broadcasted_iota(jnp.int32, (Q_BLK, seq), 0)
    c = lax.broadcasted_iota(jnp.int32, (Q_BLK, seq), 1) - diag0
    dist = jnp.abs(r - c).astype(F32)
    chunk_bits = CHUNK.bit_length() - 1
    allowed = jnp.logical_or(c < 0, lax.shift_right_arithmetic(c, chunk_bits)
                             <= lax.shift_right_arithmetic(r, chunk_bits))
    bias_ref[...] = jnp.where(allowed, (-LOG2E) * slope * dist, NEG_INF)

    lam = (jnp.exp(jnp.sum(lq1_ref[...] * lk1_ref[...], axis=-1, keepdims=True))
           - jnp.exp(jnp.sum(lq2_ref[...] * lk2_ref[...], axis=-1, keepdims=True)) + lam_init)

    for i in range(n_blk):
        rows = slice(i * Q_BLK, (i + 1) * Q_BLK)
        n_t = (i + 1) * Q_BLK // K_TILE
        win = n_t * K_TILE
        outs = []
        for comp in range(2):
            cols = slice(comp * HEAD_DIM, (comp + 1) * HEAD_DIM)
            q = q_ref[rows, cols]

            def logits(t, q=q, win=win, cols=cols):
                b0 = seq - win + t * K_TILE
                return (lax.dot_general(q, k_ref[t * K_TILE:(t + 1) * K_TILE, cols],
                                        (((1,), (1,)), ((), ())), preferred_element_type=F32)
                        + bias_ref[:, b0:b0 + K_TILE])

            l = _softmax_tiles(logits, n_t, s_scr, p_scr, m_scr, l_scr)
            pv = jnp.dot(p_scr[:, 0:win], v_ref[0:win, :], preferred_element_type=F32)
            outs.append(pv * ((1.0 if comp == 0 else lam) / l))
        o = outs[0] - outs[1]
        ms = jnp.mean(o * o, axis=-1, keepdims=True)
        o = o * lax.rsqrt(ms + NORM_EPS) * sub_ref[...] * (1.0 - lam_init)
        g = g_ref[rows, :].astype(F32)
        o_ref[rows, :] = (o * (g * _sigmoid(g))).astype(BF16)


def _attn_b(u3, lq1, lk1, lq2, lk2, subln_g, lam_init):
    b, s, _ = u3.shape
    per = B_WIDTH // B_V_DIM
    body = functools.partial(_attn_b_body, seq=s, lam_init=lam_init)

    def col_spec(section):
        return pl.BlockSpec((None, s, B_V_DIM), lambda bi, h: (bi, 0, section * per + h))

    def vec_spec(n):
        return pl.BlockSpec((1, n), lambda bi, h: (0, 0))

    return pl.pallas_call(
        body,
        grid=(b, B_HEADS),
        in_specs=[col_spec(4), col_spec(5), col_spec(6), col_spec(7),
                  vec_spec(HEAD_DIM), vec_spec(HEAD_DIM), vec_spec(HEAD_DIM), vec_spec(HEAD_DIM),
                  vec_spec(B_V_DIM)],
        out_specs=pl.BlockSpec((None, s, B_V_DIM), lambda bi, h: (bi, 0, h)),
        out_shape=jax.ShapeDtypeStruct((b, s, B_WIDTH), BF16),
        scratch_shapes=[pltpu.VMEM((Q_BLK, s), F32), pltpu.VMEM((Q_BLK, s), F32),
                        pltpu.VMEM((Q_BLK, s), BF16), pltpu.VMEM((Q_BLK, K_TILE // 2), F32),
                        pltpu.VMEM((Q_BLK, K_TILE // 2), F32)],
        compiler_params=_cparams(("parallel", "arbitrary")),
        name="attn_b",
    )(u3, u3, u3, u3, lq1, lk1, lq2, lk2, subln_g)


def _out_proj_body(ya_ref, yb_ref, w_ref, x_ref, o_ref, *, wa):
    acc = jnp.dot(ya_ref[...], w_ref[0:wa, :], preferred_element_type=F32)
    acc = acc + jnp.dot(yb_ref[...], w_ref[wa:, :], preferred_element_type=F32)
    o_ref[...] = x_ref[...] + acc


def _out_proj(ya, yb, w_bf16, x2, *, tm=512):
    t, d = x2.shape
    wa, wb = ya.shape[1], yb.shape[1]
    return pl.pallas_call(
        functools.partial(_out_proj_body, wa=wa),
        grid=(t // tm,),
        in_specs=[
            pl.BlockSpec((tm, wa), lambda i: (i, 0)),
            pl.BlockSpec((tm, wb), lambda i: (i, 0)),
            pl.BlockSpec((wa + wb, d), lambda i: (0, 0)),
            pl.BlockSpec((tm, d), lambda i: (i, 0)),
        ],
        out_specs=pl.BlockSpec((tm, d), lambda i: (i, 0)),
        out_shape=jax.ShapeDtypeStruct((t, d), F32),
        compiler_params=_cparams(("parallel",)),
        name="out_proj",
    )(ya, yb, w_bf16, x2)


def _scan_pitch(seq):
    seg = -(-seq // SCAN_SEGS)
    return seg + (4 - seg) % SUBLANES


def _lru_body(x_ref, g_ref, cw_ref, cb_ref, wa_ref, ba_ref, wx_ref, bx_ref, lam_ref,
              o_ref, x_s, a_s, b_s, *, seq, pitch, nblk):
    n_slab = nblk * LRU_BLOCK_W // LANES
    slab_per_blk = LRU_BLOCK_W // LANES
    pad = SCAN_SEGS * pitch - seq

    for sl in range(n_slab):
        x_s[sl, 0:SUBLANES, :] = jnp.zeros((SUBLANES, LANES), F32)
        x_s[sl, SUBLANES:, :] = x_ref[:, sl * LANES:(sl + 1) * LANES].astype(F32)

    for blk in range(nblk):
        parts = []
        for half in range(slab_per_blk):
            sl = blk * slab_per_blk + half
            cols = slice(sl * LANES, (sl + 1) * LANES)
            xc = cb_ref[:, cols] + cw_ref[CONV_W - 1:CONV_W, cols] * x_s[sl, SUBLANES:, :]
            for d in range(1, CONV_W):
                xc = xc + (cw_ref[CONV_W - 1 - d:CONV_W - d, cols]
                           * x_s[sl, SUBLANES - d:SUBLANES - d + seq, :])
            parts.append(xc)
        xc = jnp.concatenate(parts, axis=1)
        bcols = slice(blk * LRU_BLOCK_W, (blk + 1) * LRU_BLOCK_W)

        xb = xc.astype(BF16)
        t_r = jnp.tanh(jnp.dot(xb, wa_ref[blk], preferred_element_type=F32) + ba_ref[blk])
        t_i = jnp.tanh(jnp.dot(xb, wx_ref[blk], preferred_element_type=F32) + bx_ref[blk])
        gi = 0.5 * t_i + 0.5
        lam = lam_ref[:, bcols]
        softplus_neg = jnp.maximum(-lam, 0.0) + jnp.log1p(jnp.exp(-jnp.abs(lam)))
        half_rate = (-0.5 * LRU_C * LOG2E) * softplus_neg
        a = jnp.exp2(t_r * half_rate + half_rate)
        y = 1.0 - a * a
        mult = jnp.where(y > 0.0, y * lax.rsqrt(y), 0.0)
        bb = mult * (gi * xc)
        for half in range(slab_per_blk):
            sl = blk * slab_per_blk + half
            cols = slice(half * LANES, (half + 1) * LANES)
            a_s[sl, 0:seq, :] = a[:, cols]
            b_s[sl, 0:seq, :] = bb[:, cols]
            a_s[sl, seq:, :] = jnp.zeros((pad, LANES), F32)
            b_s[sl, seq:, :] = jnp.zeros((pad, LANES), F32)

    def pass1(j, carry):
        new = []
        for sl in range(n_slab):
            hh, pp = carry[2 * sl], carry[2 * sl + 1]
            idx = pl.ds(j, SCAN_SEGS, stride=pitch)
            aj = a_s[sl, idx, :]
            bj = b_s[sl, idx, :]
            hh = aj * hh + bj
            pp = aj * pp
            b_s[sl, idx, :] = hh
            a_s[sl, idx, :] = pp
            new += [hh, pp]
        return tuple(new)

    init = []
    for sl in range(n_slab):
        init += [jnp.zeros((SCAN_SEGS, LANES), F32), jnp.ones((SCAN_SEGS, LANES), F32)]
    ends = lax.fori_loop(0, pitch, pass1, tuple(init), unroll=2)

    seg_idx = lax.broadcasted_iota(jnp.int32, (SCAN_SEGS, LANES), 0)
    carries = []
    for sl in range(n_slab):
        h_end, p_end = ends[2 * sl], ends[2 * sl + 1]
        c = jnp.zeros((SCAN_SEGS, LANES), F32)
        for sgm in range(1, SCAN_SEGS):
            c = jnp.where(seg_idx == sgm, pltpu.roll(h_end + p_end * c, 1, axis=0), c)
        carries.append(c)

    def pass2(j, _):
        for sl in range(n_slab):
            idx = pl.ds(j, SCAN_SEGS, stride=pitch)
            b_s[sl, idx, :] = b_s[sl, idx, :] + a_s[sl, idx, :] * carries[sl]
        return 0

    lax.fori_loop(0, pitch, pass2, 0, unroll=4)

    for sl in range(n_slab):
        cols = slice(sl * LANES, (sl + 1) * LANES)
        hg = 0.5 * g_ref[:, cols].astype(F32)
        silu = hg * jnp.tanh(hg) + hg
        o_ref[:, cols] = (b_s[sl, 0:seq, :] * silu).astype(BF16)


def _rg_lru(u3, conv_w, conv_b, w_a, b_a, w_x, b_x, lam, *, blocks_per_step=2):
    b, s, _ = u3.shape
    pitch = _scan_pitch(s)
    nblk = blocks_per_step
    width = nblk * LRU_BLOCK_W
    n_slab = width // LANES
    n_steps = LRU_BLOCKS // nblk
    body = functools.partial(_lru_body, seq=s, pitch=pitch, nblk=nblk)

    def vec_spec(rows):
        return pl.BlockSpec((rows, width), lambda bi, n: (0, n))

    def blk_spec(rows):
        return pl.BlockSpec((nblk, rows, LRU_BLOCK_W), lambda bi, n: (n, 0, 0))

    return pl.pallas_call(
        body,
        grid=(b, n_steps),
        in_specs=[
            pl.BlockSpec((None, s, width), lambda bi, n: (bi, 0, n)),
            pl.BlockSpec((None, s, width), lambda bi, n: (bi, 0, n_steps + n)),
            vec_spec(CONV_W), vec_spec(1),
            blk_spec(LRU_BLOCK_W), blk_spec(1), blk_spec(LRU_BLOCK_W), blk_spec(1),
            vec_spec(1),
        ],
        out_specs=pl.BlockSpec((None, s, width), lambda bi, n: (bi, 0, n)),
        out_shape=jax.ShapeDtypeStruct((b, s, LRU_WIDTH), BF16),
        scratch_shapes=[pltpu.VMEM((n_slab, s + SUBLANES, LANES), F32),
                        pltpu.VMEM((n_slab, SCAN_SEGS * pitch, LANES), F32),
                        pltpu.VMEM((n_slab, SCAN_SEGS * pitch, LANES), F32)],
        compiler_params=_cparams(("parallel", "arbitrary")),
        name="rg_lru",
    )(u3, u3, conv_w, conv_b, w_a, b_a, w_x, b_x, lam)


def _s5_body(u_ref, bm_ref, cm_ref, are_ref, aim_ref, dskip_ref, y_ref,
             su, ut, bre, bim, hre, him, *, tt, pitch, n_sub):
    nb = SCAN_SEGS
    n_q = SSM_WIDTH // LANES
    n_state = SSM_GROUPS * SSM_STATE
    half_w = SSM_WIDTH // 2
    half_s = n_state // 2

    @pl.when(pl.program_id(1) == 0)
    def _():
        hre[...] = jnp.zeros_like(hre)
        him[...] = jnp.zeros_like(him)

    for b in range(nb):
        ub = u_ref[b].astype(F32)
        for q in range(n_q):
            su[q, b * pitch:b * pitch + tt, :] = ub[:, q * LANES:(q + 1) * LANES]

    def to_time_major(t, _):
        rows = pl.ds(pl.multiple_of(t * nb, nb), nb)
        for q in range(n_q):
            ut[rows, q * LANES:(q + 1) * LANES] = su[q, pl.ds(t, nb, stride=pitch), :]
        return 0

    lax.fori_loop(0, tt, to_time_major, 0)

    sub_rows = (tt // n_sub) * nb
    n_slab = n_state // LANES
    group = 4

    def input_map(j):
        r = slice(j * sub_rows, (j + 1) * sub_rows)
        u_bf = ut[r, :].astype(BF16)
        for hf in range(2):
            uh = u_bf[:, hf * half_w:(hf + 1) * half_w]
            cols = slice(hf * half_s, (hf + 1) * half_s)
            bre[r, cols] = jnp.dot(uh, bm_ref[hf, :, 0:half_s], preferred_element_type=F32)
            bim[r, cols] = jnp.dot(uh, bm_ref[hf, :, half_s:], preferred_element_type=F32)

    def scan(j):
        for k0 in range(0, n_slab, group):
            sl = [slice((k0 + k) * LANES, (k0 + k + 1) * LANES) for k in range(group)]
            a_r = [jnp.broadcast_to(are_ref[:, c], (nb, LANES)) for c in sl]
            a_i = [jnp.broadcast_to(aim_ref[:, c], (nb, LANES)) for c in sl]
            if j == 0:
                x = [(hre[:, c], him[:, c]) for c in sl]
            else:
                prev = slice(j * sub_rows - nb, j * sub_rows)
                x = [(bre[prev, c], bim[prev, c]) for c in sl]
            for t in range(tt // n_sub):
                rows = slice(j * sub_rows + t * nb, j * sub_rows + (t + 1) * nb)
                for k in range(group):
                    x_r, x_i = x[k]
                    n_r = a_r[k] * x_r - a_i[k] * x_i + bre[rows, sl[k]]
                    n_i = a_r[k] * x_i + a_i[k] * x_r + bim[rows, sl[k]]
                    bre[rows, sl[k]] = n_r
                    bim[rows, sl[k]] = n_i
                    x[k] = (n_r, n_i)

    def output_map(j):
        r = slice(j * sub_rows, (j + 1) * sub_rows)
        for hf in range(2):
            cols = slice(hf * half_s, (hf + 1) * half_s)
            y = jnp.dot(bre[r, cols].astype(BF16), cm_ref[hf, 0:half_s, :], preferred_element_type=F32)
            y = y + jnp.dot(bim[r, cols].astype(BF16), cm_ref[hf, half_s:, :],
                            preferred_element_type=F32)
            oc = slice(hf * half_w, (hf + 1) * half_w)
            ut[r, oc] = y + dskip_ref[:, oc] * ut[r, oc]

    input_map(0)
    for j in range(n_sub):
        if j + 1 < n_sub:
            input_map(j + 1)
        scan(j)
        if j > 0:
            output_map(j - 1)
    last = slice(tt * nb - nb, tt * nb)
    hre[...] = bre[last, :]
    him[...] = bim[last, :]
    output_map(n_sub - 1)

    def to_batch_major(t, _):
        rows = pl.ds(pl.multiple_of(t * nb, nb), nb)
        for q in range(n_q):
            su[q, pl.ds(t, nb, stride=pitch), :] = ut[rows, q * LANES:(q + 1) * LANES]
        return 0

    lax.fori_loop(0, tt, to_batch_major, 0)
    for b in range(nb):
        for q in range(n_q):
            y_ref[b, :, q * LANES:(q + 1) * LANES] = su[q, b * pitch:b * pitch + tt, :]


def _s5_scan(u3, b_mat, c_mat, a_re, a_im, d_skip, *, tt=128, n_sub=2):
    b, s, _ = u3.shape
    nb = SCAN_SEGS
    pitch = tt + 4
    n_state = SSM_GROUPS * SSM_STATE
    din_blk = 2 * LRU_WIDTH // SSM_WIDTH
    body = functools.partial(_s5_body, tt=tt, pitch=pitch, n_sub=n_sub)
    full = lambda shape: pl.BlockSpec(shape, lambda bi, ti: (0,) * len(shape))
    return pl.pallas_call(
        body,
        grid=(b // nb, s // tt),
        in_specs=[
            pl.BlockSpec((nb, tt, SSM_WIDTH), lambda bi, ti: (bi, ti, din_blk)),
            full(b_mat.shape), full(c_mat.shape), full((1, n_state)), full((1, n_state)),
            full((1, SSM_WIDTH)),
        ],
        out_specs=pl.BlockSpec((nb, tt, SSM_WIDTH), lambda bi, ti: (bi, ti, 0)),
        out_shape=jax.ShapeDtypeStruct((b, s, SSM_WIDTH), F32),
        scratch_shapes=[
            pltpu.VMEM((SSM_WIDTH // LANES, nb * pitch, LANES), F32),
            pltpu.VMEM((tt * nb, SSM_WIDTH), F32),
            pltpu.VMEM((tt * nb, n_state), F32),
            pltpu.VMEM((tt * nb, n_state), F32),
            pltpu.VMEM((nb, n_state), F32),
            pltpu.VMEM((nb, n_state), F32),
        ],
        compiler_params=_cparams(("parallel", "arbitrary")),
        name="s5_scan",
    )(u3, b_mat, c_mat, a_re, a_im, d_skip)


def _s5_operators(a_re, a_im, b_re, b_im, c_re, c_im, log_dt):
    a_re, a_im = a_re.astype(F32), a_im.astype(F32)
    dt = jnp.exp(log_dt.astype(F32))[:, None]
    mag = jnp.exp(a_re * dt)
    ab_re, ab_im = mag * jnp.cos(a_im * dt), mag * jnp.sin(a_im * dt)
    den = a_re * a_re + a_im * a_im
    f_re = ((ab_re - 1.0) * a_re + ab_im * a_im) / den
    f_im = (ab_im * a_re - (ab_re - 1.0) * a_im) / den
    b_re, b_im = b_re.astype(F32), b_im.astype(F32)
    bb_re = f_re[..., None] * b_re - f_im[..., None] * b_im
    bb_im = f_re[..., None] * b_im + f_im[..., None] * b_re

    gh = SSM_GROUPS // 2
    eye = jnp.eye(gh, dtype=F32)

    def in_map(bb):
        t = bb.reshape(2, gh, SSM_STATE, SSM_GROUP)
        t = t.transpose(0, 1, 3, 2)[:, :, :, None, :] * eye[None, :, None, :, None]
        return t.reshape(2, gh * SSM_GROUP, gh * SSM_STATE)

    def out_map(cc):
        t = cc.reshape(2, gh, SSM_GROUP, SSM_STATE)
        t = t.transpose(0, 1, 3, 2)[:, :, :, None, :] * eye[None, :, None, :, None]
        return t.reshape(2, gh * SSM_STATE, gh * SSM_GROUP)

    b_mat = jnp.concatenate([in_map(bb_re), in_map(bb_im)], axis=2).astype(BF16)
    c_mat = jnp.concatenate([out_map(c_re.astype(F32)), out_map(-c_im.astype(F32))],
                            axis=1).astype(BF16)
    n_state = SSM_GROUPS * SSM_STATE
    return b_mat, c_mat, ab_re.reshape(1, n_state), ab_im.reshape(1, n_state)


def _rec_out_body(yc_ref, y5_ref, dg_ref, wglu_ref, bglu_ref, w_ref, x_ref, o_ref, *, wc):
    yd = y5_ref[...]
    inner = math.sqrt(2.0 / math.pi) * (yd + 0.044715 * (yd * yd * yd))
    yd = 0.5 * yd * (1.0 + jnp.tanh(inner))
    glu = jnp.dot(yd.astype(BF16), wglu_ref[...], preferred_element_type=F32) + bglu_ref[...]
    yd = yd * _sigmoid(glu)
    g = dg_ref[...].astype(F32)
    yd = (yd * (g * _sigmoid(g))).astype(BF16)
    acc = jnp.dot(yc_ref[...], w_ref[0:wc, :], preferred_element_type=F32)
    acc = acc + jnp.dot(yd, w_ref[wc:, :], preferred_element_type=F32)
    o_ref[...] = x_ref[...] + acc


def _rec_out(yc, y5, u2, w_glu, b_glu, w_bf16, x2, *, tm=512):
    t, d = x2.shape
    wc = yc.shape[1]
    gate_blk = 2 * LRU_WIDTH // SSM_WIDTH + 1
    return pl.pallas_call(
        functools.partial(_rec_out_body, wc=wc),
        grid=(t // tm,),
        in_specs=[
            pl.BlockSpec((tm, wc), lambda i: (i, 0)),
            pl.BlockSpec((tm, SSM_WIDTH), lambda i: (i, 0)),
            pl.BlockSpec((tm, SSM_WIDTH), lambda i: (i, gate_blk)),
            pl.BlockSpec((SSM_WIDTH, SSM_WIDTH), lambda i: (0, 0)),
            pl.BlockSpec((1, SSM_WIDTH), lambda i: (0, 0)),
            pl.BlockSpec((wc + SSM_WIDTH, d), lambda i: (0, 0)),
            pl.BlockSpec((tm, d), lambda i: (i, 0)),
        ],
        out_specs=pl.BlockSpec((tm, d), lambda i: (i, 0)),
        out_shape=jax.ShapeDtypeStruct((t, d), F32),
        compiler_params=_cparams(("parallel",)),
        name="rec_out",
    )(yc, y5, u2, w_glu, b_glu, w_bf16, x2)


def _attention_layer(x2, b, s, norm_g, w_in, q_g_a, k_g_a, rel_bias, q_g_b, k_g_b,
                     lq1, lk1, lq2, lk2, subln_g, w_out, layer_idx):
    scale = HEAD_DIM ** -0.5 * LOG2E
    ones = jnp.ones((1024,), F32)
    col_gain = jnp.concatenate([
        jnp.tile(q_g_a.astype(F32) * scale, A_HEADS), jnp.tile(k_g_a.astype(F32), A_HEADS), ones, ones,
        jnp.tile(q_g_b.astype(F32).reshape(-1) * scale, B_HEADS),
        jnp.tile(k_g_b.astype(F32).reshape(-1), B_HEADS), ones, ones])[None, :]
    tn = 2048
    u = _norm_proj(x2, norm_g.astype(F32)[None, :], w_in.astype(BF16), col_gain,
                   norm_tiles=(0, 2), tn=tn)
    u3 = u.reshape(b, s, ATTN_IN)
    ya = _attn_a(u3, _attn_a_bias(rel_bias))
    lam_init = 0.8 - 0.6 * math.exp(-0.3 * layer_idx)
    row = lambda v: v.astype(F32)[None, :]
    yb = _attn_b(u3, row(lq1), row(lk1), row(lq2), row(lk2), row(subln_g), lam_init)
    return _out_proj(ya.reshape(b * s, A_WIDTH), yb.reshape(b * s, B_WIDTH), w_out.astype(BF16), x2)


def _recurrent_layer(x2, b, s, norm_g, w_in, conv_w, conv_b, w_a, b_a, w_x, b_x, lru_lam,
                     a_re, a_im, b_re, b_im, c_re, c_im, d_skip, log_dt, w_glu, b_glu, w_out):
    u = _norm_proj(x2, norm_g.astype(F32)[None, :], w_in.astype(BF16),
                   jnp.ones((1, REC_IN), F32), norm_tiles=(), tn=2048)
    u3 = u.reshape(b, s, REC_IN)
    yc = _rg_lru(u3, conv_w.astype(F32), conv_b.astype(F32)[None, :],
                 (0.5 * w_a).astype(BF16), 0.5 * b_a.astype(F32)[:, None, :],
                 (0.5 * w_x).astype(BF16), 0.5 * b_x.astype(F32)[:, None, :],
                 lru_lam.astype(F32)[None, :])

    b_mat, c_mat, ab_re, ab_im = _s5_operators(a_re, a_im, b_re, b_im, c_re, c_im, log_dt)
    y5 = _s5_scan(u3, b_mat, c_mat, ab_re, ab_im, d_skip.astype(F32)[None, :])
    return _rec_out(yc.reshape(b * s, LRU_WIDTH), y5.reshape(b * s, SSM_WIDTH), u,
                    w_glu.astype(BF16), b_glu.astype(F32)[None, :], w_out.astype(BF16), x2)


def kernel(x, attn_norm_g, attn_w_in, a_q_g, a_k_g, a_rel_bias, b_q_g, b_k_g, b_lam_q1, b_lam_k1,
           b_lam_q2, b_lam_k2, b_subln_g, attn_w_out, rec_norm_g, rec_w_in, lru_conv_w, lru_conv_b,
           lru_w_a, lru_b_a, lru_w_x, lru_b_x, lru_lambda, ssm_a_re, ssm_a_im, ssm_b_re, ssm_b_im,
           ssm_c_re, ssm_c_im, ssm_d, ssm_log_dt, ssm_w_glu, ssm_b_glu, rec_w_out):
    b, s, d = x.shape
    depth = attn_norm_g.shape[0] + rec_norm_g.shape[0]
    x2 = x.reshape(b * s, d)
    for layer in range(depth):
        j = layer // 2
        if layer % 2 == 0:
            x2 = _attention_layer(x2, b, s, attn_norm_g[j], attn_w_in[j], a_q_g[j], a_k_g[j],
                                  a_rel_bias[j], b_q_g[j], b_k_g[j], b_lam_q1[j], b_lam_k1[j],
                                  b_lam_q2[j], b_lam_k2[j], b_subln_g[j], attn_w_out[j], layer)
        else:
            x2 = _recurrent_layer(x2, b, s, rec_norm_g[j], rec_w_in[j], lru_conv_w[j], lru_conv_b[j],
                                  lru_w_a[j], lru_b_a[j], lru_w_x[j], lru_b_x[j], lru_lambda[j],
                                  ssm_a_re[j], ssm_a_im[j], ssm_b_re[j], ssm_b_im[j],
                                  ssm_c_re[j], ssm_c_im[j], ssm_d[j], ssm_log_dt[j],
                                  ssm_w_glu[j], ssm_b_glu[j], rec_w_out[j])
    return x2.reshape(b, s, d)
```

```python
import functools
import math

import jax
import jax.numpy as jnp
from jax import lax
from jax.experimental import pallas as pl
from jax.experimental.pallas import tpu as pltpu

F32 = jnp.float32
BF16 = jnp.bfloat16

D_MODEL = 2048
CHUNK = 64
NEG_INF = -1e30
NORM_EPS = 1e-6

HEAD_DIM = 128
A_HEADS = 8
A_WIDTH = A_HEADS * HEAD_DIM
A_LEFT_CHUNKS = 8
A_MAX_REL = 128
B_HEADS = 4
B_V_DIM = 2 * HEAD_DIM
B_WIDTH = B_HEADS * B_V_DIM
ATTN_IN = 8 * 1024

LRU_BLOCKS = 6
LRU_BLOCK_W = 256
LRU_WIDTH = LRU_BLOCKS * LRU_BLOCK_W
CONV_W = 4
LRU_C = 8.0
SSM_GROUP = 16
SSM_GROUPS = 32
SSM_WIDTH = SSM_GROUPS * SSM_GROUP
SSM_STATE = 64
REC_IN = 2 * LRU_WIDTH + 2 * SSM_WIDTH

LANES = 128
SUBLANES = 8
MXU_N = 256
VMEM_LIMIT = 56 * 1024 * 1024

Q_BLK = 256
K_TILE = 256
LOG2E = 1.0 / math.log(2.0)
BOUND_SLACK = 1.0 + 2.0 ** -6
ONE_PASS_RANGE = 100.0
A_WIN = A_LEFT_CHUNKS * CHUNK + Q_BLK
SCAN_SEGS = SUBLANES


def _sigmoid(x):
    return 0.5 * jnp.tanh(0.5 * x) + 0.5


def _cparams(sem):
    return pltpu.CompilerParams(dimension_semantics=sem, vmem_limit_bytes=VMEM_LIMIT)


def _norm_proj_body(x_ref, g_ref, w_ref, cg_ref, o_ref, h_ref, *, norm_tiles, tn):
    j = pl.program_id(1)

    @pl.when(j == 0)
    def _():
        x = x_ref[...]
        ms = jnp.mean(x * x, axis=-1, keepdims=True)
        h_ref[...] = (x * lax.rsqrt(ms + NORM_EPS) * g_ref[...]).astype(BF16)

    def step(head_norm):
        for n in range(tn // MXU_N):
            acc = jnp.dot(h_ref[...], w_ref[:, n * MXU_N:(n + 1) * MXU_N],
                          preferred_element_type=F32)
            if not head_norm:
                o_ref[:, n * MXU_N:(n + 1) * MXU_N] = acc.astype(BF16)
                continue
            for hh in range(MXU_N // HEAD_DIM):
                sl = slice(n * MXU_N + hh * HEAD_DIM, n * MXU_N + (hh + 1) * HEAD_DIM)
                blk = acc[:, hh * HEAD_DIM:(hh + 1) * HEAD_DIM]
                ms = jnp.mean(blk * blk, axis=-1, keepdims=True)
                o_ref[:, sl] = (blk * lax.rsqrt(ms + NORM_EPS) * cg_ref[:, sl]).astype(BF16)

    if norm_tiles:
        is_norm = functools.reduce(jnp.logical_or, [j == t for t in norm_tiles])
        pl.when(is_norm)(functools.partial(step, True))
        pl.when(jnp.logical_not(is_norm))(functools.partial(step, False))
    else:
        step(False)


def _norm_proj(x2, gain, w_bf16, col_gain, norm_tiles, *, tm=1024, tn=1024):
    t, d = x2.shape
    n = w_bf16.shape[1]
    body = functools.partial(_norm_proj_body, norm_tiles=tuple(norm_tiles), tn=tn)
    return pl.pallas_call(
        body,
        grid=(t // tm, n // tn),
        in_specs=[
            pl.BlockSpec((tm, d), lambda i, j: (i, 0)),
            pl.BlockSpec((1, d), lambda i, j: (0, 0)),
            pl.BlockSpec((d, tn), lambda i, j: (0, j)),
            pl.BlockSpec((1, tn), lambda i, j: (0, j)),
        ],
        out_specs=pl.BlockSpec((tm, tn), lambda i, j: (i, j)),
        out_shape=jax.ShapeDtypeStruct((t, n), BF16),
        scratch_shapes=[pltpu.VMEM((tm, d), BF16)],
        compiler_params=_cparams(("parallel", "arbitrary")),
        name="norm_proj",
    )(x2, gain, w_bf16, col_gain)


def _softmax_tiles(logits_fn, n_t, s_scr, p_scr, m_scr, l_scr):
    half = K_TILE // 2
    for t in range(n_t):
        s = logits_fn(t)
        s_scr[:, t * K_TILE:(t + 1) * K_TILE] = s
        hm = jnp.maximum(s[:, 0:half], s[:, half:])
        m_scr[...] = hm if t == 0 else jnp.maximum(m_scr[...], hm)
    m = jnp.max(m_scr[...], axis=-1, keepdims=True)
    m_scr[...] = jnp.broadcast_to(m, m_scr.shape)
    for t in range(n_t):
        lo = slice(t * K_TILE, t * K_TILE + half)
        hi = slice(t * K_TILE + half, (t + 1) * K_TILE)
        p_lo = jnp.exp2(s_scr[:, lo] - m_scr[...])
        p_hi = jnp.exp2(s_scr[:, hi] - m_scr[...])
        if l_scr is not None:
            l_scr[...] = (p_lo + p_hi) if t == 0 else l_scr[...] + (p_lo + p_hi)
        p_scr[:, lo] = p_lo.astype(BF16)
        p_scr[:, hi] = p_hi.astype(BF16)
    if l_scr is None:
        return None
    return jnp.sum(l_scr[...], axis=-1, keepdims=True)


def _attn_a_body(q_ref, k_ref, v_ref, g_ref, bias_ref, o_ref, s_scr, p_scr, m_scr, v1_scr, *,
                 heads, seq):
    for hh in range(heads):
        cols = slice(hh * HEAD_DIM, (hh + 1) * HEAD_DIM)
        v1_scr[hh, :, 0:HEAD_DIM] = v_ref[:, cols]
        v1_scr[hh, :, HEAD_DIM:] = jnp.ones((seq, HEAD_DIM), BF16)
        for i in range(seq // Q_BLK):
            rows = slice(i * Q_BLK, (i + 1) * Q_BLK)
            ks = max(0, i * Q_BLK - A_LEFT_CHUNKS * CHUNK)
            win = (i + 1) * Q_BLK - ks
            q = q_ref[rows, cols]

            def logits(t, q=q, ks=ks, win=win, hh=hh, cols=cols):
                kt = k_ref[ks + t * K_TILE:ks + (t + 1) * K_TILE, cols]
                b0 = A_WIN - win + t * K_TILE
                return (lax.dot_general(q, kt, (((1,), (1,)), ((), ())), preferred_element_type=F32)
                        + bias_ref[hh, :, b0:b0 + K_TILE])

            _softmax_tiles(logits, win // K_TILE, s_scr, p_scr, m_scr, None)
            ol = jnp.dot(p_scr[:, 0:win], v1_scr[hh, ks:ks + win, :], preferred_element_type=F32)
            o = ol[:, 0:HEAD_DIM] / ol[:, HEAD_DIM:HEAD_DIM + 1]
            g = g_ref[rows, cols].astype(F32)
            o_ref[rows, cols] = (o * (g * _sigmoid(g))).astype(BF16)


def _attn_a(u3, bias, *, heads_per_step=2):
    b, s, _ = u3.shape
    w = heads_per_step * HEAD_DIM
    per = A_WIDTH // w
    body = functools.partial(_attn_a_body, heads=heads_per_step, seq=s)

    def col_spec(section):
        return pl.BlockSpec((None, s, w), lambda bi, hg: (bi, 0, section * per + hg))

    return pl.pallas_call(
        body,
        grid=(b, per),
        in_specs=[
            col_spec(0), col_spec(1), col_spec(2), col_spec(3),
            pl.BlockSpec((heads_per_step, Q_BLK, A_WIN), lambda bi, hg: (hg, 0, 0)),
        ],
        out_specs=pl.BlockSpec((None, s, w), lambda bi, hg: (bi, 0, hg)),
        out_shape=jax.ShapeDtypeStruct((b, s, A_WIDTH), BF16),
        scratch_shapes=[pltpu.VMEM((Q_BLK, A_WIN), F32), pltpu.VMEM((Q_BLK, A_WIN), BF16),
                        pltpu.VMEM((Q_BLK, K_TILE // 2), F32),
                        pltpu.VMEM((heads_per_step, s, 2 * HEAD_DIM), BF16)],
        compiler_params=_cparams(("parallel", "arbitrary")),
        name="attn_a",
    )(u3, u3, u3, u3, bias)


def _attn_a_bias(rel_bias):
    h = rel_bias.shape[0]
    rb = rel_bias.astype(F32)
    n_v = Q_BLK + A_WIN - 1
    lo = A_LEFT_CHUNKS * CHUNK + Q_BLK - 1 - A_MAX_REL
    v = jnp.concatenate([jnp.broadcast_to(rb[:, :1], (h, lo + 1)), rb[:, 1:2 * A_MAX_REL],
                         jnp.broadcast_to(rb[:, -1:], (h, n_v - lo - 2 * A_MAX_REL))], axis=1)
    v = jnp.roll(v, -(Q_BLK - 1), axis=1)
    flat = jnp.tile(v, (1, Q_BLK))[:, :Q_BLK * (n_v - 1)]
    toep = flat.reshape(h, Q_BLK, n_v - 1)[:, :, :A_WIN]
    qc = jnp.arange(Q_BLK)[:, None] // CHUNK
    kc = jnp.floor_divide(jnp.arange(A_WIN)[None, :] - A_LEFT_CHUNKS * CHUNK, CHUNK)
    allowed = (kc <= qc) & (kc >= qc - A_LEFT_CHUNKS)
    return jnp.where(allowed[None], toep * LOG2E, NEG_INF)


def _attn_b_body(q_ref, k_ref, v_ref, g_ref, lq1_ref, lk1_ref, lq2_ref, lk2_ref, sub_ref, mb_ref,
                 o_ref, bias_ref, s_scr, p_scr, m_scr, l_scr, qa_scr, ka_scr, *, seq, lam_init):
    h = pl.program_id(0)
    n_blk = seq // Q_BLK
    half = K_TILE // 2
    chunk_bits = CHUNK.bit_length() - 1
    nt_dims = (((1,), (1,)), ((), ()))
    slope2 = LOG2E * jnp.exp2((-8.0 / B_HEADS) * jnp.full((1, 1), h + 1, jnp.int32).astype(F32))

    lam = (jnp.exp(jnp.sum(lq1_ref[...] * lk1_ref[...], axis=-1, keepdims=True))
           - jnp.exp(jnp.sum(lq2_ref[...] * lk2_ref[...], axis=-1, keepdims=True)) + lam_init)

    m_bound = mb_ref[...]

    @pl.when(pl.program_id(1) == 0)
    def _():
        pos = slope2 * lax.broadcasted_iota(jnp.int32, (seq, 1), 0).astype(F32)
        p_hi = pos.astype(BF16).astype(F32)
        p_mid = (pos - p_hi).astype(BF16).astype(F32)
        p_lo = pos - p_hi - p_mid
        lane = lax.broadcasted_iota(jnp.int32, (seq, HEAD_DIM), 1)
        aug_q = jnp.where(lane == 0, -m_bound, jnp.where(lane == 1, -p_hi, jnp.where(
            lane == 2, -p_mid, jnp.where(lane == 3, -p_lo, jnp.where(lane < 7, 1.0, 0.0)))))
        aug_k = jnp.where(lane < 4, 1.0, jnp.where(lane == 4, p_hi, jnp.where(
            lane == 5, p_mid, jnp.where(lane == 6, p_lo, 0.0))))
        for comp in range(2):
            qa_scr[comp, :, HEAD_DIM:] = aug_q.astype(BF16)
            ka_scr[comp, :, HEAD_DIM:] = aug_k.astype(BF16)

    for comp in range(2):
        cols = slice(comp * HEAD_DIM, (comp + 1) * HEAD_DIM)
        qa_scr[comp, :, 0:HEAD_DIM] = q_ref[:, cols]
        ka_scr[comp, :, 0:HEAD_DIM] = k_ref[:, cols]
    one_pass_ok = 2.0 * m_bound[0, 0] < ONE_PASS_RANGE

    def finish_block(i, outs):
        rows = slice(i * Q_BLK, (i + 1) * Q_BLK)
        o = outs[0] - outs[1]
        ms = jnp.mean(o * o, axis=-1, keepdims=True)
        o = o * lax.rsqrt(ms + NORM_EPS) * sub_ref[...] * (1.0 - lam_init)
        g = g_ref[rows, :].astype(F32)
        o_ref[rows, :] = (o * (g * _sigmoid(g))).astype(BF16)

    def weighted_values(comp, win, l):
        pv = jnp.dot(p_scr[:, 0:win], v_ref[0:win, :], preferred_element_type=F32)
        return pv * ((1.0 if comp == 0 else lam) / l)

    def one_pass():
        r = lax.broadcasted_iota(jnp.int32, (Q_BLK, K_TILE), 0)
        c = lax.broadcasted_iota(jnp.int32, (Q_BLK, K_TILE), 1)
        ahead = jnp.maximum(c - r, 0).astype(F32)
        allowed = lax.shift_right_arithmetic(c, chunk_bits) <= lax.shift_right_arithmetic(r, chunk_bits)
        s_scr[:, 0:K_TILE] = jnp.where(allowed, -2.0 * slope2 * ahead, NEG_INF)
        for i in range(n_blk):
            rows = slice(i * Q_BLK, (i + 1) * Q_BLK)
            n_t = (i + 1) * Q_BLK // K_TILE
            outs = []
            for comp in range(2):
                qa = qa_scr[comp, rows, :]
                for t in range(n_t):
                    z = lax.dot_general(qa, ka_scr[comp, t * K_TILE:(t + 1) * K_TILE, :], nt_dims,
                                        preferred_element_type=F32)
                    if t == n_t - 1:
                        z = z + s_scr[:, 0:K_TILE]
                    e_lo = jnp.exp2(z[:, 0:half])
                    e_hi = jnp.exp2(z[:, half:])
                    l_scr[...] = (e_lo + e_hi) if t == 0 else l_scr[...] + (e_lo + e_hi)
                    p_scr[:, t * K_TILE:t * K_TILE + half] = e_lo.astype(BF16)
                    p_scr[:, t * K_TILE + half:(t + 1) * K_TILE] = e_hi.astype(BF16)
                l = jnp.sum(l_scr[...], axis=-1, keepdims=True)
                outs.append(weighted_values(comp, n_t * K_TILE, l))
            finish_block(i, outs)

    def two_pass():
        diag0 = seq - Q_BLK
        r = lax.broadcasted_iota(jnp.int32, (Q_BLK, seq), 0)
        c = lax.broadcasted_iota(jnp.int32, (Q_BLK, seq), 1) - diag0
        dist = jnp.abs(r - c).astype(F32)
        allowed = jnp.logical_or(c < 0, lax.shift_right_arithmetic(c, chunk_bits)
                                 <= lax.shift_right_arithmetic(r, chunk_bits))
        bias_ref[...] = jnp.where(allowed, -slope2 * dist, NEG_INF)
        for i in range(n_blk):
            rows = slice(i * Q_BLK, (i + 1) * Q_BLK)
            n_t = (i + 1) * Q_BLK // K_TILE
            win = n_t * K_TILE
            outs = []
            for comp in range(2):
                cols = slice(comp * HEAD_DIM, (comp + 1) * HEAD_DIM)
                q = q_ref[rows, cols]

                def logits(t, q=q, win=win, cols=cols):
                    b0 = seq - win + t * K_TILE
                    return (lax.dot_general(q, k_ref[t * K_TILE:(t + 1) * K_TILE, cols], nt_dims,
                                            preferred_element_type=F32)
                            + bias_ref[:, b0:b0 + K_TILE])

                l = _softmax_tiles(logits, n_t, s_scr, p_scr, m_scr, l_scr)
                outs.append(weighted_values(comp, win, l))
            finish_block(i, outs)

    pl.when(one_pass_ok)(one_pass)
    pl.when(jnp.logical_not(one_pass_ok))(two_pass)


def _attn_b(u3, lq1, lk1, lq2, lk2, subln_g, qk_bound, lam_init):
    b, s, _ = u3.shape
    per = B_WIDTH // B_V_DIM
    body = functools.partial(_attn_b_body, seq=s, lam_init=lam_init)

    def col_spec(section):
        return pl.BlockSpec((None, s, B_V_DIM), lambda h, bi: (bi, 0, section * per + h))

    def vec_spec(n):
        return pl.BlockSpec((1, n), lambda h, bi: (0, 0))

    return pl.pallas_call(
        body,
        grid=(B_HEADS, b),
        in_specs=[col_spec(4), col_spec(5), col_spec(6), col_spec(7),
                  vec_spec(HEAD_DIM), vec_spec(HEAD_DIM), vec_spec(HEAD_DIM), vec_spec(HEAD_DIM),
                  vec_spec(B_V_DIM), vec_spec(1)],
        out_specs=pl.BlockSpec((None, s, B_V_DIM), lambda h, bi: (bi, 0, h)),
        out_shape=jax.ShapeDtypeStruct((b, s, B_WIDTH), BF16),
        scratch_shapes=[pltpu.VMEM((Q_BLK, s), F32), pltpu.VMEM((Q_BLK, s), F32),
                        pltpu.VMEM((Q_BLK, s), BF16), pltpu.VMEM((Q_BLK, K_TILE // 2), F32),
                        pltpu.VMEM((Q_BLK, K_TILE // 2), F32),
                        pltpu.VMEM((2, s, 2 * HEAD_DIM), BF16), pltpu.VMEM((2, s, 2 * HEAD_DIM), BF16)],
        compiler_params=_cparams(("parallel", "arbitrary")),
        name="attn_b",
    )(u3, u3, u3, u3, lq1, lk1, lq2, lk2, subln_g, qk_bound)


def _out_proj_body(ya_ref, yb_ref, w_ref, x_ref, o_ref, *, wa):
    acc = jnp.dot(ya_ref[...], w_ref[0:wa, :], preferred_element_type=F32)
    acc = acc + jnp.dot(yb_ref[...], w_ref[wa:, :], preferred_element_type=F32)
    o_ref[...] = x_ref[...] + acc


def _out_proj(ya, yb, w_bf16, x2, *, tm=512):
    t, d = x2.shape
    wa, wb = ya.shape[1], yb.shape[1]
    return pl.pallas_call(
        functools.partial(_out_proj_body, wa=wa),
        grid=(t // tm,),
        in_specs=[
            pl.BlockSpec((tm, wa), lambda i: (i, 0)),
            pl.BlockSpec((tm, wb), lambda i: (i, 0)),
            pl.BlockSpec((wa + wb, d), lambda i: (0, 0)),
            pl.BlockSpec((tm, d), lambda i: (i, 0)),
        ],
        out_specs=pl.BlockSpec((tm, d), lambda i: (i, 0)),
        out_shape=jax.ShapeDtypeStruct((t, d), F32),
        compiler_params=_cparams(("parallel",)),
        name="out_proj",
    )(ya, yb, w_bf16, x2)


def _scan_pitch(seq):
    seg = -(-seq // SCAN_SEGS)
    return seg + (4 - seg) % SUBLANES


def _lru_body(x_ref, g_ref, cw_ref, cb_ref, wa_ref, ba_ref, wx_ref, bx_ref, lam_ref,
              o_ref, x_s, a_s, b_s, *, seq, pitch, nblk):
    n_slab = nblk * LRU_BLOCK_W // LANES
    slab_per_blk = LRU_BLOCK_W // LANES
    pad = SCAN_SEGS * pitch - seq

    for sl in range(n_slab):
        x_s[sl, 0:SUBLANES, :] = jnp.zeros((SUBLANES, LANES), F32)
        x_s[sl, SUBLANES:, :] = x_ref[:, sl * LANES:(sl + 1) * LANES].astype(F32)

    for blk in range(nblk):
        parts = []
        for half in range(slab_per_blk):
            sl = blk * slab_per_blk + half
            cols = slice(sl * LANES, (sl + 1) * LANES)
            xc = cb_ref[:, cols] + cw_ref[CONV_W - 1:CONV_W, cols] * x_s[sl, SUBLANES:, :]
            for d in range(1, CONV_W):
                xc = xc + (cw_ref[CONV_W - 1 - d:CONV_W - d, cols]
                           * x_s[sl, SUBLANES - d:SUBLANES - d + seq, :])
            parts.append(xc)
        xc = jnp.concatenate(parts, axis=1)
        bcols = slice(blk * LRU_BLOCK_W, (blk + 1) * LRU_BLOCK_W)

        xb = xc.astype(BF16)
        t_r = jnp.tanh(jnp.dot(xb, wa_ref[blk], preferred_element_type=F32) + ba_ref[blk])
        t_i = jnp.tanh(jnp.dot(xb, wx_ref[blk], preferred_element_type=F32) + bx_ref[blk])
        gi = 0.5 * t_i + 0.5
        lam = lam_ref[:, bcols]
        softplus_neg = jnp.maximum(-lam, 0.0) + jnp.log1p(jnp.exp(-jnp.abs(lam)))
        half_rate = (-0.5 * LRU_C * LOG2E) * softplus_neg
        a = jnp.exp2(t_r * half_rate + half_rate)
        y = 1.0 - a * a
        mult = jnp.where(y > 0.0, y * lax.rsqrt(y), 0.0)
        bb = mult * (gi * xc)
        for half in range(slab_per_blk):
            sl = blk * slab_per_blk + half
            cols = slice(half * LANES, (half + 1) * LANES)
            a_s[sl, 0:seq, :] = a[:, cols]
            b_s[sl, 0:seq, :] = bb[:, cols]
            a_s[sl, seq:, :] = jnp.zeros((pad, LANES), F32)
            b_s[sl, seq:, :] = jnp.zeros((pad, LANES), F32)

    def pass1(j, carry):
        new = []
        for sl in range(n_slab):
            hh, pp = carry[2 * sl], carry[2 * sl + 1]
            idx = pl.ds(j, SCAN_SEGS, stride=pitch)
            aj = a_s[sl, idx, :]
            bj = b_s[sl, idx, :]
            hh = aj * hh + bj
            pp = aj * pp
            b_s[sl, idx, :] = hh
            a_s[sl, idx, :] = pp
            new += [hh, pp]
        return tuple(new)

    init = []
    for sl in range(n_slab):
        init += [jnp.zeros((SCAN_SEGS, LANES), F32), jnp.ones((SCAN_SEGS, LANES), F32)]
    ends = lax.fori_loop(0, pitch, pass1, tuple(init), unroll=2)

    seg_idx = lax.broadcasted_iota(jnp.int32, (SCAN_SEGS, LANES), 0)
    carries = []
    for sl in range(n_slab):
        h_end, p_end = ends[2 * sl], ends[2 * sl + 1]
        c = jnp.zeros((SCAN_SEGS, LANES), F32)
        for sgm in range(1, SCAN_SEGS):
            c = jnp.where(seg_idx == sgm, pltpu.roll(h_end + p_end * c, 1, axis=0), c)
        carries.append(c)

    def pass2(j, _):
        for sl in range(n_slab):
            idx = pl.ds(j, SCAN_SEGS, stride=pitch)
            b_s[sl, idx, :] = b_s[sl, idx, :] + a_s[sl, idx, :] * carries[sl]
        return 0

    lax.fori_loop(0, pitch, pass2, 0, unroll=4)

    for sl in range(n_slab):
        cols = slice(sl * LANES, (sl + 1) * LANES)
        hg = 0.5 * g_ref[:, cols].astype(F32)
        silu = hg * jnp.tanh(hg) + hg
        o_ref[:, cols] = (b_s[sl, 0:seq, :] * silu).astype(BF16)


def _rg_lru(u3, conv_w, conv_b, w_a, b_a, w_x, b_x, lam, *, blocks_per_step=2):
    b, s, _ = u3.shape
    pitch = _scan_pitch(s)
    nblk = blocks_per_step
    width = nblk * LRU_BLOCK_W
    n_slab = width // LANES
    n_steps = LRU_BLOCKS // nblk
    body = functools.partial(_lru_body, seq=s, pitch=pitch, nblk=nblk)

    def vec_spec(rows):
        return pl.BlockSpec((rows, width), lambda bi, n: (0, n))

    def blk_spec(rows):
        return pl.BlockSpec((nblk, rows, LRU_BLOCK_W), lambda bi, n: (n, 0, 0))

    return pl.pallas_call(
        body,
        grid=(b, n_steps),
        in_specs=[
            pl.BlockSpec((None, s, width), lambda bi, n: (bi, 0, n)),
            pl.BlockSpec((None, s, width), lambda bi, n: (bi, 0, n_steps + n)),
            vec_spec(CONV_W), vec_spec(1),
            blk_spec(LRU_BLOCK_W), blk_spec(1), blk_spec(LRU_BLOCK_W), blk_spec(1),
            vec_spec(1),
        ],
        out_specs=pl.BlockSpec((None, s, width), lambda bi, n: (bi, 0, n)),
        out_shape=jax.ShapeDtypeStruct((b, s, LRU_WIDTH), BF16),
        scratch_shapes=[pltpu.VMEM((n_slab, s + SUBLANES, LANES), F32),
                        pltpu.VMEM((n_slab, SCAN_SEGS * pitch, LANES), F32),
                        pltpu.VMEM((n_slab, SCAN_SEGS * pitch, LANES), F32)],
        compiler_params=_cparams(("parallel", "arbitrary")),
        name="rg_lru",
    )(u3, u3, conv_w, conv_b, w_a, b_a, w_x, b_x, lam)


def _s5_body(u_ref, bm_ref, cm_ref, are_ref, aim_ref, dskip_ref, y_ref,
             su, ut, bre, bim, hre, him, *, tt, pitch, n_sub):
    nb = SCAN_SEGS
    n_q = SSM_WIDTH // LANES
    n_state = SSM_GROUPS * SSM_STATE
    half_w = SSM_WIDTH // 2
    half_s = n_state // 2

    @pl.when(pl.program_id(1) == 0)
    def _():
        hre[...] = jnp.zeros_like(hre)
        him[...] = jnp.zeros_like(him)

    for b in range(nb):
        ub = u_ref[b].astype(F32)
        for q in range(n_q):
            su[q, b * pitch:b * pitch + tt, :] = ub[:, q * LANES:(q + 1) * LANES]

    def to_time_major(t, _):
        rows = pl.ds(pl.multiple_of(t * nb, nb), nb)
        for q in range(n_q):
            ut[rows, q * LANES:(q + 1) * LANES] = su[q, pl.ds(t, nb, stride=pitch), :]
        return 0

    lax.fori_loop(0, tt, to_time_major, 0)

    sub_rows = (tt // n_sub) * nb
    n_slab = n_state // LANES
    group = 4

    def input_map(j):
        r = slice(j * sub_rows, (j + 1) * sub_rows)
        u_bf = ut[r, :].astype(BF16)
        for hf in range(2):
            uh = u_bf[:, hf * half_w:(hf + 1) * half_w]
            cols = slice(hf * half_s, (hf + 1) * half_s)
            bre[r, cols] = jnp.dot(uh, bm_ref[hf, :, 0:half_s], preferred_element_type=F32)
            bim[r, cols] = jnp.dot(uh, bm_ref[hf, :, half_s:], preferred_element_type=F32)

    def scan(j):
        for k0 in range(0, n_slab, group):
            sl = [slice((k0 + k) * LANES, (k0 + k + 1) * LANES) for k in range(group)]
            a_r = [jnp.broadcast_to(are_ref[:, c], (nb, LANES)) for c in sl]
            a_i = [jnp.broadcast_to(aim_ref[:, c], (nb, LANES)) for c in sl]
            if j == 0:
                x = [(hre[:, c], him[:, c]) for c in sl]
            else:
                prev = slice(j * sub_rows - nb, j * sub_rows)
                x = [(bre[prev, c], bim[prev, c]) for c in sl]
            for t in range(tt // n_sub):
                rows = slice(j * sub_rows + t * nb, j * sub_rows + (t + 1) * nb)
                for k in range(group):
                    x_r, x_i = x[k]
                    n_r = a_r[k] * x_r - a_i[k] * x_i + bre[rows, sl[k]]
                    n_i = a_r[k] * x_i + a_i[k] * x_r + bim[rows, sl[k]]
                    bre[rows, sl[k]] = n_r
                    bim[rows, sl[k]] = n_i
                    x[k] = (n_r, n_i)

    def output_map(j):
        r = slice(j * sub_rows, (j + 1) * sub_rows)
        for hf in range(2):
            cols = slice(hf * half_s, (hf + 1) * half_s)
            y = jnp.dot(bre[r, cols].astype(BF16), cm_ref[hf, 0:half_s, :], preferred_element_type=F32)
            y = y + jnp.dot(bim[r, cols].astype(BF16), cm_ref[hf, half_s:, :],
                            preferred_element_type=F32)
            oc = slice(hf * half_w, (hf + 1) * half_w)
            ut[r, oc] = y + dskip_ref[:, oc] * ut[r, oc]

    input_map(0)
    for j in range(n_sub):
        if j + 1 < n_sub:
            input_map(j + 1)
        scan(j)
        if j > 0:
            output_map(j - 1)
    last = slice(tt * nb - nb, tt * nb)
    hre[...] = bre[last, :]
    him[...] = bim[last, :]
    output_map(n_sub - 1)

    def to_batch_major(t, _):
        rows = pl.ds(pl.multiple_of(t * nb, nb), nb)
        for q in range(n_q):
            su[q, pl.ds(t, nb, stride=pitch), :] = ut[rows, q * LANES:(q + 1) * LANES]
        return 0

    lax.fori_loop(0, tt, to_batch_major, 0)
    for b in range(nb):
        for q in range(n_q):
            y_ref[b, :, q * LANES:(q + 1) * LANES] = su[q, b * pitch:b * pitch + tt, :]


def _s5_scan(u3, b_mat, c_mat, a_re, a_im, d_skip, *, tt=128, n_sub=2):
    b, s, _ = u3.shape
    nb = SCAN_SEGS
    pitch = tt + 4
    n_state = SSM_GROUPS * SSM_STATE
    din_blk = 2 * LRU_WIDTH // SSM_WIDTH
    body = functools.partial(_s5_body, tt=tt, pitch=pitch, n_sub=n_sub)
    full = lambda shape: pl.BlockSpec(shape, lambda bi, ti: (0,) * len(shape))
    return pl.pallas_call(
        body,
        grid=(b // nb, s // tt),
        in_specs=[
            pl.BlockSpec((nb, tt, SSM_WIDTH), lambda bi, ti: (bi, ti, din_blk)),
            full(b_mat.shape), full(c_mat.shape), full((1, n_state)), full((1, n_state)),
            full((1, SSM_WIDTH)),
        ],
        out_specs=pl.BlockSpec((nb, tt, SSM_WIDTH), lambda bi, ti: (bi, ti, 0)),
        out_shape=jax.ShapeDtypeStruct((b, s, SSM_WIDTH), F32),
        scratch_shapes=[
            pltpu.VMEM((SSM_WIDTH // LANES, nb * pitch, LANES), F32),
            pltpu.VMEM((tt * nb, SSM_WIDTH), F32),
            pltpu.VMEM((tt * nb, n_state), F32),
            pltpu.VMEM((tt * nb, n_state), F32),
            pltpu.VMEM((nb, n_state), F32),
            pltpu.VMEM((nb, n_state), F32),
        ],
        compiler_params=_cparams(("parallel", "arbitrary")),
        name="s5_scan",
    )(u3, b_mat, c_mat, a_re, a_im, d_skip)


def _s5_operators(a_re, a_im, b_re, b_im, c_re, c_im, log_dt):
    a_re, a_im = a_re.astype(F32), a_im.astype(F32)
    dt = jnp.exp(log_dt.astype(F32))[:, None]
    mag = jnp.exp(a_re * dt)
    ab_re, ab_im = mag * jnp.cos(a_im * dt), mag * jnp.sin(a_im * dt)
    den = a_re * a_re + a_im * a_im
    f_re = ((ab_re - 1.0) * a_re + ab_im * a_im) / den
    f_im = (ab_im * a_re - (ab_re - 1.0) * a_im) / den
    b_re, b_im = b_re.astype(F32), b_im.astype(F32)
    bb_re = f_re[..., None] * b_re - f_im[..., None] * b_im
    bb_im = f_re[..., None] * b_im + f_im[..., None] * b_re

    gh = SSM_GROUPS // 2
    eye = jnp.eye(gh, dtype=F32)

    def in_map(bb):
        t = bb.reshape(2, gh, SSM_STATE, SSM_GROUP)
        t = t.transpose(0, 1, 3, 2)[:, :, :, None, :] * eye[None, :, None, :, None]
        return t.reshape(2, gh * SSM_GROUP, gh * SSM_STATE)

    def out_map(cc):
        t = cc.reshape(2, gh, SSM_GROUP, SSM_STATE)
        t = t.transpose(0, 1, 3, 2)[:, :, :, None, :] * eye[None, :, None, :, None]
        return t.reshape(2, gh * SSM_STATE, gh * SSM_GROUP)

    b_mat = jnp.concatenate([in_map(bb_re), in_map(bb_im)], axis=2).astype(BF16)
    c_mat = jnp.concatenate([out_map(c_re.astype(F32)), out_map(-c_im.astype(F32))],
                            axis=1).astype(BF16)
    n_state = SSM_GROUPS * SSM_STATE
    return b_mat, c_mat, ab_re.reshape(1, n_state), ab_im.reshape(1, n_state)


def _rec_out_body(yc_ref, y5_ref, dg_ref, wglu_ref, bglu_ref, w_ref, x_ref, o_ref, *, wc):
    yd = y5_ref[...]
    inner = math.sqrt(2.0 / math.pi) * (yd + 0.044715 * (yd * yd * yd))
    yd = 0.5 * yd * (1.0 + jnp.tanh(inner))
    glu = jnp.dot(yd.astype(BF16), wglu_ref[...], preferred_element_type=F32) + bglu_ref[...]
    yd = yd * _sigmoid(glu)
    g = dg_ref[...].astype(F32)
    yd = (yd * (g * _sigmoid(g))).astype(BF16)
    acc = jnp.dot(yc_ref[...], w_ref[0:wc, :], preferred_element_type=F32)
    acc = acc + jnp.dot(yd, w_ref[wc:, :], preferred_element_type=F32)
    o_ref[...] = x_ref[...] + acc


def _rec_out(yc, y5, u2, w_glu, b_glu, w_bf16, x2, *, tm=512):
    t, d = x2.shape
    wc = yc.shape[1]
    gate_blk = 2 * LRU_WIDTH // SSM_WIDTH + 1
    return pl.pallas_call(
        functools.partial(_rec_out_body, wc=wc),
        grid=(t // tm,),
        in_specs=[
            pl.BlockSpec((tm, wc), lambda i: (i, 0)),
            pl.BlockSpec((tm, SSM_WIDTH), lambda i: (i, 0)),
            pl.BlockSpec((tm, SSM_WIDTH), lambda i: (i, gate_blk)),
            pl.BlockSpec((SSM_WIDTH, SSM_WIDTH), lambda i: (0, 0)),
            pl.BlockSpec((1, SSM_WIDTH), lambda i: (0, 0)),
            pl.BlockSpec((wc + SSM_WIDTH, d), lambda i: (0, 0)),
            pl.BlockSpec((tm, d), lambda i: (i, 0)),
        ],
        out_specs=pl.BlockSpec((tm, d), lambda i: (i, 0)),
        out_shape=jax.ShapeDtypeStruct((t, d), F32),
        compiler_params=_cparams(("parallel",)),
        name="rec_out",
    )(yc, y5, u2, w_glu, b_glu, w_bf16, x2)


def _attention_layer(x2, b, s, norm_g, w_in, q_g_a, k_g_a, rel_bias, q_g_b, k_g_b,
                     lq1, lk1, lq2, lk2, subln_g, w_out, layer_idx):
    scale = HEAD_DIM ** -0.5 * LOG2E
    ones = jnp.ones((1024,), F32)
    col_gain = jnp.concatenate([
        jnp.tile(q_g_a.astype(F32) * scale, A_HEADS), jnp.tile(k_g_a.astype(F32), A_HEADS), ones, ones,
        jnp.tile(q_g_b.astype(F32).reshape(-1) * scale, B_HEADS),
        jnp.tile(k_g_b.astype(F32).reshape(-1), B_HEADS), ones, ones])[None, :]
    tn = 2048
    u = _norm_proj(x2, norm_g.astype(F32)[None, :], w_in.astype(BF16), col_gain,
                   norm_tiles=(0, 2), tn=tn)
    u3 = u.reshape(b, s, ATTN_IN)
    ya = _attn_a(u3, _attn_a_bias(rel_bias))
    lam_init = 0.8 - 0.6 * math.exp(-0.3 * layer_idx)
    row = lambda v: v.astype(F32)[None, :]
    qk_bound = (BOUND_SLACK * HEAD_DIM * scale * jnp.max(jnp.abs(q_g_b.astype(F32)))
                * jnp.max(jnp.abs(k_g_b.astype(F32)))).reshape(1, 1)
    yb = _attn_b(u3, row(lq1), row(lk1), row(lq2), row(lk2), row(subln_g), qk_bound, lam_init)
    return _out_proj(ya.reshape(b * s, A_WIDTH), yb.reshape(b * s, B_WIDTH), w_out.astype(BF16), x2)


def _recurrent_layer(x2, b, s, norm_g, w_in, conv_w, conv_b, w_a, b_a, w_x, b_x, lru_lam,
                     a_re, a_im, b_re, b_im, c_re, c_im, d_skip, log_dt, w_glu, b_glu, w_out):
    u = _norm_proj(x2, norm_g.astype(F32)[None, :], w_in.astype(BF16),
                   jnp.ones((1, REC_IN), F32), norm_tiles=(), tn=2048)
    u3 = u.reshape(b, s, REC_IN)
    yc = _rg_lru(u3, conv_w.astype(F32), conv_b.astype(F32)[None, :],
                 (0.5 * w_a).astype(BF16), 0.5 * b_a.astype(F32)[:, None, :],
                 (0.5 * w_x).astype(BF16), 0.5 * b_x.astype(F32)[:, None, :],
                 lru_lam.astype(F32)[None, :])

    b_mat, c_mat, ab_re, ab_im = _s5_operators(a_re, a_im, b_re, b_im, c_re, c_im, log_dt)
    y5 = _s5_scan(u3, b_mat, c_mat, ab_re, ab_im, d_skip.astype(F32)[None, :])
    return _rec_out(yc.reshape(b * s, LRU_WIDTH), y5.reshape(b * s, SSM_WIDTH), u,
                    w_glu.astype(BF16), b_glu.astype(F32)[None, :], w_out.astype(BF16), x2)


def kernel(x, attn_norm_g, attn_w_in, a_q_g, a_k_g, a_rel_bias, b_q_g, b_k_g, b_lam_q1, b_lam_k1,
           b_lam_q2, b_lam_k2, b_subln_g, attn_w_out, rec_norm_g, rec_w_in, lru_conv_w, lru_conv_b,
           lru_w_a, lru_b_a, lru_w_x, lru_b_x, lru_lambda, ssm_a_re, ssm_a_im, ssm_b_re, ssm_b_im,
           ssm_c_re, ssm_c_im, ssm_d, ssm_log_dt, ssm_w_glu, ssm_b_glu, rec_w_out):
    b, s, d = x.shape
    depth = attn_norm_g.shape[0] + rec_norm_g.shape[0]
    x2 = x.reshape(b * s, d)
    for layer in range(depth):
        j = layer // 2
        if layer % 2 == 0:
            x2 = _attention_layer(x2, b, s, attn_norm_g[j], attn_w_in[j], a_q_g[j], a_k_g[j],
                                  a_rel_bias[j], b_q_g[j], b_k_g[j], b_lam_q1[j], b_lam_k1[j],
                                  b_lam_q2[j], b_lam_k2[j], b_subln_g[j], attn_w_out[j], layer)
        else:
            x2 = _recurrent_layer(x2, b, s, rec_norm_g[j], rec_w_in[j], lru_conv_w[j], lru_conv_b[j],
                                  lru_w_a[j], lru_b_a[j], lru_w_x[j], lru_b_x[j], lru_lambda[j],
                                  ssm_a_re[j], ssm_a_im[j], ssm_b_re[j], ssm_b_im[j],
                                  ssm_c_re[j], ssm_c_im[j], ssm_d[j], ssm_log_dt[j],
                                  ssm_w_glu[j], ssm_b_glu[j], rec_w_out[j])
    return x2.reshape(b, s, d)
```

```python
import functools
import math

import jax
import jax.numpy as jnp
from jax import lax
from jax.experimental import pallas as pl
from jax.experimental.pallas import tpu as pltpu

F32 = jnp.float32
BF16 = jnp.bfloat16

D_MODEL = 2048
CHUNK = 64
NEG_INF = -1e30
NORM_EPS = 1e-6

HEAD_DIM = 128
A_HEADS = 8
A_WIDTH = A_HEADS * HEAD_DIM
A_LEFT_CHUNKS = 8
A_MAX_REL = 128
B_HEADS = 4
B_V_DIM = 2 * HEAD_DIM
B_WIDTH = B_HEADS * B_V_DIM
ATTN_IN = 8 * 1024

LRU_BLOCKS = 6
LRU_BLOCK_W = 256
LRU_WIDTH = LRU_BLOCKS * LRU_BLOCK_W
CONV_W = 4
LRU_C = 8.0
SSM_GROUP = 16
SSM_GROUPS = 32
SSM_WIDTH = SSM_GROUPS * SSM_GROUP
SSM_STATE = 64
REC_IN = 2 * LRU_WIDTH + 2 * SSM_WIDTH

LANES = 128
SUBLANES = 8
MXU_N = 256
VMEM_LIMIT = 56 * 1024 * 1024

Q_BLK = 256
K_TILE = 256
LOG2E = 1.0 / math.log(2.0)
BOUND_SLACK = 1.0 + 2.0 ** -6
ONE_PASS_RANGE = 100.0
A_WIN = A_LEFT_CHUNKS * CHUNK + Q_BLK
SCAN_SEGS = SUBLANES


def _sigmoid(x):
    return 0.5 * jnp.tanh(0.5 * x) + 0.5


def _cparams(sem):
    return pltpu.CompilerParams(dimension_semantics=sem, vmem_limit_bytes=VMEM_LIMIT)


def _norm_proj_body(x_ref, g_ref, w_ref, cg_ref, o_ref, h_ref, *, norm_tiles, tn):
    j = pl.program_id(1)

    @pl.when(j == 0)
    def _():
        x = x_ref[...]
        ms = jnp.mean(x * x, axis=-1, keepdims=True)
        h_ref[...] = (x * lax.rsqrt(ms + NORM_EPS) * g_ref[...]).astype(BF16)

    def step(head_norm):
        for n in range(tn // MXU_N):
            acc = jnp.dot(h_ref[...], w_ref[:, n * MXU_N:(n + 1) * MXU_N],
                          preferred_element_type=F32)
            if not head_norm:
                o_ref[:, n * MXU_N:(n + 1) * MXU_N] = acc.astype(BF16)
                continue
            for hh in range(MXU_N // HEAD_DIM):
                sl = slice(n * MXU_N + hh * HEAD_DIM, n * MXU_N + (hh + 1) * HEAD_DIM)
                blk = acc[:, hh * HEAD_DIM:(hh + 1) * HEAD_DIM]
                ms = jnp.mean(blk * blk, axis=-1, keepdims=True)
                o_ref[:, sl] = (blk * lax.rsqrt(ms + NORM_EPS) * cg_ref[:, sl]).astype(BF16)

    if norm_tiles:
        is_norm = functools.reduce(jnp.logical_or, [j == t for t in norm_tiles])
        pl.when(is_norm)(functools.partial(step, True))
        pl.when(jnp.logical_not(is_norm))(functools.partial(step, False))
    else:
        step(False)


def _norm_proj(x2, gain, w_bf16, col_gain, norm_tiles, *, tm=1024, tn=1024):
    t, d = x2.shape
    n = w_bf16.shape[1]
    body = functools.partial(_norm_proj_body, norm_tiles=tuple(norm_tiles), tn=tn)
    return pl.pallas_call(
        body,
        grid=(t // tm, n // tn),
        in_specs=[
            pl.BlockSpec((tm, d), lambda i, j: (i, 0)),
            pl.BlockSpec((1, d), lambda i, j: (0, 0)),
            pl.BlockSpec((d, tn), lambda i, j: (0, j)),
            pl.BlockSpec((1, tn), lambda i, j: (0, j)),
        ],
        out_specs=pl.BlockSpec((tm, tn), lambda i, j: (i, j)),
        out_shape=jax.ShapeDtypeStruct((t, n), BF16),
        scratch_shapes=[pltpu.VMEM((tm, d), BF16)],
        compiler_params=_cparams(("parallel", "arbitrary")),
        name="norm_proj",
    )(x2, gain, w_bf16, col_gain)


def _softmax_tiles(logits_fn, n_t, s_scr, p_scr, m_scr, l_scr):
    half = K_TILE // 2
    for t in range(n_t):
        s = logits_fn(t)
        s_scr[:, t * K_TILE:(t + 1) * K_TILE] = s
        hm = jnp.maximum(s[:, 0:half], s[:, half:])
        m_scr[...] = hm if t == 0 else jnp.maximum(m_scr[...], hm)
    m = jnp.max(m_scr[...], axis=-1, keepdims=True)
    m_scr[...] = jnp.broadcast_to(m, m_scr.shape)
    for t in range(n_t):
        lo = slice(t * K_TILE, t * K_TILE + half)
        hi = slice(t * K_TILE + half, (t + 1) * K_TILE)
        p_lo = jnp.exp2(s_scr[:, lo] - m_scr[...])
        p_hi = jnp.exp2(s_scr[:, hi] - m_scr[...])
        if l_scr is not None:
            l_scr[...] = (p_lo + p_hi) if t == 0 else l_scr[...] + (p_lo + p_hi)
        p_scr[:, lo] = p_lo.astype(BF16)
        p_scr[:, hi] = p_hi.astype(BF16)
    if l_scr is None:
        return None
    return jnp.sum(l_scr[...], axis=-1, keepdims=True)


def _attn_a_body(q_ref, k_ref, v_ref, g_ref, bias_ref, span_ref, o_ref, s_scr, p_scr, m_scr, v1_scr, *,
                 heads, seq):
    one_pass_ok = span_ref[0, 0] < ONE_PASS_RANGE
    nt_dims = (((1,), (1,)), ((), ()))

    def run(one_pass):
        for hh in range(heads):
            cols = slice(hh * HEAD_DIM, (hh + 1) * HEAD_DIM)
            v1_scr[hh, :, 0:HEAD_DIM] = v_ref[:, cols]
            v1_scr[hh, :, HEAD_DIM:] = jnp.ones((seq, HEAD_DIM), BF16)
            for i in range(seq // Q_BLK):
                rows = slice(i * Q_BLK, (i + 1) * Q_BLK)
                ks = max(0, i * Q_BLK - A_LEFT_CHUNKS * CHUNK)
                win = (i + 1) * Q_BLK - ks
                q = q_ref[rows, cols]

                def logits(t, q=q, ks=ks, win=win, hh=hh, cols=cols):
                    kt = k_ref[ks + t * K_TILE:ks + (t + 1) * K_TILE, cols]
                    b0 = A_WIN - win + t * K_TILE
                    return (lax.dot_general(q, kt, nt_dims, preferred_element_type=F32)
                            + bias_ref[hh, :, b0:b0 + K_TILE])

                if one_pass:
                    for t in range(win // K_TILE):
                        p_scr[:, t * K_TILE:(t + 1) * K_TILE] = jnp.exp2(logits(t)).astype(BF16)
                else:
                    _softmax_tiles(logits, win // K_TILE, s_scr, p_scr, m_scr, None)
                ol = jnp.dot(p_scr[:, 0:win], v1_scr[hh, ks:ks + win, :], preferred_element_type=F32)
                o = ol[:, 0:HEAD_DIM] / ol[:, HEAD_DIM:HEAD_DIM + 1]
                g = g_ref[rows, cols].astype(F32)
                o_ref[rows, cols] = (o * (g * _sigmoid(g))).astype(BF16)

    pl.when(one_pass_ok)(functools.partial(run, True))
    pl.when(jnp.logical_not(one_pass_ok))(functools.partial(run, False))


def _attn_a(u3, bias, span, *, heads_per_step=2):
    b, s, _ = u3.shape
    w = heads_per_step * HEAD_DIM
    per = A_WIDTH // w
    body = functools.partial(_attn_a_body, heads=heads_per_step, seq=s)

    def col_spec(section):
        return pl.BlockSpec((None, s, w), lambda bi, hg: (bi, 0, section * per + hg))

    return pl.pallas_call(
        body,
        grid=(b, per),
        in_specs=[
            col_spec(0), col_spec(1), col_spec(2), col_spec(3),
            pl.BlockSpec((heads_per_step, Q_BLK, A_WIN), lambda bi, hg: (hg, 0, 0)),
            pl.BlockSpec((1, 1), lambda bi, hg: (0, 0)),
        ],
        out_specs=pl.BlockSpec((None, s, w), lambda bi, hg: (bi, 0, hg)),
        out_shape=jax.ShapeDtypeStruct((b, s, A_WIDTH), BF16),
        scratch_shapes=[pltpu.VMEM((Q_BLK, A_WIN), F32), pltpu.VMEM((Q_BLK, A_WIN), BF16),
                        pltpu.VMEM((Q_BLK, K_TILE // 2), F32),
                        pltpu.VMEM((heads_per_step, s, 2 * HEAD_DIM), BF16)],
        compiler_params=_cparams(("parallel", "arbitrary")),
        name="attn_a",
    )(u3, u3, u3, u3, bias, span)


def _attn_a_bias(rel_bias, qk_bound):
    h = rel_bias.shape[0]
    rb = rel_bias.astype(F32)
    n_v = Q_BLK + A_WIN - 1
    lo = A_LEFT_CHUNKS * CHUNK + Q_BLK - 1 - A_MAX_REL
    v = jnp.concatenate([jnp.broadcast_to(rb[:, :1], (h, lo + 1)), rb[:, 1:2 * A_MAX_REL],
                         jnp.broadcast_to(rb[:, -1:], (h, n_v - lo - 2 * A_MAX_REL))], axis=1)
    v = jnp.roll(v, -(Q_BLK - 1), axis=1)
    flat = jnp.tile(v, (1, Q_BLK))[:, :Q_BLK * (n_v - 1)]
    toep = flat.reshape(h, Q_BLK, n_v - 1)[:, :, :A_WIN]
    qc = jnp.arange(Q_BLK)[:, None] // CHUNK
    kc = jnp.floor_divide(jnp.arange(A_WIN)[None, :] - A_LEFT_CHUNKS * CHUNK, CHUNK)
    allowed = (kc <= qc) & (kc >= qc - A_LEFT_CHUNKS)
    rb_max = jnp.max(rb, axis=1)
    shift = (qk_bound + LOG2E * rb_max)[:, None, None]
    span = jnp.max(2.0 * qk_bound + LOG2E * (rb_max - rb[:, A_MAX_REL])).reshape(1, 1)
    return jnp.where(allowed[None], toep * LOG2E - shift, NEG_INF), span


def _attn_b_body(q_ref, k_ref, v_ref, g_ref, lq1_ref, lk1_ref, lq2_ref, lk2_ref, sub_ref, mb_ref,
                 o_ref, bias_ref, s_scr, p_scr, m_scr, l_scr, qa_scr, ka_scr, *, seq, lam_init):
    h = pl.program_id(0)
    n_blk = seq // Q_BLK
    half = K_TILE // 2
    chunk_bits = CHUNK.bit_length() - 1
    nt_dims = (((1,), (1,)), ((), ()))
    slope2 = LOG2E * jnp.exp2((-8.0 / B_HEADS) * jnp.full((1, 1), h + 1, jnp.int32).astype(F32))

    lam = (jnp.exp(jnp.sum(lq1_ref[...] * lk1_ref[...], axis=-1, keepdims=True))
           - jnp.exp(jnp.sum(lq2_ref[...] * lk2_ref[...], axis=-1, keepdims=True)) + lam_init)

    m_bound = mb_ref[...]

    @pl.when(pl.program_id(1) == 0)
    def _():
        pos = slope2 * lax.broadcasted_iota(jnp.int32, (seq, 1), 0).astype(F32)
        p_hi = pos.astype(BF16).astype(F32)
        p_mid = (pos - p_hi).astype(BF16).astype(F32)
        p_lo = pos - p_hi - p_mid
        lane = lax.broadcasted_iota(jnp.int32, (seq, HEAD_DIM), 1)
        aug_q = jnp.where(lane == 0, -m_bound, jnp.where(lane == 1, -p_hi, jnp.where(
            lane == 2, -p_mid, jnp.where(lane == 3, -p_lo, jnp.where(lane < 7, 1.0, 0.0)))))
        aug_k = jnp.where(lane < 4, 1.0, jnp.where(lane == 4, p_hi, jnp.where(
            lane == 5, p_mid, jnp.where(lane == 6, p_lo, 0.0))))
        for comp in range(2):
            qa_scr[comp, :, HEAD_DIM:] = aug_q.astype(BF16)
            ka_scr[comp, :, HEAD_DIM:] = aug_k.astype(BF16)

    for comp in range(2):
        cols = slice(comp * HEAD_DIM, (comp + 1) * HEAD_DIM)
        qa_scr[comp, :, 0:HEAD_DIM] = q_ref[:, cols]
        ka_scr[comp, :, 0:HEAD_DIM] = k_ref[:, cols]
    one_pass_ok = 2.0 * m_bound[0, 0] < ONE_PASS_RANGE

    def finish_block(i, outs):
        rows = slice(i * Q_BLK, (i + 1) * Q_BLK)
        o = outs[0] - outs[1]
        ms = jnp.mean(o * o, axis=-1, keepdims=True)
        o = o * lax.rsqrt(ms + NORM_EPS) * sub_ref[...] * (1.0 - lam_init)
        g = g_ref[rows, :].astype(F32)
        o_ref[rows, :] = (o * (g * _sigmoid(g))).astype(BF16)

    def weighted_values(comp, win, l):
        pv = jnp.dot(p_scr[:, 0:win], v_ref[0:win, :], preferred_element_type=F32)
        return pv * ((1.0 if comp == 0 else lam) / l)

    def one_pass():
        r = lax.broadcasted_iota(jnp.int32, (Q_BLK, K_TILE), 0)
        c = lax.broadcasted_iota(jnp.int32, (Q_BLK, K_TILE), 1)
        ahead = jnp.maximum(c - r, 0).astype(F32)
        allowed = lax.shift_right_arithmetic(c, chunk_bits) <= lax.shift_right_arithmetic(r, chunk_bits)
        s_scr[:, 0:K_TILE] = jnp.where(allowed, -2.0 * slope2 * ahead, NEG_INF)
        for i in range(n_blk):
            rows = slice(i * Q_BLK, (i + 1) * Q_BLK)
            n_t = (i + 1) * Q_BLK // K_TILE
            outs = []
            for comp in range(2):
                qa = qa_scr[comp, rows, :]
                for t in range(n_t):
                    z = lax.dot_general(qa, ka_scr[comp, t * K_TILE:(t + 1) * K_TILE, :], nt_dims,
                                        preferred_element_type=F32)
                    if t == n_t - 1:
                        z = z + s_scr[:, 0:K_TILE]
                    e_lo = jnp.exp2(z[:, 0:half])
                    e_hi = jnp.exp2(z[:, half:])
                    l_scr[...] = (e_lo + e_hi) if t == 0 else l_scr[...] + (e_lo + e_hi)
                    p_scr[:, t * K_TILE:t * K_TILE + half] = e_lo.astype(BF16)
                    p_scr[:, t * K_TILE + half:(t + 1) * K_TILE] = e_hi.astype(BF16)
                l = jnp.sum(l_scr[...], axis=-1, keepdims=True)
                outs.append(weighted_values(comp, n_t * K_TILE, l))
            finish_block(i, outs)

    def two_pass():
        diag0 = seq - Q_BLK
        r = lax.broadcasted_iota(jnp.int32, (Q_BLK, seq), 0)
        c = lax.broadcasted_iota(jnp.int32, (Q_BLK, seq), 1) - diag0
        dist = jnp.abs(r - c).astype(F32)
        allowed = jnp.logical_or(c < 0, lax.shift_right_arithmetic(c, chunk_bits)
                                 <= lax.shift_right_arithmetic(r, chunk_bits))
        bias_ref[...] = jnp.where(allowed, -slope2 * dist, NEG_INF)
        for i in range(n_blk):
            rows = slice(i * Q_BLK, (i + 1) * Q_BLK)
            n_t = (i + 1) * Q_BLK // K_TILE
            win = n_t * K_TILE
            outs = []
            for comp in range(2):
                cols = slice(comp * HEAD_DIM, (comp + 1) * HEAD_DIM)
                q = q_ref[rows, cols]

                def logits(t, q=q, win=win, cols=cols):
                    b0 = seq - win + t * K_TILE
                    return (lax.dot_general(q, k_ref[t * K_TILE:(t + 1) * K_TILE, cols], nt_dims,
                                            preferred_element_type=F32)
                            + bias_ref[:, b0:b0 + K_TILE])

                l = _softmax_tiles(logits, n_t, s_scr, p_scr, m_scr, l_scr)
                outs.append(weighted_values(comp, win, l))
            finish_block(i, outs)

    pl.when(one_pass_ok)(one_pass)
    pl.when(jnp.logical_not(one_pass_ok))(two_pass)


def _attn_b(u3, lq1, lk1, lq2, lk2, subln_g, qk_bound, lam_init):
    b, s, _ = u3.shape
    per = B_WIDTH // B_V_DIM
    body = functools.partial(_attn_b_body, seq=s, lam_init=lam_init)

    def col_spec(section):
        return pl.BlockSpec((None, s, B_V_DIM), lambda h, bi: (bi, 0, section * per + h))

    def vec_spec(n):
        return pl.BlockSpec((1, n), lambda h, bi: (0, 0))

    return pl.pallas_call(
        body,
        grid=(B_HEADS, b),
        in_specs=[col_spec(4), col_spec(5), col_spec(6), col_spec(7),
                  vec_spec(HEAD_DIM), vec_spec(HEAD_DIM), vec_spec(HEAD_DIM), vec_spec(HEAD_DIM),
                  vec_spec(B_V_DIM), vec_spec(1)],
        out_specs=pl.BlockSpec((None, s, B_V_DIM), lambda h, bi: (bi, 0, h)),
        out_shape=jax.ShapeDtypeStruct((b, s, B_WIDTH), BF16),
        scratch_shapes=[pltpu.VMEM((Q_BLK, s), F32), pltpu.VMEM((Q_BLK, s), F32),
                        pltpu.VMEM((Q_BLK, s), BF16), pltpu.VMEM((Q_BLK, K_TILE // 2), F32),
                        pltpu.VMEM((Q_BLK, K_TILE // 2), F32),
                        pltpu.VMEM((2, s, 2 * HEAD_DIM), BF16), pltpu.VMEM((2, s, 2 * HEAD_DIM), BF16)],
        compiler_params=_cparams(("parallel", "arbitrary")),
        name="attn_b",
    )(u3, u3, u3, u3, lq1, lk1, lq2, lk2, subln_g, qk_bound)


def _out_proj_body(ya_ref, yb_ref, w_ref, x_ref, o_ref, *, wa):
    acc = jnp.dot(ya_ref[...], w_ref[0:wa, :], preferred_element_type=F32)
    acc = acc + jnp.dot(yb_ref[...], w_ref[wa:, :], preferred_element_type=F32)
    o_ref[...] = x_ref[...] + acc


def _out_proj(ya, yb, w_bf16, x2, *, tm=512):
    t, d = x2.shape
    wa, wb = ya.shape[1], yb.shape[1]
    return pl.pallas_call(
        functools.partial(_out_proj_body, wa=wa),
        grid=(t // tm,),
        in_specs=[
            pl.BlockSpec((tm, wa), lambda i: (i, 0)),
            pl.BlockSpec((tm, wb), lambda i: (i, 0)),
            pl.BlockSpec((wa + wb, d), lambda i: (0, 0)),
            pl.BlockSpec((tm, d), lambda i: (i, 0)),
        ],
        out_specs=pl.BlockSpec((tm, d), lambda i: (i, 0)),
        out_shape=jax.ShapeDtypeStruct((t, d), F32),
        compiler_params=_cparams(("parallel",)),
        name="out_proj",
    )(ya, yb, w_bf16, x2)


def _scan_pitch(seq):
    seg = -(-seq // SCAN_SEGS)
    return seg + (4 - seg) % SUBLANES


def _lru_body(x_ref, g_ref, cw_ref, cb_ref, wa_ref, ba_ref, wx_ref, bx_ref, lam_ref,
              o_ref, x_s, a_s, b_s, *, seq, pitch, nblk):
    n_slab = nblk * LRU_BLOCK_W // LANES
    slab_per_blk = LRU_BLOCK_W // LANES
    pad = SCAN_SEGS * pitch - seq

    for sl in range(n_slab):
        x_s[sl, 0:SUBLANES, :] = jnp.zeros((SUBLANES, LANES), F32)
        x_s[sl, SUBLANES:, :] = x_ref[:, sl * LANES:(sl + 1) * LANES].astype(F32)

    for blk in range(nblk):
        parts = []
        for half in range(slab_per_blk):
            sl = blk * slab_per_blk + half
            cols = slice(sl * LANES, (sl + 1) * LANES)
            xc = cb_ref[:, cols] + cw_ref[CONV_W - 1:CONV_W, cols] * x_s[sl, SUBLANES:, :]
            for d in range(1, CONV_W):
                xc = xc + (cw_ref[CONV_W - 1 - d:CONV_W - d, cols]
                           * x_s[sl, SUBLANES - d:SUBLANES - d + seq, :])
            parts.append(xc)
        xc = jnp.concatenate(parts, axis=1)
        bcols = slice(blk * LRU_BLOCK_W, (blk + 1) * LRU_BLOCK_W)

        xb = xc.astype(BF16)
        t_r = jnp.tanh(jnp.dot(xb, wa_ref[blk], preferred_element_type=F32) + ba_ref[blk])
        t_i = jnp.tanh(jnp.dot(xb, wx_ref[blk], preferred_element_type=F32) + bx_ref[blk])
        gi = 0.5 * t_i + 0.5
        lam = lam_ref[:, bcols]
        softplus_neg = jnp.maximum(-lam, 0.0) + jnp.log1p(jnp.exp(-jnp.abs(lam)))
        half_rate = (-0.5 * LRU_C * LOG2E) * softplus_neg
        a = jnp.exp2(t_r * half_rate + half_rate)
        y = 1.0 - a * a
        mult = jnp.where(y > 0.0, y * lax.rsqrt(y), 0.0)
        bb = mult * (gi * xc)
        for half in range(slab_per_blk):
            sl = blk * slab_per_blk + half
            cols = slice(half * LANES, (half + 1) * LANES)
            a_s[sl, 0:seq, :] = a[:, cols]
            b_s[sl, 0:seq, :] = bb[:, cols]
            a_s[sl, seq:, :] = jnp.zeros((pad, LANES), F32)
            b_s[sl, seq:, :] = jnp.zeros((pad, LANES), F32)

    def pass1(j, carry):
        new = []
        for sl in range(n_slab):
            hh, pp = carry[2 * sl], carry[2 * sl + 1]
            idx = pl.ds(j, SCAN_SEGS, stride=pitch)
            aj = a_s[sl, idx, :]
            bj = b_s[sl, idx, :]
            hh = aj * hh + bj
            pp = aj * pp
            b_s[sl, idx, :] = hh
            a_s[sl, idx, :] = pp
            new += [hh, pp]
        return tuple(new)

    init = []
    for sl in range(n_slab):
        init += [jnp.zeros((SCAN_SEGS, LANES), F32), jnp.ones((SCAN_SEGS, LANES), F32)]
    ends = lax.fori_loop(0, pitch, pass1, tuple(init), unroll=2)

    seg_idx = lax.broadcasted_iota(jnp.int32, (SCAN_SEGS, LANES), 0)
    carries = []
    for sl in range(n_slab):
        h_end, p_end = ends[2 * sl], ends[2 * sl + 1]
        c = jnp.zeros((SCAN_SEGS, LANES), F32)
        for sgm in range(1, SCAN_SEGS):
            c = jnp.where(seg_idx == sgm, pltpu.roll(h_end + p_end * c, 1, axis=0), c)
        carries.append(c)

    def pass2(j, _):
        for sl in range(n_slab):
            idx = pl.ds(j, SCAN_SEGS, stride=pitch)
            b_s[sl, idx, :] = b_s[sl, idx, :] + a_s[sl, idx, :] * carries[sl]
        return 0

    lax.fori_loop(0, pitch, pass2, 0, unroll=4)

    for sl in range(n_slab):
        cols = slice(sl * LANES, (sl + 1) * LANES)
        hg = 0.5 * g_ref[:, cols].astype(F32)
        silu = hg * jnp.tanh(hg) + hg
        o_ref[:, cols] = (b_s[sl, 0:seq, :] * silu).astype(BF16)


def _rg_lru(u3, conv_w, conv_b, w_a, b_a, w_x, b_x, lam, *, blocks_per_step=2):
    b, s, _ = u3.shape
    pitch = _scan_pitch(s)
    nblk = blocks_per_step
    width = nblk * LRU_BLOCK_W
    n_slab = width // LANES
    n_steps = LRU_BLOCKS // nblk
    body = functools.partial(_lru_body, seq=s, pitch=pitch, nblk=nblk)

    def vec_spec(rows):
        return pl.BlockSpec((rows, width), lambda bi, n: (0, n))

    def blk_spec(rows):
        return pl.BlockSpec((nblk, rows, LRU_BLOCK_W), lambda bi, n: (n, 0, 0))

    return pl.pallas_call(
        body,
        grid=(b, n_steps),
        in_specs=[
            pl.BlockSpec((None, s, width), lambda bi, n: (bi, 0, n)),
            pl.BlockSpec((None, s, width), lambda bi, n: (bi, 0, n_steps + n)),
            vec_spec(CONV_W), vec_spec(1),
            blk_spec(LRU_BLOCK_W), blk_spec(1), blk_spec(LRU_BLOCK_W), blk_spec(1),
            vec_spec(1),
        ],
        out_specs=pl.BlockSpec((None, s, width), lambda bi, n: (bi, 0, n)),
        out_shape=jax.ShapeDtypeStruct((b, s, LRU_WIDTH), BF16),
        scratch_shapes=[pltpu.VMEM((n_slab, s + SUBLANES, LANES), F32),
                        pltpu.VMEM((n_slab, SCAN_SEGS * pitch, LANES), F32),
                        pltpu.VMEM((n_slab, SCAN_SEGS * pitch, LANES), F32)],
        compiler_params=_cparams(("parallel", "arbitrary")),
        name="rg_lru",
    )(u3, u3, conv_w, conv_b, w_a, b_a, w_x, b_x, lam)


def _s5_body(u_ref, bm_ref, cm_ref, are_ref, aim_ref, dskip_ref, y_ref,
             su, ut, bre, bim, hre, him, *, tt, pitch, n_sub):
    nb = SCAN_SEGS
    n_q = SSM_WIDTH // LANES
    n_state = SSM_GROUPS * SSM_STATE
    half_w = SSM_WIDTH // 2
    half_s = n_state // 2

    @pl.when(pl.program_id(1) == 0)
    def _():
        hre[...] = jnp.zeros_like(hre)
        him[...] = jnp.zeros_like(him)

    for b in range(nb):
        ub = u_ref[b].astype(F32)
        for q in range(n_q):
            su[q, b * pitch:b * pitch + tt, :] = ub[:, q * LANES:(q + 1) * LANES]

    def to_time_major(t, _):
        rows = pl.ds(pl.multiple_of(t * nb, nb), nb)
        for q in range(n_q):
            ut[rows, q * LANES:(q + 1) * LANES] = su[q, pl.ds(t, nb, stride=pitch), :]
        return 0

    lax.fori_loop(0, tt, to_time_major, 0)

    sub_rows = (tt // n_sub) * nb
    n_slab = n_state // LANES
    group = 4

    def input_map(j):
        r = slice(j * sub_rows, (j + 1) * sub_rows)
        u_bf = ut[r, :].astype(BF16)
        for hf in range(2):
            uh = u_bf[:, hf * half_w:(hf + 1) * half_w]
            cols = slice(hf * half_s, (hf + 1) * half_s)
            bre[r, cols] = jnp.dot(uh, bm_ref[hf, :, 0:half_s], preferred_element_type=F32)
            bim[r, cols] = jnp.dot(uh, bm_ref[hf, :, half_s:], preferred_element_type=F32)

    def scan(j):
        for k0 in range(0, n_slab, group):
            sl = [slice((k0 + k) * LANES, (k0 + k + 1) * LANES) for k in range(group)]
            a_r = [jnp.broadcast_to(are_ref[:, c], (nb, LANES)) for c in sl]
            a_i = [jnp.broadcast_to(aim_ref[:, c], (nb, LANES)) for c in sl]
            if j == 0:
                x = [(hre[:, c], him[:, c]) for c in sl]
            else:
                prev = slice(j * sub_rows - nb, j * sub_rows)
                x = [(bre[prev, c], bim[prev, c]) for c in sl]
            for t in range(tt // n_sub):
                rows = slice(j * sub_rows + t * nb, j * sub_rows + (t + 1) * nb)
                for k in range(group):
                    x_r, x_i = x[k]
                    n_r = a_r[k] * x_r - a_i[k] * x_i + bre[rows, sl[k]]
                    n_i = a_r[k] * x_i + a_i[k] * x_r + bim[rows, sl[k]]
                    bre[rows, sl[k]] = n_r
                    bim[rows, sl[k]] = n_i
                    x[k] = (n_r, n_i)

    def output_map(j):
        r = slice(j * sub_rows, (j + 1) * sub_rows)
        for hf in range(2):
            cols = slice(hf * half_s, (hf + 1) * half_s)
            y = jnp.dot(bre[r, cols].astype(BF16), cm_ref[hf, 0:half_s, :], preferred_element_type=F32)
            y = y + jnp.dot(bim[r, cols].astype(BF16), cm_ref[hf, half_s:, :],
                            preferred_element_type=F32)
            oc = slice(hf * half_w, (hf + 1) * half_w)
            ut[r, oc] = y + dskip_ref[:, oc] * ut[r, oc]

    input_map(0)
    for j in range(n_sub):
        if j + 1 < n_sub:
            input_map(j + 1)
        scan(j)
        if j > 0:
            output_map(j - 1)
    last = slice(tt * nb - nb, tt * nb)
    hre[...] = bre[last, :]
    him[...] = bim[last, :]
    output_map(n_sub - 1)

    def to_batch_major(t, _):
        rows = pl.ds(pl.multiple_of(t * nb, nb), nb)
        for q in range(n_q):
            su[q, pl.ds(t, nb, stride=pitch), :] = ut[rows, q * LANES:(q + 1) * LANES]
        return 0

    lax.fori_loop(0, tt, to_batch_major, 0)
    for b in range(nb):
        for q in range(n_q):
            y_ref[b, :, q * LANES:(q + 1) * LANES] = su[q, b * pitch:b * pitch + tt, :]


def _s5_scan(u3, b_mat, c_mat, a_re, a_im, d_skip, *, tt=128, n_sub=2):
    b, s, _ = u3.shape
    nb = SCAN_SEGS
    pitch = tt + 4
    n_state = SSM_GROUPS * SSM_STATE
    din_blk = 2 * LRU_WIDTH // SSM_WIDTH
    body = functools.partial(_s5_body, tt=tt, pitch=pitch, n_sub=n_sub)
    full = lambda shape: pl.BlockSpec(shape, lambda bi, ti: (0,) * len(shape))
    return pl.pallas_call(
        body,
        grid=(b // nb, s // tt),
        in_specs=[
            pl.BlockSpec((nb, tt, SSM_WIDTH), lambda bi, ti: (bi, ti, din_blk)),
            full(b_mat.shape), full(c_mat.shape), full((1, n_state)), full((1, n_state)),
            full((1, SSM_WIDTH)),
        ],
        out_specs=pl.BlockSpec((nb, tt, SSM_WIDTH), lambda bi, ti: (bi, ti, 0)),
        out_shape=jax.ShapeDtypeStruct((b, s, SSM_WIDTH), F32),
        scratch_shapes=[
            pltpu.VMEM((SSM_WIDTH // LANES, nb * pitch, LANES), F32),
            pltpu.VMEM((tt * nb, SSM_WIDTH), F32),
            pltpu.VMEM((tt * nb, n_state), F32),
            pltpu.VMEM((tt * nb, n_state), F32),
            pltpu.VMEM((nb, n_state), F32),
            pltpu.VMEM((nb, n_state), F32),
        ],
        compiler_params=_cparams(("parallel", "arbitrary")),
        name="s5_scan",
    )(u3, b_mat, c_mat, a_re, a_im, d_skip)


def _s5_operators(a_re, a_im, b_re, b_im, c_re, c_im, log_dt):
    a_re, a_im = a_re.astype(F32), a_im.astype(F32)
    dt = jnp.exp(log_dt.astype(F32))[:, None]
    mag = jnp.exp(a_re * dt)
    ab_re, ab_im = mag * jnp.cos(a_im * dt), mag * jnp.sin(a_im * dt)
    den = a_re * a_re + a_im * a_im
    f_re = ((ab_re - 1.0) * a_re + ab_im * a_im) / den
    f_im = (ab_im * a_re - (ab_re - 1.0) * a_im) / den
    b_re, b_im = b_re.astype(F32), b_im.astype(F32)
    bb_re = f_re[..., None] * b_re - f_im[..., None] * b_im
    bb_im = f_re[..., None] * b_im + f_im[..., None] * b_re

    gh = SSM_GROUPS // 2
    eye = jnp.eye(gh, dtype=F32)

    def in_map(bb):
        t = bb.reshape(2, gh, SSM_STATE, SSM_GROUP)
        t = t.transpose(0, 1, 3, 2)[:, :, :, None, :] * eye[None, :, None, :, None]
        return t.reshape(2, gh * SSM_GROUP, gh * SSM_STATE)

    def out_map(cc):
        t = cc.reshape(2, gh, SSM_GROUP, SSM_STATE)
        t = t.transpose(0, 1, 3, 2)[:, :, :, None, :] * eye[None, :, None, :, None]
        return t.reshape(2, gh * SSM_STATE, gh * SSM_GROUP)

    b_mat = jnp.concatenate([in_map(bb_re), in_map(bb_im)], axis=2).astype(BF16)
    c_mat = jnp.concatenate([out_map(c_re.astype(F32)), out_map(-c_im.astype(F32))],
                            axis=1).astype(BF16)
    n_state = SSM_GROUPS * SSM_STATE
    return b_mat, c_mat, ab_re.reshape(1, n_state), ab_im.reshape(1, n_state)


def _rec_out_body(yc_ref, y5_ref, dg_ref, wglu_ref, bglu_ref, w_ref, x_ref, o_ref, *, wc):
    yd = y5_ref[...]
    inner = math.sqrt(2.0 / math.pi) * (yd + 0.044715 * (yd * yd * yd))
    yd = 0.5 * yd * (1.0 + jnp.tanh(inner))
    glu = jnp.dot(yd.astype(BF16), wglu_ref[...], preferred_element_type=F32) + bglu_ref[...]
    yd = yd * _sigmoid(glu)
    g = dg_ref[...].astype(F32)
    yd = (yd * (g * _sigmoid(g))).astype(BF16)
    acc = jnp.dot(yc_ref[...], w_ref[0:wc, :], preferred_element_type=F32)
    acc = acc + jnp.dot(yd, w_ref[wc:, :], preferred_element_type=F32)
    o_ref[...] = x_ref[...] + acc


def _rec_out(yc, y5, u2, w_glu, b_glu, w_bf16, x2, *, tm=512):
    t, d = x2.shape
    wc = yc.shape[1]
    gate_blk = 2 * LRU_WIDTH // SSM_WIDTH + 1
    return pl.pallas_call(
        functools.partial(_rec_out_body, wc=wc),
        grid=(t // tm,),
        in_specs=[
            pl.BlockSpec((tm, wc), lambda i: (i, 0)),
            pl.BlockSpec((tm, SSM_WIDTH), lambda i: (i, 0)),
            pl.BlockSpec((tm, SSM_WIDTH), lambda i: (i, gate_blk)),
            pl.BlockSpec((SSM_WIDTH, SSM_WIDTH), lambda i: (0, 0)),
            pl.BlockSpec((1, SSM_WIDTH), lambda i: (0, 0)),
            pl.BlockSpec((wc + SSM_WIDTH, d), lambda i: (0, 0)),
            pl.BlockSpec((tm, d), lambda i: (i, 0)),
        ],
        out_specs=pl.BlockSpec((tm, d), lambda i: (i, 0)),
        out_shape=jax.ShapeDtypeStruct((t, d), F32),
        compiler_params=_cparams(("parallel",)),
        name="rec_out",
    )(yc, y5, u2, w_glu, b_glu, w_bf16, x2)


def _attention_layer(x2, b, s, norm_g, w_in, q_g_a, k_g_a, rel_bias, q_g_b, k_g_b,
                     lq1, lk1, lq2, lk2, subln_g, w_out, layer_idx):
    scale = HEAD_DIM ** -0.5 * LOG2E
    ones = jnp.ones((1024,), F32)
    col_gain = jnp.concatenate([
        jnp.tile(q_g_a.astype(F32) * scale, A_HEADS), jnp.tile(k_g_a.astype(F32), A_HEADS), ones, ones,
        jnp.tile(q_g_b.astype(F32).reshape(-1) * scale, B_HEADS),
        jnp.tile(k_g_b.astype(F32).reshape(-1), B_HEADS), ones, ones])[None, :]
    tn = 2048
    u = _norm_proj(x2, norm_g.astype(F32)[None, :], w_in.astype(BF16), col_gain,
                   norm_tiles=(0, 2), tn=tn)
    u3 = u.reshape(b, s, ATTN_IN)
    bound_a = (BOUND_SLACK * HEAD_DIM * scale * jnp.max(jnp.abs(q_g_a.astype(F32)))
               * jnp.max(jnp.abs(k_g_a.astype(F32))))
    ya = _attn_a(u3, *_attn_a_bias(rel_bias, bound_a))
    lam_init = 0.8 - 0.6 * math.exp(-0.3 * layer_idx)
    row = lambda v: v.astype(F32)[None, :]
    qk_bound = (BOUND_SLACK * HEAD_DIM * scale * jnp.max(jnp.abs(q_g_b.astype(F32)))
                * jnp.max(jnp.abs(k_g_b.astype(F32)))).reshape(1, 1)
    yb = _attn_b(u3, row(lq1), row(lk1), row(lq2), row(lk2), row(subln_g), qk_bound, lam_init)
    return _out_proj(ya.reshape(b * s, A_WIDTH), yb.reshape(b * s, B_WIDTH), w_out.astype(BF16), x2)


def _recurrent_layer(x2, b, s, norm_g, w_in, conv_w, conv_b, w_a, b_a, w_x, b_x, lru_lam,
                     a_re, a_im, b_re, b_im, c_re, c_im, d_skip, log_dt, w_glu, b_glu, w_out):
    u = _norm_proj(x2, norm_g.astype(F32)[None, :], w_in.astype(BF16),
                   jnp.ones((1, REC_IN), F32), norm_tiles=(), tn=2048)
    u3 = u.reshape(b, s, REC_IN)
    yc = _rg_lru(u3, conv_w.astype(F32), conv_b.astype(F32)[None, :],
                 (0.5 * w_a).astype(BF16), 0.5 * b_a.astype(F32)[:, None, :],
                 (0.5 * w_x).astype(BF16), 0.5 * b_x.astype(F32)[:, None, :],
                 lru_lam.astype(F32)[None, :])

    b_mat, c_mat, ab_re, ab_im = _s5_operators(a_re, a_im, b_re, b_im, c_re, c_im, log_dt)
    y5 = _s5_scan(u3, b_mat, c_mat, ab_re, ab_im, d_skip.astype(F32)[None, :])
    return _rec_out(yc.reshape(b * s, LRU_WIDTH), y5.reshape(b * s, SSM_WIDTH), u,
                    w_glu.astype(BF16), b_glu.astype(F32)[None, :], w_out.astype(BF16), x2)


def kernel(x, attn_norm_g, attn_w_in, a_q_g, a_k_g, a_rel_bias, b_q_g, b_k_g, b_lam_q1, b_lam_k1,
           b_lam_q2, b_lam_k2, b_subln_g, attn_w_out, rec_norm_g, rec_w_in, lru_conv_w, lru_conv_b,
           lru_w_a, lru_b_a, lru_w_x, lru_b_x, lru_lambda, ssm_a_re, ssm_a_im, ssm_b_re, ssm_b_im,
           ssm_c_re, ssm_c_im, ssm_d, ssm_log_dt, ssm_w_glu, ssm_b_glu, rec_w_out):
    b, s, d = x.shape
    depth = attn_norm_g.shape[0] + rec_norm_g.shape[0]
    x2 = x.reshape(b * s, d)
    for layer in range(depth):
        j = layer // 2
        if layer % 2 == 0:
            x2 = _attention_layer(x2, b, s, attn_norm_g[j], attn_w_in[j], a_q_g[j], a_k_g[j],
                                  a_rel_bias[j], b_q_g[j], b_k_g[j], b_lam_q1[j], b_lam_k1[j],
                                  b_lam_q2[j], b_lam_k2[j], b_subln_g[j], attn_w_out[j], layer)
        else:
            x2 = _recurrent_layer(x2, b, s, rec_norm_g[j], rec_w_in[j], lru_conv_w[j], lru_conv_b[j],
                                  lru_w_a[j], lru_b_a[j], lru_w_x[j], lru_b_x[j], lru_lambda[j],
                                  ssm_a_re[j], ssm_a_im[j], ssm_b_re[j], ssm_b_im[j],
                                  ssm_c_re[j], ssm_c_im[j], ssm_d[j], ssm_log_dt[j],
                                  ssm_w_glu[j], ssm_b_glu[j], rec_w_out[j])
    return x2.reshape(b, s, d)
```

```python
import functools
import math

import jax
import jax.numpy as jnp
from jax import lax
from jax.experimental import pallas as pl
from jax.experimental.pallas import tpu as pltpu

F32 = jnp.float32
BF16 = jnp.bfloat16

D_MODEL = 2048
CHUNK = 64
NEG_INF = -1e30
NORM_EPS = 1e-6

HEAD_DIM = 128
A_HEADS = 8
A_WIDTH = A_HEADS * HEAD_DIM
A_LEFT_CHUNKS = 8
A_MAX_REL = 128
B_HEADS = 4
B_V_DIM = 2 * HEAD_DIM
B_WIDTH = B_HEADS * B_V_DIM
ATTN_IN = 8 * 1024

LRU_BLOCKS = 6
LRU_BLOCK_W = 256
LRU_WIDTH = LRU_BLOCKS * LRU_BLOCK_W
CONV_W = 4
LRU_C = 8.0
SSM_GROUP = 16
SSM_GROUPS = 32
SSM_WIDTH = SSM_GROUPS * SSM_GROUP
SSM_STATE = 64
REC_IN = 2 * LRU_WIDTH + 2 * SSM_WIDTH

LANES = 128
SUBLANES = 8
MXU_N = 256
VMEM_LIMIT = 56 * 1024 * 1024

Q_BLK = 256
K_TILE = 256
LOG2E = 1.0 / math.log(2.0)
BOUND_SLACK = 1.0 + 2.0 ** -6
ONE_PASS_RANGE = 100.0
A_WIN = A_LEFT_CHUNKS * CHUNK + Q_BLK
SCAN_SEGS = SUBLANES


def _sigmoid(x):
    return 0.5 * jnp.tanh(0.5 * x) + 0.5


def _cparams(sem):
    return pltpu.CompilerParams(dimension_semantics=sem, vmem_limit_bytes=VMEM_LIMIT)


def _norm_proj_body(x_ref, w_ref, cg_ref, o_ref, h_ref, r_ref, *, norm_tiles, tn):
    j = pl.program_id(1)

    @pl.when(j == 0)
    def _():
        x = x_ref[...]
        ms = jnp.mean(x * x, axis=-1, keepdims=True)
        r_ref[...] = jnp.broadcast_to(lax.rsqrt(ms + NORM_EPS), r_ref.shape)
        h_ref[...] = x.astype(BF16)

    def step(head_norm):
        for n in range(tn // MXU_N):
            acc = jnp.dot(h_ref[...], w_ref[:, n * MXU_N:(n + 1) * MXU_N],
                          preferred_element_type=F32)
            for hh in range(MXU_N // HEAD_DIM):
                sl = slice(n * MXU_N + hh * HEAD_DIM, n * MXU_N + (hh + 1) * HEAD_DIM)
                blk = acc[:, hh * HEAD_DIM:(hh + 1) * HEAD_DIM] * r_ref[...]
                if head_norm:
                    ms = jnp.mean(blk * blk, axis=-1, keepdims=True)
                    blk = blk * lax.rsqrt(ms + NORM_EPS) * cg_ref[:, sl]
                o_ref[:, sl] = blk.astype(BF16)

    if norm_tiles:
        is_norm = functools.reduce(jnp.logical_or, [j == t for t in norm_tiles])
        pl.when(is_norm)(functools.partial(step, True))
        pl.when(jnp.logical_not(is_norm))(functools.partial(step, False))
    else:
        step(False)


def _norm_proj(x2, w_bf16, col_gain, norm_tiles, *, tm=1024, tn=1024):
    t, d = x2.shape
    n = w_bf16.shape[1]
    body = functools.partial(_norm_proj_body, norm_tiles=tuple(norm_tiles), tn=tn)
    return pl.pallas_call(
        body,
        grid=(t // tm, n // tn),
        in_specs=[
            pl.BlockSpec((tm, d), lambda i, j: (i, 0)),
            pl.BlockSpec((d, tn), lambda i, j: (0, j)),
            pl.BlockSpec((1, tn), lambda i, j: (0, j)),
        ],
        out_specs=pl.BlockSpec((tm, tn), lambda i, j: (i, j)),
        out_shape=jax.ShapeDtypeStruct((t, n), BF16),
        scratch_shapes=[pltpu.VMEM((tm, d), BF16), pltpu.VMEM((tm, LANES), F32)],
        compiler_params=_cparams(("parallel", "arbitrary")),
        name="norm_proj",
    )(x2, w_bf16, col_gain)


def _softmax_tiles(logits_fn, n_t, s_scr, p_scr, m_scr, l_scr):
    half = K_TILE // 2
    for t in range(n_t):
        s = logits_fn(t)
        s_scr[:, t * K_TILE:(t + 1) * K_TILE] = s
        hm = jnp.maximum(s[:, 0:half], s[:, half:])
        m_scr[...] = hm if t == 0 else jnp.maximum(m_scr[...], hm)
    m = jnp.max(m_scr[...], axis=-1, keepdims=True)
    m_scr[...] = jnp.broadcast_to(m, m_scr.shape)
    for t in range(n_t):
        lo = slice(t * K_TILE, t * K_TILE + half)
        hi = slice(t * K_TILE + half, (t + 1) * K_TILE)
        p_lo = jnp.exp2(s_scr[:, lo] - m_scr[...])
        p_hi = jnp.exp2(s_scr[:, hi] - m_scr[...])
        if l_scr is not None:
            l_scr[...] = (p_lo + p_hi) if t == 0 else l_scr[...] + (p_lo + p_hi)
        p_scr[:, lo] = p_lo.astype(BF16)
        p_scr[:, hi] = p_hi.astype(BF16)
    if l_scr is None:
        return None
    return jnp.sum(l_scr[...], axis=-1, keepdims=True)


def _attn_a_body(q_ref, k_ref, v_ref, g_ref, bias_ref, span_ref, o_ref, s_scr, p_scr, m_scr, v1_scr, *,
                 heads, seq):
    one_pass_ok = span_ref[0, 0] < ONE_PASS_RANGE
    nt_dims = (((1,), (1,)), ((), ()))

    def run(one_pass):
        for hh in range(heads):
            cols = slice(hh * HEAD_DIM, (hh + 1) * HEAD_DIM)
            v1_scr[hh, :, 0:HEAD_DIM] = v_ref[:, cols]
            v1_scr[hh, :, HEAD_DIM:] = jnp.ones((seq, HEAD_DIM), BF16)
            for i in range(seq // Q_BLK):
                rows = slice(i * Q_BLK, (i + 1) * Q_BLK)
                ks = max(0, i * Q_BLK - A_LEFT_CHUNKS * CHUNK)
                win = (i + 1) * Q_BLK - ks
                q = q_ref[rows, cols]

                def logits(t, q=q, ks=ks, win=win, hh=hh, cols=cols):
                    kt = k_ref[ks + t * K_TILE:ks + (t + 1) * K_TILE, cols]
                    b0 = A_WIN - win + t * K_TILE
                    return (lax.dot_general(q, kt, nt_dims, preferred_element_type=F32)
                            + bias_ref[hh, :, b0:b0 + K_TILE])

                if one_pass:
                    for t in range(win // K_TILE):
                        p_scr[:, t * K_TILE:(t + 1) * K_TILE] = jnp.exp2(logits(t)).astype(BF16)
                else:
                    _softmax_tiles(logits, win // K_TILE, s_scr, p_scr, m_scr, None)
                ol = jnp.dot(p_scr[:, 0:win], v1_scr[hh, ks:ks + win, :], preferred_element_type=F32)
                o = ol[:, 0:HEAD_DIM] / ol[:, HEAD_DIM:HEAD_DIM + 1]
                g = g_ref[rows, cols].astype(F32)
                o_ref[rows, cols] = (o * (g * _sigmoid(g))).astype(BF16)

    pl.when(one_pass_ok)(functools.partial(run, True))
    pl.when(jnp.logical_not(one_pass_ok))(functools.partial(run, False))


def _attn_a(u3, bias, span, *, heads_per_step=2):
    b, s, _ = u3.shape
    w = heads_per_step * HEAD_DIM
    per = A_WIDTH // w
    body = functools.partial(_attn_a_body, heads=heads_per_step, seq=s)

    def col_spec(section):
        return pl.BlockSpec((None, s, w), lambda bi, hg: (bi, 0, section * per + hg))

    return pl.pallas_call(
        body,
        grid=(b, per),
        in_specs=[
            col_spec(0), col_spec(1), col_spec(2), col_spec(3),
            pl.BlockSpec((heads_per_step, Q_BLK, A_WIN), lambda bi, hg: (hg, 0, 0)),
            pl.BlockSpec((1, 1), lambda bi, hg: (0, 0)),
        ],
        out_specs=pl.BlockSpec((None, s, w), lambda bi, hg: (bi, 0, hg)),
        out_shape=jax.ShapeDtypeStruct((b, s, A_WIDTH), BF16),
        scratch_shapes=[pltpu.VMEM((Q_BLK, A_WIN), F32), pltpu.VMEM((Q_BLK, A_WIN), BF16),
                        pltpu.VMEM((Q_BLK, K_TILE // 2), F32),
                        pltpu.VMEM((heads_per_step, s, 2 * HEAD_DIM), BF16)],
        compiler_params=_cparams(("parallel", "arbitrary")),
        name="attn_a",
    )(u3, u3, u3, u3, bias, span)


def _attn_a_bias(rel_bias, qk_bound):
    h = rel_bias.shape[0]
    rb = rel_bias.astype(F32)
    n_v = Q_BLK + A_WIN - 1
    lo = A_LEFT_CHUNKS * CHUNK + Q_BLK - 1 - A_MAX_REL
    v = jnp.concatenate([jnp.broadcast_to(rb[:, :1], (h, lo + 1)), rb[:, 1:2 * A_MAX_REL],
                         jnp.broadcast_to(rb[:, -1:], (h, n_v - lo - 2 * A_MAX_REL))], axis=1)
    v = jnp.roll(v, -(Q_BLK - 1), axis=1)
    flat = jnp.tile(v, (1, Q_BLK))[:, :Q_BLK * (n_v - 1)]
    toep = flat.reshape(h, Q_BLK, n_v - 1)[:, :, :A_WIN]
    qc = jnp.arange(Q_BLK)[:, None] // CHUNK
    kc = jnp.floor_divide(jnp.arange(A_WIN)[None, :] - A_LEFT_CHUNKS * CHUNK, CHUNK)
    allowed = (kc <= qc) & (kc >= qc - A_LEFT_CHUNKS)
    rb_max = jnp.max(rb, axis=1)
    shift = (qk_bound + LOG2E * rb_max)[:, None, None]
    span = jnp.max(2.0 * qk_bound + LOG2E * (rb_max - rb[:, A_MAX_REL])).reshape(1, 1)
    return jnp.where(allowed[None], toep * LOG2E - shift, NEG_INF), span


def _attn_b_body(q_ref, k_ref, v_ref, g_ref, lq1_ref, lk1_ref, lq2_ref, lk2_ref, sub_ref, mb_ref,
                 o_ref, bias_ref, s_scr, p_scr, m_scr, l_scr, qa_scr, ka_scr, *, seq, lam_init):
    h = pl.program_id(0)
    n_blk = seq // Q_BLK
    half = K_TILE // 2
    chunk_bits = CHUNK.bit_length() - 1
    nt_dims = (((1,), (1,)), ((), ()))
    slope2 = LOG2E * jnp.exp2((-8.0 / B_HEADS) * jnp.full((1, 1), h + 1, jnp.int32).astype(F32))

    lam = (jnp.exp(jnp.sum(lq1_ref[...] * lk1_ref[...], axis=-1, keepdims=True))
           - jnp.exp(jnp.sum(lq2_ref[...] * lk2_ref[...], axis=-1, keepdims=True)) + lam_init)

    m_bound = mb_ref[...]

    @pl.when(pl.program_id(1) == 0)
    def _():
        pos = slope2 * lax.broadcasted_iota(jnp.int32, (seq, 1), 0).astype(F32)
        p_hi = pos.astype(BF16).astype(F32)
        p_mid = (pos - p_hi).astype(BF16).astype(F32)
        p_lo = pos - p_hi - p_mid
        lane = lax.broadcasted_iota(jnp.int32, (seq, HEAD_DIM), 1)
        aug_q = jnp.where(lane == 0, -m_bound, jnp.where(lane == 1, -p_hi, jnp.where(
            lane == 2, -p_mid, jnp.where(lane == 3, -p_lo, jnp.where(lane < 7, 1.0, 0.0)))))
        aug_k = jnp.where(lane < 4, 1.0, jnp.where(lane == 4, p_hi, jnp.where(
            lane == 5, p_mid, jnp.where(lane == 6, p_lo, 0.0))))
        for comp in range(2):
            qa_scr[comp, :, HEAD_DIM:] = aug_q.astype(BF16)
            ka_scr[comp, :, HEAD_DIM:] = aug_k.astype(BF16)

    for comp in range(2):
        cols = slice(comp * HEAD_DIM, (comp + 1) * HEAD_DIM)
        qa_scr[comp, :, 0:HEAD_DIM] = q_ref[:, cols]
        ka_scr[comp, :, 0:HEAD_DIM] = k_ref[:, cols]
    one_pass_ok = 2.0 * m_bound[0, 0] < ONE_PASS_RANGE

    def finish_block(i, outs):
        rows = slice(i * Q_BLK, (i + 1) * Q_BLK)
        o = outs[0] - outs[1]
        ms = jnp.mean(o * o, axis=-1, keepdims=True)
        o = o * lax.rsqrt(ms + NORM_EPS) * sub_ref[...] * (1.0 - lam_init)
        g = g_ref[rows, :].astype(F32)
        o_ref[rows, :] = (o * (g * _sigmoid(g))).astype(BF16)

    def weighted_values(comp, win, l):
        pv = jnp.dot(p_scr[:, 0:win], v_ref[0:win, :], preferred_element_type=F32)
        return pv * ((1.0 if comp == 0 else lam) / l)

    def one_pass():
        r = lax.broadcasted_iota(jnp.int32, (Q_BLK, K_TILE), 0)
        c = lax.broadcasted_iota(jnp.int32, (Q_BLK, K_TILE), 1)
        ahead = jnp.maximum(c - r, 0).astype(F32)
        allowed = lax.shift_right_arithmetic(c, chunk_bits) <= lax.shift_right_arithmetic(r, chunk_bits)
        s_scr[:, 0:K_TILE] = jnp.where(allowed, -2.0 * slope2 * ahead, NEG_INF)
        for i in range(n_blk):
            rows = slice(i * Q_BLK, (i + 1) * Q_BLK)
            n_t = (i + 1) * Q_BLK // K_TILE
            outs = []
            for comp in range(2):
                qa = qa_scr[comp, rows, :]
                for t in range(n_t):
                    z = lax.dot_general(qa, ka_scr[comp, t * K_TILE:(t + 1) * K_TILE, :], nt_dims,
                                        preferred_element_type=F32)
                    if t == n_t - 1:
                        z = z + s_scr[:, 0:K_TILE]
                    e_lo = jnp.exp2(z[:, 0:half])
                    e_hi = jnp.exp2(z[:, half:])
                    l_scr[...] = (e_lo + e_hi) if t == 0 else l_scr[...] + (e_lo + e_hi)
                    p_scr[:, t * K_TILE:t * K_TILE + half] = e_lo.astype(BF16)
                    p_scr[:, t * K_TILE + half:(t + 1) * K_TILE] = e_hi.astype(BF16)
                l = jnp.sum(l_scr[...], axis=-1, keepdims=True)
                outs.append(weighted_values(comp, n_t * K_TILE, l))
            finish_block(i, outs)

    def two_pass():
        diag0 = seq - Q_BLK
        r = lax.broadcasted_iota(jnp.int32, (Q_BLK, seq), 0)
        c = lax.broadcasted_iota(jnp.int32, (Q_BLK, seq), 1) - diag0
        dist = jnp.abs(r - c).astype(F32)
        allowed = jnp.logical_or(c < 0, lax.shift_right_arithmetic(c, chunk_bits)
                                 <= lax.shift_right_arithmetic(r, chunk_bits))
        bias_ref[...] = jnp.where(allowed, -slope2 * dist, NEG_INF)
        for i in range(n_blk):
            rows = slice(i * Q_BLK, (i + 1) * Q_BLK)
            n_t = (i + 1) * Q_BLK // K_TILE
            win = n_t * K_TILE
            outs = []
            for comp in range(2):
                cols = slice(comp * HEAD_DIM, (comp + 1) * HEAD_DIM)
                q = q_ref[rows, cols]

                def logits(t, q=q, win=win, cols=cols):
                    b0 = seq - win + t * K_TILE
                    return (lax.dot_general(q, k_ref[t * K_TILE:(t + 1) * K_TILE, cols], nt_dims,
                                            preferred_element_type=F32)
                            + bias_ref[:, b0:b0 + K_TILE])

                l = _softmax_tiles(logits, n_t, s_scr, p_scr, m_scr, l_scr)
                outs.append(weighted_values(comp, win, l))
            finish_block(i, outs)

    pl.when(one_pass_ok)(one_pass)
    pl.when(jnp.logical_not(one_pass_ok))(two_pass)


def _attn_b(u3, lq1, lk1, lq2, lk2, subln_g, qk_bound, lam_init):
    b, s, _ = u3.shape
    per = B_WIDTH // B_V_DIM
    body = functools.partial(_attn_b_body, seq=s, lam_init=lam_init)

    def col_spec(section):
        return pl.BlockSpec((None, s, B_V_DIM), lambda h, bi: (bi, 0, section * per + h))

    def vec_spec(n):
        return pl.BlockSpec((1, n), lambda h, bi: (0, 0))

    return pl.pallas_call(
        body,
        grid=(B_HEADS, b),
        in_specs=[col_spec(4), col_spec(5), col_spec(6), col_spec(7),
                  vec_spec(HEAD_DIM), vec_spec(HEAD_DIM), vec_spec(HEAD_DIM), vec_spec(HEAD_DIM),
                  vec_spec(B_V_DIM), vec_spec(1)],
        out_specs=pl.BlockSpec((None, s, B_V_DIM), lambda h, bi: (bi, 0, h)),
        out_shape=jax.ShapeDtypeStruct((b, s, B_WIDTH), BF16),
        scratch_shapes=[pltpu.VMEM((Q_BLK, s), F32), pltpu.VMEM((Q_BLK, s), F32),
                        pltpu.VMEM((Q_BLK, s), BF16), pltpu.VMEM((Q_BLK, K_TILE // 2), F32),
                        pltpu.VMEM((Q_BLK, K_TILE // 2), F32),
                        pltpu.VMEM((2, s, 2 * HEAD_DIM), BF16), pltpu.VMEM((2, s, 2 * HEAD_DIM), BF16)],
        compiler_params=_cparams(("parallel", "arbitrary")),
        name="attn_b",
    )(u3, u3, u3, u3, lq1, lk1, lq2, lk2, subln_g, qk_bound)


def _out_proj_body(ya_ref, yb_ref, w_ref, x_ref, o_ref, *, wa):
    acc = jnp.dot(ya_ref[...], w_ref[0:wa, :], preferred_element_type=F32)
    acc = acc + jnp.dot(yb_ref[...], w_ref[wa:, :], preferred_element_type=F32)
    o_ref[...] = x_ref[...] + acc


def _out_proj(ya, yb, w_bf16, x2, *, tm=512):
    t, d = x2.shape
    wa, wb = ya.shape[1], yb.shape[1]
    return pl.pallas_call(
        functools.partial(_out_proj_body, wa=wa),
        grid=(t // tm,),
        in_specs=[
            pl.BlockSpec((tm, wa), lambda i: (i, 0)),
            pl.BlockSpec((tm, wb), lambda i: (i, 0)),
            pl.BlockSpec((wa + wb, d), lambda i: (0, 0)),
            pl.BlockSpec((tm, d), lambda i: (i, 0)),
        ],
        out_specs=pl.BlockSpec((tm, d), lambda i: (i, 0)),
        out_shape=jax.ShapeDtypeStruct((t, d), F32),
        compiler_params=_cparams(("parallel",)),
        name="out_proj",
    )(ya, yb, w_bf16, x2)


def _scan_pitch(seq):
    seg = -(-seq // SCAN_SEGS)
    return seg + (4 - seg) % SUBLANES


def _lru_body(x_ref, g_ref, cw_ref, cb_ref, wa_ref, ba_ref, wx_ref, bx_ref, lam_ref,
              o_ref, x_s, a_s, b_s, *, seq, pitch, nblk):
    n_slab = nblk * LRU_BLOCK_W // LANES
    slab_per_blk = LRU_BLOCK_W // LANES
    pad = SCAN_SEGS * pitch - seq

    for sl in range(n_slab):
        x_s[sl, 0:SUBLANES, :] = jnp.zeros((SUBLANES, LANES), F32)
        x_s[sl, SUBLANES:, :] = x_ref[:, sl * LANES:(sl + 1) * LANES].astype(F32)

    for blk in range(nblk):
        parts = []
        for half in range(slab_per_blk):
            sl = blk * slab_per_blk + half
            cols = slice(sl * LANES, (sl + 1) * LANES)
            xc = cb_ref[:, cols] + cw_ref[CONV_W - 1:CONV_W, cols] * x_s[sl, SUBLANES:, :]
            for d in range(1, CONV_W):
                xc = xc + (cw_ref[CONV_W - 1 - d:CONV_W - d, cols]
                           * x_s[sl, SUBLANES - d:SUBLANES - d + seq, :])
            parts.append(xc)
        xc = jnp.concatenate(parts, axis=1)
        bcols = slice(blk * LRU_BLOCK_W, (blk + 1) * LRU_BLOCK_W)

        xb = xc.astype(BF16)
        t_r = jnp.tanh(jnp.dot(xb, wa_ref[blk], preferred_element_type=F32) + ba_ref[blk])
        t_i = jnp.tanh(jnp.dot(xb, wx_ref[blk], preferred_element_type=F32) + bx_ref[blk])
        gi = t_i + 1.0
        lam = lam_ref[:, bcols]
        softplus_neg = jnp.maximum(-lam, 0.0) + jnp.log1p(jnp.exp(-jnp.abs(lam)))
        half_rate = (-0.5 * LRU_C * LOG2E) * softplus_neg
        a = jnp.exp2(t_r * half_rate + half_rate)
        y = 1.0 - a * a
        mult = jnp.where(y > 0.0, y * lax.rsqrt(y), 0.0)
        bb = mult * (gi * xc)
        for half in range(slab_per_blk):
            sl = blk * slab_per_blk + half
            cols = slice(half * LANES, (half + 1) * LANES)
            a_s[sl, 0:seq, :] = a[:, cols]
            b_s[sl, 0:seq, :] = bb[:, cols]
            a_s[sl, seq:, :] = jnp.zeros((pad, LANES), F32)
            b_s[sl, seq:, :] = jnp.zeros((pad, LANES), F32)

    def pass1(j, carry):
        new = []
        for sl in range(n_slab):
            hh, pp = carry[2 * sl], carry[2 * sl + 1]
            idx = pl.ds(j, SCAN_SEGS, stride=pitch)
            aj = a_s[sl, idx, :]
            bj = b_s[sl, idx, :]
            hh = aj * hh + bj
            pp = aj * pp
            b_s[sl, idx, :] = hh
            a_s[sl, idx, :] = pp
            new += [hh, pp]
        return tuple(new)

    init = []
    for sl in range(n_slab):
        init += [jnp.zeros((SCAN_SEGS, LANES), F32), jnp.ones((SCAN_SEGS, LANES), F32)]
    ends = lax.fori_loop(0, pitch, pass1, tuple(init), unroll=2)

    seg_idx = lax.broadcasted_iota(jnp.int32, (SCAN_SEGS, LANES), 0)
    carries = []
    for sl in range(n_slab):
        h_end, p_end = ends[2 * sl], ends[2 * sl + 1]
        c = jnp.zeros((SCAN_SEGS, LANES), F32)
        for sgm in range(1, SCAN_SEGS):
            c = jnp.where(seg_idx == sgm, pltpu.roll(h_end + p_end * c, 1, axis=0), c)
        carries.append(c)

    def pass2(j, _):
        for sl in range(n_slab):
            idx = pl.ds(j, SCAN_SEGS, stride=pitch)
            b_s[sl, idx, :] = b_s[sl, idx, :] + a_s[sl, idx, :] * carries[sl]
        return 0

    lax.fori_loop(0, pitch, pass2, 0, unroll=4)

    for sl in range(n_slab):
        cols = slice(sl * LANES, (sl + 1) * LANES)
        hg = 0.5 * g_ref[:, cols].astype(F32)
        silu = hg * jnp.tanh(hg) + hg
        o_ref[:, cols] = (b_s[sl, 0:seq, :] * silu).astype(BF16)


def _rg_lru(u3, conv_w, conv_b, w_a, b_a, w_x, b_x, lam, *, blocks_per_step=2):
    b, s, _ = u3.shape
    pitch = _scan_pitch(s)
    nblk = blocks_per_step
    width = nblk * LRU_BLOCK_W
    n_slab = width // LANES
    n_steps = LRU_BLOCKS // nblk
    body = functools.partial(_lru_body, seq=s, pitch=pitch, nblk=nblk)

    def vec_spec(rows):
        return pl.BlockSpec((rows, width), lambda bi, n: (0, n))

    def blk_spec(rows):
        return pl.BlockSpec((nblk, rows, LRU_BLOCK_W), lambda bi, n: (n, 0, 0))

    return pl.pallas_call(
        body,
        grid=(b, n_steps),
        in_specs=[
            pl.BlockSpec((None, s, width), lambda bi, n: (bi, 0, n)),
            pl.BlockSpec((None, s, width), lambda bi, n: (bi, 0, n_steps + n)),
            vec_spec(CONV_W), vec_spec(1),
            blk_spec(LRU_BLOCK_W), blk_spec(1), blk_spec(LRU_BLOCK_W), blk_spec(1),
            vec_spec(1),
        ],
        out_specs=pl.BlockSpec((None, s, width), lambda bi, n: (bi, 0, n)),
        out_shape=jax.ShapeDtypeStruct((b, s, LRU_WIDTH), BF16),
        scratch_shapes=[pltpu.VMEM((n_slab, s + SUBLANES, LANES), F32),
                        pltpu.VMEM((n_slab, SCAN_SEGS * pitch, LANES), F32),
                        pltpu.VMEM((n_slab, SCAN_SEGS * pitch, LANES), F32)],
        compiler_params=_cparams(("parallel", "arbitrary")),
        name="rg_lru",
    )(u3, u3, conv_w, conv_b, w_a, b_a, w_x, b_x, lam)


def _s5_body(u_ref, bm_ref, cm_ref, are_ref, aim_ref, dskip_ref, y_ref,
             su, ut, bre, bim, hre, him, *, tt, pitch, n_sub):
    nb = SCAN_SEGS
    n_q = SSM_WIDTH // LANES
    n_state = SSM_GROUPS * SSM_STATE
    half_w = SSM_WIDTH // 2
    half_s = n_state // 2

    @pl.when(pl.program_id(1) == 0)
    def _():
        hre[...] = jnp.zeros_like(hre)
        him[...] = jnp.zeros_like(him)

    for b in range(nb):
        ub = u_ref[b].astype(F32)
        for q in range(n_q):
            su[q, b * pitch:b * pitch + tt, :] = ub[:, q * LANES:(q + 1) * LANES]

    def to_time_major(t, _):
        rows = pl.ds(pl.multiple_of(t * nb, nb), nb)
        for q in range(n_q):
            ut[rows, q * LANES:(q + 1) * LANES] = su[q, pl.ds(t, nb, stride=pitch), :]
        return 0

    lax.fori_loop(0, tt, to_time_major, 0)

    sub_rows = (tt // n_sub) * nb
    n_slab = n_state // LANES
    group = 4

    def input_map(j):
        r = slice(j * sub_rows, (j + 1) * sub_rows)
        u_bf = ut[r, :].astype(BF16)
        for hf in range(2):
            uh = u_bf[:, hf * half_w:(hf + 1) * half_w]
            cols = slice(hf * half_s, (hf + 1) * half_s)
            bre[r, cols] = jnp.dot(uh, bm_ref[hf, :, 0:half_s], preferred_element_type=F32)
            bim[r, cols] = jnp.dot(uh, bm_ref[hf, :, half_s:], preferred_element_type=F32)

    def scan(j):
        for k0 in range(0, n_slab, group):
            sl = [slice((k0 + k) * LANES, (k0 + k + 1) * LANES) for k in range(group)]
            a_r = [jnp.broadcast_to(are_ref[:, c], (nb, LANES)) for c in sl]
            a_i = [jnp.broadcast_to(aim_ref[:, c], (nb, LANES)) for c in sl]
            if j == 0:
                x = [(hre[:, c], him[:, c]) for c in sl]
            else:
                prev = slice(j * sub_rows - nb, j * sub_rows)
                x = [(bre[prev, c], bim[prev, c]) for c in sl]
            for t in range(tt // n_sub):
                rows = slice(j * sub_rows + t * nb, j * sub_rows + (t + 1) * nb)
                for k in range(group):
                    x_r, x_i = x[k]
                    n_r = a_r[k] * x_r - a_i[k] * x_i + bre[rows, sl[k]]
                    n_i = a_r[k] * x_i + a_i[k] * x_r + bim[rows, sl[k]]
                    bre[rows, sl[k]] = n_r
                    bim[rows, sl[k]] = n_i
                    x[k] = (n_r, n_i)

    def output_map(j):
        r = slice(j * sub_rows, (j + 1) * sub_rows)
        for hf in range(2):
            cols = slice(hf * half_s, (hf + 1) * half_s)
            y = jnp.dot(bre[r, cols].astype(BF16), cm_ref[hf, 0:half_s, :], preferred_element_type=F32)
            y = y + jnp.dot(bim[r, cols].astype(BF16), cm_ref[hf, half_s:, :],
                            preferred_element_type=F32)
            oc = slice(hf * half_w, (hf + 1) * half_w)
            ut[r, oc] = y + dskip_ref[:, oc] * ut[r, oc]

    input_map(0)
    for j in range(n_sub):
        if j + 1 < n_sub:
            input_map(j + 1)
        scan(j)
        if j > 0:
            output_map(j - 1)
    last = slice(tt * nb - nb, tt * nb)
    hre[...] = bre[last, :]
    him[...] = bim[last, :]
    output_map(n_sub - 1)

    def to_batch_major(t, _):
        rows = pl.ds(pl.multiple_of(t * nb, nb), nb)
        for q in range(n_q):
            su[q, pl.ds(t, nb, stride=pitch), :] = ut[rows, q * LANES:(q + 1) * LANES]
        return 0

    lax.fori_loop(0, tt, to_batch_major, 0)
    for b in range(nb):
        for q in range(n_q):
            y_ref[b, :, q * LANES:(q + 1) * LANES] = su[q, b * pitch:b * pitch + tt, :]


def _s5_scan(u3, b_mat, c_mat, a_re, a_im, d_skip, *, tt=128, n_sub=2):
    b, s, _ = u3.shape
    nb = SCAN_SEGS
    pitch = tt + 4
    n_state = SSM_GROUPS * SSM_STATE
    din_blk = 2 * LRU_WIDTH // SSM_WIDTH
    body = functools.partial(_s5_body, tt=tt, pitch=pitch, n_sub=n_sub)
    full = lambda shape: pl.BlockSpec(shape, lambda bi, ti: (0,) * len(shape))
    return pl.pallas_call(
        body,
        grid=(b // nb, s // tt),
        in_specs=[
            pl.BlockSpec((nb, tt, SSM_WIDTH), lambda bi, ti: (bi, ti, din_blk)),
            full(b_mat.shape), full(c_mat.shape), full((1, n_state)), full((1, n_state)),
            full((1, SSM_WIDTH)),
        ],
        out_specs=pl.BlockSpec((nb, tt, SSM_WIDTH), lambda bi, ti: (bi, ti, 0)),
        out_shape=jax.ShapeDtypeStruct((b, s, SSM_WIDTH), F32),
        scratch_shapes=[
            pltpu.VMEM((SSM_WIDTH // LANES, nb * pitch, LANES), F32),
            pltpu.VMEM((tt * nb, SSM_WIDTH), F32),
            pltpu.VMEM((tt * nb, n_state), F32),
            pltpu.VMEM((tt * nb, n_state), F32),
            pltpu.VMEM((nb, n_state), F32),
            pltpu.VMEM((nb, n_state), F32),
        ],
        compiler_params=_cparams(("parallel", "arbitrary")),
        name="s5_scan",
    )(u3, b_mat, c_mat, a_re, a_im, d_skip)


def _s5_operators(a_re, a_im, b_re, b_im, c_re, c_im, log_dt):
    a_re, a_im = a_re.astype(F32), a_im.astype(F32)
    dt = jnp.exp(log_dt.astype(F32))[:, None]
    mag = jnp.exp(a_re * dt)
    ab_re, ab_im = mag * jnp.cos(a_im * dt), mag * jnp.sin(a_im * dt)
    den = a_re * a_re + a_im * a_im
    f_re = ((ab_re - 1.0) * a_re + ab_im * a_im) / den
    f_im = (ab_im * a_re - (ab_re - 1.0) * a_im) / den
    b_re, b_im = b_re.astype(F32), b_im.astype(F32)
    bb_re = f_re[..., None] * b_re - f_im[..., None] * b_im
    bb_im = f_re[..., None] * b_im + f_im[..., None] * b_re

    gh = SSM_GROUPS // 2
    eye = jnp.eye(gh, dtype=F32)

    def in_map(bb):
        t = bb.reshape(2, gh, SSM_STATE, SSM_GROUP)
        t = t.transpose(0, 1, 3, 2)[:, :, :, None, :] * eye[None, :, None, :, None]
        return t.reshape(2, gh * SSM_GROUP, gh * SSM_STATE)

    def out_map(cc):
        t = cc.reshape(2, gh, SSM_GROUP, SSM_STATE)
        t = t.transpose(0, 1, 3, 2)[:, :, :, None, :] * eye[None, :, None, :, None]
        return t.reshape(2, gh * SSM_STATE, gh * SSM_GROUP)

    b_mat = jnp.concatenate([in_map(bb_re), in_map(bb_im)], axis=2).astype(BF16)
    c_mat = jnp.concatenate([out_map(c_re.astype(F32)), out_map(-c_im.astype(F32))],
                            axis=1).astype(BF16)
    n_state = SSM_GROUPS * SSM_STATE
    return b_mat, c_mat, ab_re.reshape(1, n_state), ab_im.reshape(1, n_state)


def _rec_out_body(yc_ref, y5_ref, dg_ref, wglu_ref, bglu_ref, w_ref, x_ref, o_ref, *, wc):
    yd = y5_ref[...]
    inner = math.sqrt(2.0 / math.pi) * (yd + 0.044715 * (yd * yd * yd))
    yd = 0.5 * yd * (1.0 + jnp.tanh(inner))
    glu = jnp.dot(yd.astype(BF16), wglu_ref[...], preferred_element_type=F32) + bglu_ref[...]
    yd = yd * _sigmoid(glu)
    g = dg_ref[...].astype(F32)
    yd = (yd * (g * _sigmoid(g))).astype(BF16)
    acc = jnp.dot(yc_ref[...], w_ref[0:wc, :], preferred_element_type=F32)
    acc = acc + jnp.dot(yd, w_ref[wc:, :], preferred_element_type=F32)
    o_ref[...] = x_ref[...] + acc


def _rec_out(yc, y5, u2, w_glu, b_glu, w_bf16, x2, *, tm=512):
    t, d = x2.shape
    wc = yc.shape[1]
    gate_blk = 2 * LRU_WIDTH // SSM_WIDTH + 1
    return pl.pallas_call(
        functools.partial(_rec_out_body, wc=wc),
        grid=(t // tm,),
        in_specs=[
            pl.BlockSpec((tm, wc), lambda i: (i, 0)),
            pl.BlockSpec((tm, SSM_WIDTH), lambda i: (i, 0)),
            pl.BlockSpec((tm, SSM_WIDTH), lambda i: (i, gate_blk)),
            pl.BlockSpec((SSM_WIDTH, SSM_WIDTH), lambda i: (0, 0)),
            pl.BlockSpec((1, SSM_WIDTH), lambda i: (0, 0)),
            pl.BlockSpec((wc + SSM_WIDTH, d), lambda i: (0, 0)),
            pl.BlockSpec((tm, d), lambda i: (i, 0)),
        ],
        out_specs=pl.BlockSpec((tm, d), lambda i: (i, 0)),
        out_shape=jax.ShapeDtypeStruct((t, d), F32),
        compiler_params=_cparams(("parallel",)),
        name="rec_out",
    )(yc, y5, u2, w_glu, b_glu, w_bf16, x2)


def _attention_layer(x2, b, s, norm_g, w_in, q_g_a, k_g_a, rel_bias, q_g_b, k_g_b,
                     lq1, lk1, lq2, lk2, subln_g, w_out, layer_idx):
    scale = HEAD_DIM ** -0.5 * LOG2E
    ones = jnp.ones((1024,), F32)
    col_gain = jnp.concatenate([
        jnp.tile(q_g_a.astype(F32) * scale, A_HEADS), jnp.tile(k_g_a.astype(F32), A_HEADS), ones, ones,
        jnp.tile(q_g_b.astype(F32).reshape(-1) * scale, B_HEADS),
        jnp.tile(k_g_b.astype(F32).reshape(-1), B_HEADS), ones, ones])[None, :]
    tn = 2048
    u = _norm_proj(x2, (norm_g.astype(F32)[:, None] * w_in).astype(BF16), col_gain,
                   norm_tiles=(0, 2), tn=tn)
    u3 = u.reshape(b, s, ATTN_IN)
    bound_a = (BOUND_SLACK * HEAD_DIM * scale * jnp.max(jnp.abs(q_g_a.astype(F32)))
               * jnp.max(jnp.abs(k_g_a.astype(F32))))
    ya = _attn_a(u3, *_attn_a_bias(rel_bias, bound_a))
    lam_init = 0.8 - 0.6 * math.exp(-0.3 * layer_idx)
    row = lambda v: v.astype(F32)[None, :]
    qk_bound = (BOUND_SLACK * HEAD_DIM * scale * jnp.max(jnp.abs(q_g_b.astype(F32)))
                * jnp.max(jnp.abs(k_g_b.astype(F32)))).reshape(1, 1)
    yb = _attn_b(u3, row(lq1), row(lk1), row(lq2), row(lk2), row(subln_g), qk_bound, lam_init)
    return _out_proj(ya.reshape(b * s, A_WIDTH), yb.reshape(b * s, B_WIDTH), w_out.astype(BF16), x2)


def _recurrent_layer(x2, b, s, norm_g, w_in, conv_w, conv_b, w_a, b_a, w_x, b_x, lru_lam,
                     a_re, a_im, b_re, b_im, c_re, c_im, d_skip, log_dt, w_glu, b_glu, w_out):
    u = _norm_proj(x2, (norm_g.astype(F32)[:, None] * w_in).astype(BF16),
                   jnp.ones((1, REC_IN), F32), norm_tiles=(), tn=2048)
    u3 = u.reshape(b, s, REC_IN)
    yc = _rg_lru(u3, 0.5 * conv_w.astype(F32), 0.5 * conv_b.astype(F32)[None, :],
                 w_a.astype(BF16), 0.5 * b_a.astype(F32)[:, None, :],
                 w_x.astype(BF16), 0.5 * b_x.astype(F32)[:, None, :],
                 lru_lam.astype(F32)[None, :])

    b_mat, c_mat, ab_re, ab_im = _s5_operators(a_re, a_im, b_re, b_im, c_re, c_im, log_dt)
    y5 = _s5_scan(u3, b_mat, c_mat, ab_re, ab_im, d_skip.astype(F32)[None, :])
    return _rec_out(yc.reshape(b * s, LRU_WIDTH), y5.reshape(b * s, SSM_WIDTH), u,
                    w_glu.astype(BF16), b_glu.astype(F32)[None, :], w_out.astype(BF16), x2)


def kernel(x, attn_norm_g, attn_w_in, a_q_g, a_k_g, a_rel_bias, b_q_g, b_k_g, b_lam_q1, b_lam_k1,
           b_lam_q2, b_lam_k2, b_subln_g, attn_w_out, rec_norm_g, rec_w_in, lru_conv_w, lru_conv_b,
           lru_w_a, lru_b_a, lru_w_x, lru_b_x, lru_lambda, ssm_a_re, ssm_a_im, ssm_b_re, ssm_b_im,
           ssm_c_re, ssm_c_im, ssm_d, ssm_log_dt, ssm_w_glu, ssm_b_glu, rec_w_out):
    b, s, d = x.shape
    depth = attn_norm_g.shape[0] + rec_norm_g.shape[0]
    x2 = x.reshape(b * s, d)
    for layer in range(depth):
        j = layer // 2
        if layer % 2 == 0:
            x2 = _attention_layer(x2, b, s, attn_norm_g[j], attn_w_in[j], a_q_g[j], a_k_g[j],
                                  a_rel_bias[j], b_q_g[j], b_k_g[j], b_lam_q1[j], b_lam_k1[j],
                                  b_lam_q2[j], b_lam_k2[j], b_subln_g[j], attn_w_out[j], layer)
        else:
            x2 = _recurrent_layer(x2, b, s, rec_norm_g[j], rec_w_in[j], lru_conv_w[j], lru_conv_b[j],
                                  lru_w_a[j], lru_b_a[j], lru_w_x[j], lru_b_x[j], lru_lambda[j],
                                  ssm_a_re[j], ssm_a_im[j], ssm_b_re[j], ssm_b_im[j],
                                  ssm_c_re[j], ssm_c_im[j], ssm_d[j], ssm_log_dt[j],
                                  ssm_w_glu[j], ssm_b_glu[j], rec_w_out[j])
    return x2.reshape(b, s, d)
```

```python
import functools
import math

import jax
import jax.numpy as jnp
from jax import lax
from jax.experimental import pallas as pl
from jax.experimental.pallas import tpu as pltpu

F32 = jnp.float32
BF16 = jnp.bfloat16

D_MODEL = 2048
CHUNK = 64
NEG_INF = -1e30
NORM_EPS = 1e-6

HEAD_DIM = 128
A_HEADS = 8
A_WIDTH = A_HEADS * HEAD_DIM
A_LEFT_CHUNKS = 8
A_MAX_REL = 128
B_HEADS = 4
B_V_DIM = 2 * HEAD_DIM
B_WIDTH = B_HEADS * B_V_DIM
ATTN_IN = 8 * 1024

LRU_BLOCKS = 6
LRU_BLOCK_W = 256
LRU_WIDTH = LRU_BLOCKS * LRU_BLOCK_W
CONV_W = 4
LRU_C = 8.0
SSM_GROUP = 16
SSM_GROUPS = 32
SSM_WIDTH = SSM_GROUPS * SSM_GROUP
SSM_STATE = 64
REC_IN = 2 * LRU_WIDTH + 2 * SSM_WIDTH

LANES = 128
SUBLANES = 8
MXU_N = 256
VMEM_LIMIT = 56 * 1024 * 1024

Q_BLK = 256
K_TILE = 256
LOG2E = 1.0 / math.log(2.0)
BOUND_SLACK = 1.0 + 2.0 ** -6
ONE_PASS_RANGE = 100.0
A_WIN = A_LEFT_CHUNKS * CHUNK + Q_BLK
SCAN_SEGS = SUBLANES


def _sigmoid(x):
    return 0.5 * jnp.tanh(0.5 * x) + 0.5


def _cparams(sem):
    return pltpu.CompilerParams(dimension_semantics=sem, vmem_limit_bytes=VMEM_LIMIT)


def _norm_proj_body(x_ref, w_ref, cg_ref, o_ref, h_ref, r_ref, *, norm_tiles, tn):
    j = pl.program_id(1)

    @pl.when(j == 0)
    def _():
        x = x_ref[...]
        ms = jnp.mean(x * x, axis=-1, keepdims=True)
        r_ref[...] = jnp.broadcast_to(lax.rsqrt(ms + NORM_EPS), r_ref.shape)
        h_ref[...] = x.astype(BF16)

    def step(head_norm):
        for n in range(tn // MXU_N):
            acc = jnp.dot(h_ref[...], w_ref[:, n * MXU_N:(n + 1) * MXU_N],
                          preferred_element_type=F32)
            for hh in range(MXU_N // HEAD_DIM):
                sl = slice(n * MXU_N + hh * HEAD_DIM, n * MXU_N + (hh + 1) * HEAD_DIM)
                blk = acc[:, hh * HEAD_DIM:(hh + 1) * HEAD_DIM] * r_ref[...]
                if head_norm:
                    ms = jnp.mean(blk * blk, axis=-1, keepdims=True)
                    blk = blk * lax.rsqrt(ms + NORM_EPS) * cg_ref[:, sl]
                o_ref[:, sl] = blk.astype(BF16)

    if norm_tiles:
        is_norm = functools.reduce(jnp.logical_or, [j == t for t in norm_tiles])
        pl.when(is_norm)(functools.partial(step, True))
        pl.when(jnp.logical_not(is_norm))(functools.partial(step, False))
    else:
        step(False)


def _norm_proj(x2, w_bf16, col_gain, norm_tiles, *, tm=1024, tn=1024):
    t, d = x2.shape
    n = w_bf16.shape[1]
    body = functools.partial(_norm_proj_body, norm_tiles=tuple(norm_tiles), tn=tn)
    return pl.pallas_call(
        body,
        grid=(t // tm, n // tn),
        in_specs=[
            pl.BlockSpec((tm, d), lambda i, j: (i, 0)),
            pl.BlockSpec((d, tn), lambda i, j: (0, j)),
            pl.BlockSpec((1, tn), lambda i, j: (0, j)),
        ],
        out_specs=pl.BlockSpec((tm, tn), lambda i, j: (i, j)),
        out_shape=jax.ShapeDtypeStruct((t, n), BF16),
        scratch_shapes=[pltpu.VMEM((tm, d), BF16), pltpu.VMEM((tm, LANES), F32)],
        compiler_params=_cparams(("parallel", "arbitrary")),
        name="norm_proj",
    )(x2, w_bf16, col_gain)


def _softmax_tiles(logits_fn, n_t, s_scr, p_scr, m_scr, l_scr):
    half = K_TILE // 2
    for t in range(n_t):
        s = logits_fn(t)
        s_scr[:, t * K_TILE:(t + 1) * K_TILE] = s
        hm = jnp.maximum(s[:, 0:half], s[:, half:])
        m_scr[...] = hm if t == 0 else jnp.maximum(m_scr[...], hm)
    m = jnp.max(m_scr[...], axis=-1, keepdims=True)
    m_scr[...] = jnp.broadcast_to(m, m_scr.shape)
    for t in range(n_t):
        lo = slice(t * K_TILE, t * K_TILE + half)
        hi = slice(t * K_TILE + half, (t + 1) * K_TILE)
        p_lo = jnp.exp2(s_scr[:, lo] - m_scr[...])
        p_hi = jnp.exp2(s_scr[:, hi] - m_scr[...])
        if l_scr is not None:
            l_scr[...] = (p_lo + p_hi) if t == 0 else l_scr[...] + (p_lo + p_hi)
        p_scr[:, lo] = p_lo.astype(BF16)
        p_scr[:, hi] = p_hi.astype(BF16)
    if l_scr is None:
        return None
    return jnp.sum(l_scr[...], axis=-1, keepdims=True)


def _attn_a_body(q_ref, k_ref, v_ref, g_ref, bias_ref, span_ref, o_ref, s_scr, p_scr, m_scr, v1_scr, *,
                 heads, seq):
    one_pass_ok = span_ref[0, 0] < ONE_PASS_RANGE
    nt_dims = (((1,), (1,)), ((), ()))

    def run(one_pass):
        for hh in range(heads):
            cols = slice(hh * HEAD_DIM, (hh + 1) * HEAD_DIM)
            v1_scr[hh, :, 0:HEAD_DIM] = v_ref[:, cols]
            v1_scr[hh, :, HEAD_DIM:] = jnp.ones((seq, HEAD_DIM), BF16)
            for i in range(seq // Q_BLK):
                rows = slice(i * Q_BLK, (i + 1) * Q_BLK)
                ks = max(0, i * Q_BLK - A_LEFT_CHUNKS * CHUNK)
                win = (i + 1) * Q_BLK - ks
                q = q_ref[rows, cols]

                def logits(t, q=q, ks=ks, win=win, hh=hh, cols=cols):
                    kt = k_ref[ks + t * K_TILE:ks + (t + 1) * K_TILE, cols]
                    b0 = A_WIN - win + t * K_TILE
                    return (lax.dot_general(q, kt, nt_dims, preferred_element_type=F32)
                            + bias_ref[hh, :, b0:b0 + K_TILE])

                if one_pass:
                    for t in range(win // K_TILE):
                        p_scr[:, t * K_TILE:(t + 1) * K_TILE] = jnp.exp2(logits(t)).astype(BF16)
                else:
                    _softmax_tiles(logits, win // K_TILE, s_scr, p_scr, m_scr, None)
                ol = jnp.dot(p_scr[:, 0:win], v1_scr[hh, ks:ks + win, :], preferred_element_type=F32)
                o = ol[:, 0:HEAD_DIM] / ol[:, HEAD_DIM:HEAD_DIM + 1]
                g = g_ref[rows, cols].astype(F32)
                o_ref[rows, cols] = (o * (g * _sigmoid(g))).astype(BF16)

    pl.when(one_pass_ok)(functools.partial(run, True))
    pl.when(jnp.logical_not(one_pass_ok))(functools.partial(run, False))


def _attn_a(u3, bias, span, *, heads_per_step=2):
    b, s, _ = u3.shape
    w = heads_per_step * HEAD_DIM
    per = A_WIDTH // w
    body = functools.partial(_attn_a_body, heads=heads_per_step, seq=s)

    def col_spec(section):
        return pl.BlockSpec((None, s, w), lambda bi, hg: (bi, 0, section * per + hg))

    return pl.pallas_call(
        body,
        grid=(b, per),
        in_specs=[
            col_spec(0), col_spec(1), col_spec(2), col_spec(3),
            pl.BlockSpec((heads_per_step, Q_BLK, A_WIN), lambda bi, hg: (hg, 0, 0)),
            pl.BlockSpec((1, 1), lambda bi, hg: (0, 0)),
        ],
        out_specs=pl.BlockSpec((None, s, w), lambda bi, hg: (bi, 0, hg)),
        out_shape=jax.ShapeDtypeStruct((b, s, A_WIDTH), BF16),
        scratch_shapes=[pltpu.VMEM((Q_BLK, A_WIN), F32), pltpu.VMEM((Q_BLK, A_WIN), BF16),
                        pltpu.VMEM((Q_BLK, K_TILE // 2), F32),
                        pltpu.VMEM((heads_per_step, s, 2 * HEAD_DIM), BF16)],
        compiler_params=_cparams(("parallel", "arbitrary")),
        name="attn_a",
    )(u3, u3, u3, u3, bias, span)


def _attn_a_bias(rel_bias, qk_bound):
    h = rel_bias.shape[0]
    rb = rel_bias.astype(F32)
    n_v = Q_BLK + A_WIN - 1
    lo = A_LEFT_CHUNKS * CHUNK + Q_BLK - 1 - A_MAX_REL
    v = jnp.concatenate([jnp.broadcast_to(rb[:, :1], (h, lo + 1)), rb[:, 1:2 * A_MAX_REL],
                         jnp.broadcast_to(rb[:, -1:], (h, n_v - lo - 2 * A_MAX_REL))], axis=1)
    v = jnp.roll(v, -(Q_BLK - 1), axis=1)
    flat = jnp.tile(v, (1, Q_BLK))[:, :Q_BLK * (n_v - 1)]
    toep = flat.reshape(h, Q_BLK, n_v - 1)[:, :, :A_WIN]
    qc = jnp.arange(Q_BLK)[:, None] // CHUNK
    kc = jnp.floor_divide(jnp.arange(A_WIN)[None, :] - A_LEFT_CHUNKS * CHUNK, CHUNK)
    allowed = (kc <= qc) & (kc >= qc - A_LEFT_CHUNKS)
    rb_max = jnp.max(rb, axis=1)
    shift = (qk_bound + LOG2E * rb_max)[:, None, None]
    span = jnp.max(2.0 * qk_bound + LOG2E * (rb_max - rb[:, A_MAX_REL])).reshape(1, 1)
    return jnp.where(allowed[None], toep * LOG2E - shift, NEG_INF), span


def _attn_b_body(q_ref, k_ref, v_ref, g_ref, lq1_ref, lk1_ref, lq2_ref, lk2_ref, sub_ref, mb_ref,
                 o_ref, bias_ref, s_scr, p_scr, p1_scr, m_scr, l_scr, qa_scr, ka_scr, *, seq, lam_init):
    h = pl.program_id(0)
    n_blk = seq // Q_BLK
    half = K_TILE // 2
    chunk_bits = CHUNK.bit_length() - 1
    nt_dims = (((1,), (1,)), ((), ()))
    slope2 = LOG2E * jnp.exp2((-8.0 / B_HEADS) * jnp.full((1, 1), h + 1, jnp.int32).astype(F32))

    lam = (jnp.exp(jnp.sum(lq1_ref[...] * lk1_ref[...], axis=-1, keepdims=True))
           - jnp.exp(jnp.sum(lq2_ref[...] * lk2_ref[...], axis=-1, keepdims=True)) + lam_init)

    m_bound = mb_ref[...]

    @pl.when(pl.program_id(1) == 0)
    def _():
        pos = slope2 * lax.broadcasted_iota(jnp.int32, (seq, 1), 0).astype(F32)
        p_hi = pos.astype(BF16).astype(F32)
        p_mid = (pos - p_hi).astype(BF16).astype(F32)
        p_lo = pos - p_hi - p_mid
        lane = lax.broadcasted_iota(jnp.int32, (seq, HEAD_DIM), 1)
        aug_q = jnp.where(lane == 0, -m_bound, jnp.where(lane == 1, -p_hi, jnp.where(
            lane == 2, -p_mid, jnp.where(lane == 3, -p_lo, jnp.where(lane < 7, 1.0, 0.0)))))
        aug_k = jnp.where(lane < 4, 1.0, jnp.where(lane == 4, p_hi, jnp.where(
            lane == 5, p_mid, jnp.where(lane == 6, p_lo, 0.0))))
        for comp in range(2):
            qa_scr[comp, :, HEAD_DIM:] = aug_q.astype(BF16)
            ka_scr[comp, :, HEAD_DIM:] = aug_k.astype(BF16)

    for comp in range(2):
        cols = slice(comp * HEAD_DIM, (comp + 1) * HEAD_DIM)
        qa_scr[comp, :, 0:HEAD_DIM] = q_ref[:, cols]
        ka_scr[comp, :, 0:HEAD_DIM] = k_ref[:, cols]
    one_pass_ok = 2.0 * m_bound[0, 0] < ONE_PASS_RANGE

    def finish_block(i, o):
        rows = slice(i * Q_BLK, (i + 1) * Q_BLK)
        ms = jnp.mean(o * o, axis=-1, keepdims=True)
        o = o * lax.rsqrt(ms + NORM_EPS) * sub_ref[...] * (1.0 - lam_init)
        g = g_ref[rows, :].astype(F32)
        o_ref[rows, :] = (o * (g * _sigmoid(g))).astype(BF16)

    def weighted_values(comp, win, l):
        pv = jnp.dot(p_scr[:, 0:win], v_ref[0:win, :], preferred_element_type=F32)
        return pv * ((1.0 if comp == 0 else lam) / l)

    def one_pass():
        r = lax.broadcasted_iota(jnp.int32, (Q_BLK, K_TILE), 0)
        c = lax.broadcasted_iota(jnp.int32, (Q_BLK, K_TILE), 1)
        ahead = jnp.maximum(c - r, 0).astype(F32)
        allowed = lax.shift_right_arithmetic(c, chunk_bits) <= lax.shift_right_arithmetic(r, chunk_bits)
        s_scr[:, 0:K_TILE] = jnp.where(allowed, -2.0 * slope2 * ahead, NEG_INF)
        for i in range(n_blk):
            rows = slice(i * Q_BLK, (i + 1) * Q_BLK)
            n_t = (i + 1) * Q_BLK // K_TILE
            win = n_t * K_TILE
            sums = []
            for comp, e_scr in enumerate((p_scr, p1_scr)):
                qa = qa_scr[comp, rows, :]
                for t in range(n_t):
                    z = lax.dot_general(qa, ka_scr[comp, t * K_TILE:(t + 1) * K_TILE, :], nt_dims,
                                        preferred_element_type=F32)
                    if t == n_t - 1:
                        z = z + s_scr[:, 0:K_TILE]
                    e_lo = jnp.exp2(z[:, 0:half])
                    e_hi = jnp.exp2(z[:, half:])
                    l_scr[...] = (e_lo + e_hi) if t == 0 else l_scr[...] + (e_lo + e_hi)
                    e_scr[:, t * K_TILE:t * K_TILE + half] = e_lo.astype(BF16)
                    e_scr[:, t * K_TILE + half:(t + 1) * K_TILE] = e_hi.astype(BF16)
                sums.append(jnp.sum(l_scr[...], axis=-1, keepdims=True))
            ratio = (lam * sums[0] / sums[1]).astype(BF16)
            w = p_scr[:, 0:win] - ratio * p1_scr[:, 0:win]
            pv = jnp.dot(w, v_ref[0:win, :], preferred_element_type=F32)
            finish_block(i, pv / sums[0])

    def two_pass():
        diag0 = seq - Q_BLK
        r = lax.broadcasted_iota(jnp.int32, (Q_BLK, seq), 0)
        c = lax.broadcasted_iota(jnp.int32, (Q_BLK, seq), 1) - diag0
        dist = jnp.abs(r - c).astype(F32)
        allowed = jnp.logical_or(c < 0, lax.shift_right_arithmetic(c, chunk_bits)
                                 <= lax.shift_right_arithmetic(r, chunk_bits))
        bias_ref[...] = jnp.where(allowed, -slope2 * dist, NEG_INF)
        for i in range(n_blk):
            rows = slice(i * Q_BLK, (i + 1) * Q_BLK)
            n_t = (i + 1) * Q_BLK // K_TILE
            win = n_t * K_TILE
            outs = []
            for comp in range(2):
                cols = slice(comp * HEAD_DIM, (comp + 1) * HEAD_DIM)
                q = q_ref[rows, cols]

                def logits(t, q=q, win=win, cols=cols):
                    b0 = seq - win + t * K_TILE
                    return (lax.dot_general(q, k_ref[t * K_TILE:(t + 1) * K_TILE, cols], nt_dims,
                                            preferred_element_type=F32)
                            + bias_ref[:, b0:b0 + K_TILE])

                l = _softmax_tiles(logits, n_t, s_scr, p_scr, m_scr, l_scr)
                outs.append(weighted_values(comp, win, l))
            finish_block(i, outs[0] - outs[1])

    pl.when(one_pass_ok)(one_pass)
    pl.when(jnp.logical_not(one_pass_ok))(two_pass)


def _attn_b(u3, lq1, lk1, lq2, lk2, subln_g, qk_bound, lam_init):
    b, s, _ = u3.shape
    per = B_WIDTH // B_V_DIM
    body = functools.partial(_attn_b_body, seq=s, lam_init=lam_init)

    def col_spec(section):
        return pl.BlockSpec((None, s, B_V_DIM), lambda h, bi: (bi, 0, section * per + h))

    def vec_spec(n):
        return pl.BlockSpec((1, n), lambda h, bi: (0, 0))

    return pl.pallas_call(
        body,
        grid=(B_HEADS, b),
        in_specs=[col_spec(4), col_spec(5), col_spec(6), col_spec(7),
                  vec_spec(HEAD_DIM), vec_spec(HEAD_DIM), vec_spec(HEAD_DIM), vec_spec(HEAD_DIM),
                  vec_spec(B_V_DIM), vec_spec(1)],
        out_specs=pl.BlockSpec((None, s, B_V_DIM), lambda h, bi: (bi, 0, h)),
        out_shape=jax.ShapeDtypeStruct((b, s, B_WIDTH), BF16),
        scratch_shapes=[pltpu.VMEM((Q_BLK, s), F32), pltpu.VMEM((Q_BLK, s), F32),
                        pltpu.VMEM((Q_BLK, s), BF16), pltpu.VMEM((Q_BLK, s), BF16),
                        pltpu.VMEM((Q_BLK, K_TILE // 2), F32), pltpu.VMEM((Q_BLK, K_TILE // 2), F32),
                        pltpu.VMEM((2, s, 2 * HEAD_DIM), BF16), pltpu.VMEM((2, s, 2 * HEAD_DIM), BF16)],
        compiler_params=_cparams(("parallel", "arbitrary")),
        name="attn_b",
    )(u3, u3, u3, u3, lq1, lk1, lq2, lk2, subln_g, qk_bound)


def _out_proj_body(ya_ref, yb_ref, w_ref, x_ref, o_ref, *, wa):
    acc = jnp.dot(ya_ref[...], w_ref[0:wa, :], preferred_element_type=F32)
    acc = acc + jnp.dot(yb_ref[...], w_ref[wa:, :], preferred_element_type=F32)
    o_ref[...] = x_ref[...] + acc


def _out_proj(ya, yb, w_bf16, x2, *, tm=512):
    t, d = x2.shape
    wa, wb = ya.shape[1], yb.shape[1]
    return pl.pallas_call(
        functools.partial(_out_proj_body, wa=wa),
        grid=(t // tm,),
        in_specs=[
            pl.BlockSpec((tm, wa), lambda i: (i, 0)),
            pl.BlockSpec((tm, wb), lambda i: (i, 0)),
            pl.BlockSpec((wa + wb, d), lambda i: (0, 0)),
            pl.BlockSpec((tm, d), lambda i: (i, 0)),
        ],
        out_specs=pl.BlockSpec((tm, d), lambda i: (i, 0)),
        out_shape=jax.ShapeDtypeStruct((t, d), F32),
        compiler_params=_cparams(("parallel",)),
        name="out_proj",
    )(ya, yb, w_bf16, x2)


def _scan_pitch(seq):
    seg = -(-seq // SCAN_SEGS)
    return seg + (4 - seg) % SUBLANES


def _lru_body(x_ref, g_ref, cw_ref, cb_ref, wa_ref, ba_ref, wx_ref, bx_ref, lam_ref,
              o_ref, x_s, a_s, b_s, *, seq, pitch, nblk):
    n_slab = nblk * LRU_BLOCK_W // LANES
    slab_per_blk = LRU_BLOCK_W // LANES
    pad = SCAN_SEGS * pitch - seq

    for sl in range(n_slab):
        x_s[sl, 0:SUBLANES, :] = jnp.zeros((SUBLANES, LANES), F32)
        x_s[sl, SUBLANES:, :] = x_ref[:, sl * LANES:(sl + 1) * LANES].astype(F32)

    for blk in range(nblk):
        parts = []
        for half in range(slab_per_blk):
            sl = blk * slab_per_blk + half
            cols = slice(sl * LANES, (sl + 1) * LANES)
            xc = cb_ref[:, cols] + cw_ref[CONV_W - 1:CONV_W, cols] * x_s[sl, SUBLANES:, :]
            for d in range(1, CONV_W):
                xc = xc + (cw_ref[CONV_W - 1 - d:CONV_W - d, cols]
                           * x_s[sl, SUBLANES - d:SUBLANES - d + seq, :])
            parts.append(xc)
        xc = jnp.concatenate(parts, axis=1)
        bcols = slice(blk * LRU_BLOCK_W, (blk + 1) * LRU_BLOCK_W)

        xb = xc.astype(BF16)
        t_r = jnp.tanh(jnp.dot(xb, wa_ref[blk], preferred_element_type=F32) + ba_ref[blk])
        t_i = jnp.tanh(jnp.dot(xb, wx_ref[blk], preferred_element_type=F32) + bx_ref[blk])
        gi = t_i + 1.0
        lam = lam_ref[:, bcols]
        softplus_neg = jnp.maximum(-lam, 0.0) + jnp.log1p(jnp.exp(-jnp.abs(lam)))
        half_rate = (-0.5 * LRU_C * LOG2E) * softplus_neg
        a = jnp.exp2(t_r * half_rate + half_rate)
        y = 1.0 - a * a
        mult = jnp.where(y > 0.0, y * lax.rsqrt(y), 0.0)
        bb = mult * (gi * xc)
        for half in range(slab_per_blk):
            sl = blk * slab_per_blk + half
            cols = slice(half * LANES, (half + 1) * LANES)
            a_s[sl, 0:seq, :] = a[:, cols]
            b_s[sl, 0:seq, :] = bb[:, cols]
            a_s[sl, seq:, :] = jnp.zeros((pad, LANES), F32)
            b_s[sl, seq:, :] = jnp.zeros((pad, LANES), F32)

    def pass1(j, carry):
        new = []
        for sl in range(n_slab):
            hh, pp = carry[2 * sl], carry[2 * sl + 1]
            idx = pl.ds(j, SCAN_SEGS, stride=pitch)
            aj = a_s[sl, idx, :]
            bj = b_s[sl, idx, :]
            hh = aj * hh + bj
            pp = aj * pp
            b_s[sl, idx, :] = hh
            a_s[sl, idx, :] = pp
            new += [hh, pp]
        return tuple(new)

    init = []
    for sl in range(n_slab):
        init += [jnp.zeros((SCAN_SEGS, LANES), F32), jnp.ones((SCAN_SEGS, LANES), F32)]
    ends = lax.fori_loop(0, pitch, pass1, tuple(init), unroll=2)

    seg_idx = lax.broadcasted_iota(jnp.int32, (SCAN_SEGS, LANES), 0)
    carries = []
    for sl in range(n_slab):
        h_end, p_end = ends[2 * sl], ends[2 * sl + 1]
        c = jnp.zeros((SCAN_SEGS, LANES), F32)
        for sgm in range(1, SCAN_SEGS):
            c = jnp.where(seg_idx == sgm, pltpu.roll(h_end + p_end * c, 1, axis=0), c)
        carries.append(c)

    def pass2(j, _):
        for sl in range(n_slab):
            idx = pl.ds(j, SCAN_SEGS, stride=pitch)
            b_s[sl, idx, :] = b_s[sl, idx, :] + a_s[sl, idx, :] * carries[sl]
        return 0

    lax.fori_loop(0, pitch, pass2, 0, unroll=4)

    for sl in range(n_slab):
        cols = slice(sl * LANES, (sl + 1) * LANES)
        hg = 0.5 * g_ref[:, cols].astype(F32)
        silu = hg * jnp.tanh(hg) + hg
        o_ref[:, cols] = (b_s[sl, 0:seq, :] * silu).astype(BF16)


def _rg_lru(u3, conv_w, conv_b, w_a, b_a, w_x, b_x, lam, *, blocks_per_step=2):
    b, s, _ = u3.shape
    pitch = _scan_pitch(s)
    nblk = blocks_per_step
    width = nblk * LRU_BLOCK_W
    n_slab = width // LANES
    n_steps = LRU_BLOCKS // nblk
    body = functools.partial(_lru_body, seq=s, pitch=pitch, nblk=nblk)

    def vec_spec(rows):
        return pl.BlockSpec((rows, width), lambda bi, n: (0, n))

    def blk_spec(rows):
        return pl.BlockSpec((nblk, rows, LRU_BLOCK_W), lambda bi, n: (n, 0, 0))

    return pl.pallas_call(
        body,
        grid=(b, n_steps),
        in_specs=[
            pl.BlockSpec((None, s, width), lambda bi, n: (bi, 0, n)),
            pl.BlockSpec((None, s, width), lambda bi, n: (bi, 0, n_steps + n)),
            vec_spec(CONV_W), vec_spec(1),
            blk_spec(LRU_BLOCK_W), blk_spec(1), blk_spec(LRU_BLOCK_W), blk_spec(1),
            vec_spec(1),
        ],
        out_specs=pl.BlockSpec((None, s, width), lambda bi, n: (bi, 0, n)),
        out_shape=jax.ShapeDtypeStruct((b, s, LRU_WIDTH), BF16),
        scratch_shapes=[pltpu.VMEM((n_slab, s + SUBLANES, LANES), F32),
                        pltpu.VMEM((n_slab, SCAN_SEGS * pitch, LANES), F32),
                        pltpu.VMEM((n_slab, SCAN_SEGS * pitch, LANES), F32)],
        compiler_params=_cparams(("parallel", "arbitrary")),
        name="rg_lru",
    )(u3, u3, conv_w, conv_b, w_a, b_a, w_x, b_x, lam)


def _s5_body(u_ref, bm_ref, cm_ref, are_ref, aim_ref, dskip_ref, y_ref,
             su, ut, bre, bim, hre, him, *, tt, pitch, n_sub):
    nb = SCAN_SEGS
    n_q = SSM_WIDTH // LANES
    n_state = SSM_GROUPS * SSM_STATE
    half_w = SSM_WIDTH // 2
    half_s = n_state // 2

    @pl.when(pl.program_id(1) == 0)
    def _():
        hre[...] = jnp.zeros_like(hre)
        him[...] = jnp.zeros_like(him)

    for b in range(nb):
        ub = u_ref[b].astype(F32)
        for q in range(n_q):
            su[q, b * pitch:b * pitch + tt, :] = ub[:, q * LANES:(q + 1) * LANES]

    def to_time_major(t, _):
        rows = pl.ds(pl.multiple_of(t * nb, nb), nb)
        for q in range(n_q):
            ut[rows, q * LANES:(q + 1) * LANES] = su[q, pl.ds(t, nb, stride=pitch), :]
        return 0

    lax.fori_loop(0, tt, to_time_major, 0)

    sub_rows = (tt // n_sub) * nb
    n_slab = n_state // LANES
    group = 4

    def input_map(j):
        r = slice(j * sub_rows, (j + 1) * sub_rows)
        u_bf = ut[r, :].astype(BF16)
        for hf in range(2):
            uh = u_bf[:, hf * half_w:(hf + 1) * half_w]
            cols = slice(hf * half_s, (hf + 1) * half_s)
            bre[r, cols] = jnp.dot(uh, bm_ref[hf, :, 0:half_s], preferred_element_type=F32)
            bim[r, cols] = jnp.dot(uh, bm_ref[hf, :, half_s:], preferred_element_type=F32)

    def scan(j):
        for k0 in range(0, n_slab, group):
            sl = [slice((k0 + k) * LANES, (k0 + k + 1) * LANES) for k in range(group)]
            a_r = [jnp.broadcast_to(are_ref[:, c], (nb, LANES)) for c in sl]
            a_i = [jnp.broadcast_to(aim_ref[:, c], (nb, LANES)) for c in sl]
            if j == 0:
                x = [(hre[:, c], him[:, c]) for c in sl]
            else:
                prev = slice(j * sub_rows - nb, j * sub_rows)
                x = [(bre[prev, c], bim[prev, c]) for c in sl]
            for t in range(tt // n_sub):
                rows = slice(j * sub_rows + t * nb, j * sub_rows + (t + 1) * nb)
                for k in range(group):
                    x_r, x_i = x[k]
                    n_r = a_r[k] * x_r - a_i[k] * x_i + bre[rows, sl[k]]
                    n_i = a_r[k] * x_i + a_i[k] * x_r + bim[rows, sl[k]]
                    bre[rows, sl[k]] = n_r
                    bim[rows, sl[k]] = n_i
                    x[k] = (n_r, n_i)

    def output_map(j):
        r = slice(j * sub_rows, (j + 1) * sub_rows)
        for hf in range(2):
            cols = slice(hf * half_s, (hf + 1) * half_s)
            y = jnp.dot(bre[r, cols].astype(BF16), cm_ref[hf, 0:half_s, :], preferred_element_type=F32)
            y = y + jnp.dot(bim[r, cols].astype(BF16), cm_ref[hf, half_s:, :],
                            preferred_element_type=F32)
            oc = slice(hf * half_w, (hf + 1) * half_w)
            ut[r, oc] = y + dskip_ref[:, oc] * ut[r, oc]

    input_map(0)
    for j in range(n_sub):
        if j + 1 < n_sub:
            input_map(j + 1)
        scan(j)
        if j > 0:
            output_map(j - 1)
    last = slice(tt * nb - nb, tt * nb)
    hre[...] = bre[last, :]
    him[...] = bim[last, :]
    output_map(n_sub - 1)

    def to_batch_major(t, _):
        rows = pl.ds(pl.multiple_of(t * nb, nb), nb)
        for q in range(n_q):
            su[q, pl.ds(t, nb, stride=pitch), :] = ut[rows, q * LANES:(q + 1) * LANES]
        return 0

    lax.fori_loop(0, tt, to_batch_major, 0)
    for b in range(nb):
        for q in range(n_q):
            y_ref[b, :, q * LANES:(q + 1) * LANES] = su[q, b * pitch:b * pitch + tt, :]


def _s5_scan(u3, b_mat, c_mat, a_re, a_im, d_skip, *, tt=128, n_sub=2):
    b, s, _ = u3.shape
    nb = SCAN_SEGS
    pitch = tt + 4
    n_state = SSM_GROUPS * SSM_STATE
    din_blk = 2 * LRU_WIDTH // SSM_WIDTH
    body = functools.partial(_s5_body, tt=tt, pitch=pitch, n_sub=n_sub)
    full = lambda shape: pl.BlockSpec(shape, lambda bi, ti: (0,) * len(shape))
    return pl.pallas_call(
        body,
        grid=(b // nb, s // tt),
        in_specs=[
            pl.BlockSpec((nb, tt, SSM_WIDTH), lambda bi, ti: (bi, ti, din_blk)),
            full(b_mat.shape), full(c_mat.shape), full((1, n_state)), full((1, n_state)),
            full((1, SSM_WIDTH)),
        ],
        out_specs=pl.BlockSpec((nb, tt, SSM_WIDTH), lambda bi, ti: (bi, ti, 0)),
        out_shape=jax.ShapeDtypeStruct((b, s, SSM_WIDTH), F32),
        scratch_shapes=[
            pltpu.VMEM((SSM_WIDTH // LANES, nb * pitch, LANES), F32),
            pltpu.VMEM((tt * nb, SSM_WIDTH), F32),
            pltpu.VMEM((tt * nb, n_state), F32),
            pltpu.VMEM((tt * nb, n_state), F32),
            pltpu.VMEM((nb, n_state), F32),
            pltpu.VMEM((nb, n_state), F32),
        ],
        compiler_params=_cparams(("parallel", "arbitrary")),
        name="s5_scan",
    )(u3, b_mat, c_mat, a_re, a_im, d_skip)


def _s5_operators(a_re, a_im, b_re, b_im, c_re, c_im, log_dt):
    a_re, a_im = a_re.astype(F32), a_im.astype(F32)
    dt = jnp.exp(log_dt.astype(F32))[:, None]
    mag = jnp.exp(a_re * dt)
    ab_re, ab_im = mag * jnp.cos(a_im * dt), mag * jnp.sin(a_im * dt)
    den = a_re * a_re + a_im * a_im
    f_re = ((ab_re - 1.0) * a_re + ab_im * a_im) / den
    f_im = (ab_im * a_re - (ab_re - 1.0) * a_im) / den
    b_re, b_im = b_re.astype(F32), b_im.astype(F32)
    bb_re = f_re[..., None] * b_re - f_im[..., None] * b_im
    bb_im = f_re[..., None] * b_im + f_im[..., None] * b_re

    gh = SSM_GROUPS // 2
    eye = jnp.eye(gh, dtype=F32)

    def in_map(bb):
        t = bb.reshape(2, gh, SSM_STATE, SSM_GROUP)
        t = t.transpose(0, 1, 3, 2)[:, :, :, None, :] * eye[None, :, None, :, None]
        return t.reshape(2, gh * SSM_GROUP, gh * SSM_STATE)

    def out_map(cc):
        t = cc.reshape(2, gh, SSM_GROUP, SSM_STATE)
        t = t.transpose(0, 1, 3, 2)[:, :, :, None, :] * eye[None, :, None, :, None]
        return t.reshape(2, gh * SSM_STATE, gh * SSM_GROUP)

    b_mat = jnp.concatenate([in_map(bb_re), in_map(bb_im)], axis=2).astype(BF16)
    c_mat = jnp.concatenate([out_map(c_re.astype(F32)), out_map(-c_im.astype(F32))],
                            axis=1).astype(BF16)
    n_state = SSM_GROUPS * SSM_STATE
    return b_mat, c_mat, ab_re.reshape(1, n_state), ab_im.reshape(1, n_state)


def _rec_out_body(yc_ref, y5_ref, dg_ref, wglu_ref, bglu_ref, w_ref, x_ref, o_ref, *, wc):
    yd = y5_ref[...]
    inner = math.sqrt(2.0 / math.pi) * (yd + 0.044715 * (yd * yd * yd))
    yd = 0.5 * yd * (1.0 + jnp.tanh(inner))
    glu = jnp.dot(yd.astype(BF16), wglu_ref[...], preferred_element_type=F32) + bglu_ref[...]
    yd = yd * _sigmoid(glu)
    g = dg_ref[...].astype(F32)
    yd = (yd * (g * _sigmoid(g))).astype(BF16)
    acc = jnp.dot(yc_ref[...], w_ref[0:wc, :], preferred_element_type=F32)
    acc = acc + jnp.dot(yd, w_ref[wc:, :], preferred_element_type=F32)
    o_ref[...] = x_ref[...] + acc


def _rec_out(yc, y5, u2, w_glu, b_glu, w_bf16, x2, *, tm=512):
    t, d = x2.shape
    wc = yc.shape[1]
    gate_blk = 2 * LRU_WIDTH // SSM_WIDTH + 1
    return pl.pallas_call(
        functools.partial(_rec_out_body, wc=wc),
        grid=(t // tm,),
        in_specs=[
            pl.BlockSpec((tm, wc), lambda i: (i, 0)),
            pl.BlockSpec((tm, SSM_WIDTH), lambda i: (i, 0)),
            pl.BlockSpec((tm, SSM_WIDTH), lambda i: (i, gate_blk)),
            pl.BlockSpec((SSM_WIDTH, SSM_WIDTH), lambda i: (0, 0)),
            pl.BlockSpec((1, SSM_WIDTH), lambda i: (0, 0)),
            pl.BlockSpec((wc + SSM_WIDTH, d), lambda i: (0, 0)),
            pl.BlockSpec((tm, d), lambda i: (i, 0)),
        ],
        out_specs=pl.BlockSpec((tm, d), lambda i: (i, 0)),
        out_shape=jax.ShapeDtypeStruct((t, d), F32),
        compiler_params=_cparams(("parallel",)),
        name="rec_out",
    )(yc, y5, u2, w_glu, b_glu, w_bf16, x2)


def _attention_layer(x2, b, s, norm_g, w_in, q_g_a, k_g_a, rel_bias, q_g_b, k_g_b,
                     lq1, lk1, lq2, lk2, subln_g, w_out, layer_idx):
    scale = HEAD_DIM ** -0.5 * LOG2E
    ones = jnp.ones((1024,), F32)
    col_gain = jnp.concatenate([
        jnp.tile(q_g_a.astype(F32) * scale, A_HEADS), jnp.tile(k_g_a.astype(F32), A_HEADS), ones, ones,
        jnp.tile(q_g_b.astype(F32).reshape(-1) * scale, B_HEADS),
        jnp.tile(k_g_b.astype(F32).reshape(-1), B_HEADS), ones, ones])[None, :]
    tn = 2048
    u = _norm_proj(x2, (norm_g.astype(F32)[:, None] * w_in).astype(BF16), col_gain,
                   norm_tiles=(0, 2), tn=tn)
    u3 = u.reshape(b, s, ATTN_IN)
    bound_a = (BOUND_SLACK * HEAD_DIM * scale * jnp.max(jnp.abs(q_g_a.astype(F32)))
               * jnp.max(jnp.abs(k_g_a.astype(F32))))
    ya = _attn_a(u3, *_attn_a_bias(rel_bias, bound_a))
    lam_init = 0.8 - 0.6 * math.exp(-0.3 * layer_idx)
    row = lambda v: v.astype(F32)[None, :]
    qk_bound = (BOUND_SLACK * HEAD_DIM * scale * jnp.max(jnp.abs(q_g_b.astype(F32)))
                * jnp.max(jnp.abs(k_g_b.astype(F32)))).reshape(1, 1)
    yb = _attn_b(u3, row(lq1), row(lk1), row(lq2), row(lk2), row(subln_g), qk_bound, lam_init)
    return _out_proj(ya.reshape(b * s, A_WIDTH), yb.reshape(b * s, B_WIDTH), w_out.astype(BF16), x2)


def _recurrent_layer(x2, b, s, norm_g, w_in, conv_w, conv_b, w_a, b_a, w_x, b_x, lru_lam,
                     a_re, a_im, b_re, b_im, c_re, c_im, d_skip, log_dt, w_glu, b_glu, w_out):
    u = _norm_proj(x2, (norm_g.astype(F32)[:, None] * w_in).astype(BF16),
                   jnp.ones((1, REC_IN), F32), norm_tiles=(), tn=2048)
    u3 = u.reshape(b, s, REC_IN)
    yc = _rg_lru(u3, 0.5 * conv_w.astype(F32), 0.5 * conv_b.astype(F32)[None, :],
                 w_a.astype(BF16), 0.5 * b_a.astype(F32)[:, None, :],
                 w_x.astype(BF16), 0.5 * b_x.astype(F32)[:, None, :],
                 lru_lam.astype(F32)[None, :])

    b_mat, c_mat, ab_re, ab_im = _s5_operators(a_re, a_im, b_re, b_im, c_re, c_im, log_dt)
    y5 = _s5_scan(u3, b_mat, c_mat, ab_re, ab_im, d_skip.astype(F32)[None, :])
    return _rec_out(yc.reshape(b * s, LRU_WIDTH), y5.reshape(b * s, SSM_WIDTH), u,
                    w_glu.astype(BF16), b_glu.astype(F32)[None, :], w_out.astype(BF16), x2)


def kernel(x, attn_norm_g, attn_w_in, a_q_g, a_k_g, a_rel_bias, b_q_g, b_k_g, b_lam_q1, b_lam_k1,
           b_lam_q2, b_lam_k2, b_subln_g, attn_w_out, rec_norm_g, rec_w_in, lru_conv_w, lru_conv_b,
           lru_w_a, lru_b_a, lru_w_x, lru_b_x, lru_lambda, ssm_a_re, ssm_a_im, ssm_b_re, ssm_b_im,
           ssm_c_re, ssm_c_im, ssm_d, ssm_log_dt, ssm_w_glu, ssm_b_glu, rec_w_out):
    b, s, d = x.shape
    depth = attn_norm_g.shape[0] + rec_norm_g.shape[0]
    x2 = x.reshape(b * s, d)
    for layer in range(depth):
        j = layer // 2
        if layer % 2 == 0:
            x2 = _attention_layer(x2, b, s, attn_norm_g[j], attn_w_in[j], a_q_g[j], a_k_g[j],
                                  a_rel_bias[j], b_q_g[j], b_k_g[j], b_lam_q1[j], b_lam_k1[j],
                                  b_lam_q2[j], b_lam_k2[j], b_subln_g[j], attn_w_out[j], layer)
        else:
            x2 = _recurrent_layer(x2, b, s, rec_norm_g[j], rec_w_in[j], lru_conv_w[j], lru_conv_b[j],
                                  lru_w_a[j], lru_b_a[j], lru_w_x[j], lru_b_x[j], lru_lambda[j],
                                  ssm_a_re[j], ssm_a_im[j], ssm_b_re[j], ssm_b_im[j],
                                  ssm_c_re[j], ssm_c_im[j], ssm_d[j], ssm_log_dt[j],
                                  ssm_w_glu[j], ssm_b_glu[j], rec_w_out[j])
    return x2.reshape(b, s, d)
```

```python
import functools
import math

import jax
import jax.numpy as jnp
from jax import lax
from jax.experimental import pallas as pl
from jax.experimental.pallas import tpu as pltpu

F32 = jnp.float32
BF16 = jnp.bfloat16

D_MODEL = 2048
CHUNK = 64
NEG_INF = -1e30
NORM_EPS = 1e-6

HEAD_DIM = 128
A_HEADS = 8
A_WIDTH = A_HEADS * HEAD_DIM
A_LEFT_CHUNKS = 8
A_MAX_REL = 128
B_HEADS = 4
B_V_DIM = 2 * HEAD_DIM
B_WIDTH = B_HEADS * B_V_DIM
ATTN_IN = 8 * 1024

LRU_BLOCKS = 6
LRU_BLOCK_W = 256
LRU_WIDTH = LRU_BLOCKS * LRU_BLOCK_W
CONV_W = 4
LRU_C = 8.0
SSM_GROUP = 16
SSM_GROUPS = 32
SSM_WIDTH = SSM_GROUPS * SSM_GROUP
SSM_STATE = 64
REC_IN = 2 * LRU_WIDTH + 2 * SSM_WIDTH

LANES = 128
SUBLANES = 8
MXU_N = 256
VMEM_LIMIT = 56 * 1024 * 1024

Q_BLK = 256
K_TILE = 256
LOG2E = 1.0 / math.log(2.0)
BOUND_SLACK = 1.0 + 2.0 ** -6
ONE_PASS_RANGE = 100.0
A_WIN = A_LEFT_CHUNKS * CHUNK + Q_BLK
SCAN_SEGS = SUBLANES


def _sigmoid(x):
    return 0.5 * jnp.tanh(0.5 * x) + 0.5


def _cparams(sem):
    return pltpu.CompilerParams(dimension_semantics=sem, vmem_limit_bytes=VMEM_LIMIT)


def _norm_proj_body(x_ref, w_ref, cg_ref, o_ref, h_ref, r_ref, *, norm_tiles, tn):
    j = pl.program_id(1)

    @pl.when(j == 0)
    def _():
        x = x_ref[...]
        ms = jnp.mean(x * x, axis=-1, keepdims=True)
        r_ref[...] = jnp.broadcast_to(lax.rsqrt(ms + NORM_EPS), r_ref.shape)
        h_ref[...] = x.astype(BF16)

    def step(head_norm):
        for n in range(tn // MXU_N):
            acc = jnp.dot(h_ref[...], w_ref[:, n * MXU_N:(n + 1) * MXU_N],
                          preferred_element_type=F32)
            for hh in range(MXU_N // HEAD_DIM):
                sl = slice(n * MXU_N + hh * HEAD_DIM, n * MXU_N + (hh + 1) * HEAD_DIM)
                blk = acc[:, hh * HEAD_DIM:(hh + 1) * HEAD_DIM] * r_ref[...]
                if head_norm:
                    ms = jnp.mean(blk * blk, axis=-1, keepdims=True)
                    blk = blk * lax.rsqrt(ms + NORM_EPS) * cg_ref[:, sl]
                o_ref[:, sl] = blk.astype(BF16)

    if norm_tiles:
        is_norm = functools.reduce(jnp.logical_or, [j == t for t in norm_tiles])
        pl.when(is_norm)(functools.partial(step, True))
        pl.when(jnp.logical_not(is_norm))(functools.partial(step, False))
    else:
        step(False)


def _norm_proj(x2, w_bf16, col_gain, norm_tiles, *, tm=1024, tn=1024):
    t, d = x2.shape
    n = w_bf16.shape[1]
    body = functools.partial(_norm_proj_body, norm_tiles=tuple(norm_tiles), tn=tn)
    return pl.pallas_call(
        body,
        grid=(t // tm, n // tn),
        in_specs=[
            pl.BlockSpec((tm, d), lambda i, j: (i, 0)),
            pl.BlockSpec((d, tn), lambda i, j: (0, j)),
            pl.BlockSpec((1, tn), lambda i, j: (0, j)),
        ],
        out_specs=pl.BlockSpec((tm, tn), lambda i, j: (i, j)),
        out_shape=jax.ShapeDtypeStruct((t, n), BF16),
        scratch_shapes=[pltpu.VMEM((tm, d), BF16), pltpu.VMEM((tm, LANES), F32)],
        compiler_params=_cparams(("parallel", "arbitrary")),
        name="norm_proj",
    )(x2, w_bf16, col_gain)


def _softmax_tiles(logits_fn, n_t, s_scr, p_scr, m_scr, l_scr):
    half = K_TILE // 2
    for t in range(n_t):
        s = logits_fn(t)
        s_scr[:, t * K_TILE:(t + 1) * K_TILE] = s
        hm = jnp.maximum(s[:, 0:half], s[:, half:])
        m_scr[...] = hm if t == 0 else jnp.maximum(m_scr[...], hm)
    m = jnp.max(m_scr[...], axis=-1, keepdims=True)
    m_scr[...] = jnp.broadcast_to(m, m_scr.shape)
    for t in range(n_t):
        lo = slice(t * K_TILE, t * K_TILE + half)
        hi = slice(t * K_TILE + half, (t + 1) * K_TILE)
        p_lo = jnp.exp2(s_scr[:, lo] - m_scr[...])
        p_hi = jnp.exp2(s_scr[:, hi] - m_scr[...])
        if l_scr is not None:
            l_scr[...] = (p_lo + p_hi) if t == 0 else l_scr[...] + (p_lo + p_hi)
        p_scr[:, lo] = p_lo.astype(BF16)
        p_scr[:, hi] = p_hi.astype(BF16)
    if l_scr is None:
        return None
    return jnp.sum(l_scr[...], axis=-1, keepdims=True)


def _attn_a_body(q_ref, k_ref, v_ref, g_ref, bias_ref, span_ref, o_ref, s_scr, p_scr, m_scr, v1_scr, *,
                 heads, seq):
    one_pass_ok = span_ref[0, 0] < ONE_PASS_RANGE
    nt_dims = (((1,), (1,)), ((), ()))

    def run(one_pass):
        for hh in range(heads):
            cols = slice(hh * HEAD_DIM, (hh + 1) * HEAD_DIM)
            v1_scr[hh, :, 0:HEAD_DIM] = v_ref[:, cols]
            v1_scr[hh, :, HEAD_DIM:] = jnp.ones((seq, HEAD_DIM), BF16)
            for i in range(seq // Q_BLK):
                rows = slice(i * Q_BLK, (i + 1) * Q_BLK)
                ks = max(0, i * Q_BLK - A_LEFT_CHUNKS * CHUNK)
                win = (i + 1) * Q_BLK - ks
                q = q_ref[rows, cols]

                def logits(t, q=q, ks=ks, win=win, hh=hh, cols=cols):
                    kt = k_ref[ks + t * K_TILE:ks + (t + 1) * K_TILE, cols]
                    b0 = A_WIN - win + t * K_TILE
                    return (lax.dot_general(q, kt, nt_dims, preferred_element_type=F32)
                            + bias_ref[hh, :, b0:b0 + K_TILE])

                if one_pass:
                    for t in range(win // K_TILE):
                        p_scr[:, t * K_TILE:(t + 1) * K_TILE] = jnp.exp2(logits(t)).astype(BF16)
                else:
                    _softmax_tiles(logits, win // K_TILE, s_scr, p_scr, m_scr, None)
                ol = jnp.dot(p_scr[:, 0:win], v1_scr[hh, ks:ks + win, :], preferred_element_type=F32)
                o = ol[:, 0:HEAD_DIM] / ol[:, HEAD_DIM:HEAD_DIM + 1]
                g = g_ref[rows, cols].astype(F32)
                o_ref[rows, cols] = (o * (g * _sigmoid(g))).astype(BF16)

    pl.when(one_pass_ok)(functools.partial(run, True))
    pl.when(jnp.logical_not(one_pass_ok))(functools.partial(run, False))


def _attn_a(u3, bias, span, *, heads_per_step=4):
    b, s, _ = u3.shape
    w = heads_per_step * HEAD_DIM
    per = A_WIDTH // w
    body = functools.partial(_attn_a_body, heads=heads_per_step, seq=s)

    def col_spec(section):
        return pl.BlockSpec((None, s, w), lambda bi, hg: (bi, 0, section * per + hg))

    return pl.pallas_call(
        body,
        grid=(b, per),
        in_specs=[
            col_spec(0), col_spec(1), col_spec(2), col_spec(3),
            pl.BlockSpec((heads_per_step, Q_BLK, A_WIN), lambda bi, hg: (hg, 0, 0)),
            pl.BlockSpec((1, 1), lambda bi, hg: (0, 0)),
        ],
        out_specs=pl.BlockSpec((None, s, w), lambda bi, hg: (bi, 0, hg)),
        out_shape=jax.ShapeDtypeStruct((b, s, A_WIDTH), BF16),
        scratch_shapes=[pltpu.VMEM((Q_BLK, A_WIN), F32), pltpu.VMEM((Q_BLK, A_WIN), BF16),
                        pltpu.VMEM((Q_BLK, K_TILE // 2), F32),
                        pltpu.VMEM((heads_per_step, s, 2 * HEAD_DIM), BF16)],
        compiler_params=_cparams(("parallel", "arbitrary")),
        name="attn_a",
    )(u3, u3, u3, u3, bias, span)


def _attn_a_bias(rel_bias, qk_bound):
    h = rel_bias.shape[0]
    rb = rel_bias.astype(F32)
    n_v = Q_BLK + A_WIN - 1
    lo = A_LEFT_CHUNKS * CHUNK + Q_BLK - 1 - A_MAX_REL
    v = jnp.concatenate([jnp.broadcast_to(rb[:, :1], (h, lo + 1)), rb[:, 1:2 * A_MAX_REL],
                         jnp.broadcast_to(rb[:, -1:], (h, n_v - lo - 2 * A_MAX_REL))], axis=1)
    v = jnp.roll(v, -(Q_BLK - 1), axis=1)
    flat = jnp.tile(v, (1, Q_BLK))[:, :Q_BLK * (n_v - 1)]
    toep = flat.reshape(h, Q_BLK, n_v - 1)[:, :, :A_WIN]
    qc = jnp.arange(Q_BLK)[:, None] // CHUNK
    kc = jnp.floor_divide(jnp.arange(A_WIN)[None, :] - A_LEFT_CHUNKS * CHUNK, CHUNK)
    allowed = (kc <= qc) & (kc >= qc - A_LEFT_CHUNKS)
    rb_max = jnp.max(rb, axis=1)
    shift = (qk_bound + LOG2E * rb_max)[:, None, None]
    span = jnp.max(2.0 * qk_bound + LOG2E * (rb_max - rb[:, A_MAX_REL])).reshape(1, 1)
    return jnp.where(allowed[None], toep * LOG2E - shift, NEG_INF), span


def _attn_b_body(q_ref, k_ref, v_ref, g_ref, lq1_ref, lk1_ref, lq2_ref, lk2_ref, sub_ref, mb_ref,
                 o_ref, bias_ref, s_scr, p_scr, m_scr, l_scr, qa_scr, ka_scr, *, seq, lam_init):
    h = pl.program_id(0)
    n_blk = seq // Q_BLK
    half = K_TILE // 2
    chunk_bits = CHUNK.bit_length() - 1
    nt_dims = (((1,), (1,)), ((), ()))
    slope2 = LOG2E * jnp.exp2((-8.0 / B_HEADS) * jnp.full((1, 1), h + 1, jnp.int32).astype(F32))

    lam = (jnp.exp(jnp.sum(lq1_ref[...] * lk1_ref[...], axis=-1, keepdims=True))
           - jnp.exp(jnp.sum(lq2_ref[...] * lk2_ref[...], axis=-1, keepdims=True)) + lam_init)

    m_bound = mb_ref[...]

    @pl.when(pl.program_id(1) == 0)
    def _():
        pos = slope2 * lax.broadcasted_iota(jnp.int32, (seq, 1), 0).astype(F32)
        p_hi = pos.astype(BF16).astype(F32)
        p_mid = (pos - p_hi).astype(BF16).astype(F32)
        p_lo = pos - p_hi - p_mid
        lane = lax.broadcasted_iota(jnp.int32, (seq, HEAD_DIM), 1)
        aug_q = jnp.where(lane == 0, -m_bound, jnp.where(lane == 1, -p_hi, jnp.where(
            lane == 2, -p_mid, jnp.where(lane == 3, -p_lo, jnp.where(lane < 7, 1.0, 0.0)))))
        aug_k = jnp.where(lane < 4, 1.0, jnp.where(lane == 4, p_hi, jnp.where(
            lane == 5, p_mid, jnp.where(lane == 6, p_lo, 0.0))))
        for comp in range(2):
            qa_scr[comp, :, HEAD_DIM:] = aug_q.astype(BF16)
            ka_scr[comp, :, HEAD_DIM:] = aug_k.astype(BF16)

    for comp in range(2):
        cols = slice(comp * HEAD_DIM, (comp + 1) * HEAD_DIM)
        qa_scr[comp, :, 0:HEAD_DIM] = q_ref[:, cols]
        ka_scr[comp, :, 0:HEAD_DIM] = k_ref[:, cols]
    one_pass_ok = 2.0 * m_bound[0, 0] < ONE_PASS_RANGE

    def finish_block(i, outs):
        rows = slice(i * Q_BLK, (i + 1) * Q_BLK)
        o = outs[0] - outs[1]
        ms = jnp.mean(o * o, axis=-1, keepdims=True)
        o = o * lax.rsqrt(ms + NORM_EPS) * sub_ref[...] * (1.0 - lam_init)
        g = g_ref[rows, :].astype(F32)
        o_ref[rows, :] = (o * (g * _sigmoid(g))).astype(BF16)

    def weighted_values(comp, win, l):
        pv = jnp.dot(p_scr[:, 0:win], v_ref[0:win, :], preferred_element_type=F32)
        return pv * ((1.0 if comp == 0 else lam) / l)

    def one_pass():
        r = lax.broadcasted_iota(jnp.int32, (Q_BLK, K_TILE), 0)
        c = lax.broadcasted_iota(jnp.int32, (Q_BLK, K_TILE), 1)
        ahead = jnp.maximum(c - r, 0).astype(F32)
        allowed = lax.shift_right_arithmetic(c, chunk_bits) <= lax.shift_right_arithmetic(r, chunk_bits)
        s_scr[:, 0:K_TILE] = jnp.where(allowed, -2.0 * slope2 * ahead, NEG_INF)
        for i in range(n_blk):
            rows = slice(i * Q_BLK, (i + 1) * Q_BLK)
            n_t = (i + 1) * Q_BLK // K_TILE
            outs = []
            for comp in range(2):
                qa = qa_scr[comp, rows, :]
                for t in range(n_t):
                    z = lax.dot_general(qa, ka_scr[comp, t * K_TILE:(t + 1) * K_TILE, :], nt_dims,
                                        preferred_element_type=F32)
                    if t == n_t - 1:
                        z = z + s_scr[:, 0:K_TILE]
                    e_lo = jnp.exp2(z[:, 0:half])
                    e_hi = jnp.exp2(z[:, half:])
                    l_scr[...] = (e_lo + e_hi) if t == 0 else l_scr[...] + (e_lo + e_hi)
                    p_scr[:, t * K_TILE:t * K_TILE + half] = e_lo.astype(BF16)
                    p_scr[:, t * K_TILE + half:(t + 1) * K_TILE] = e_hi.astype(BF16)
                l = jnp.sum(l_scr[...], axis=-1, keepdims=True)
                outs.append(weighted_values(comp, n_t * K_TILE, l))
            finish_block(i, outs)

    def two_pass():
        diag0 = seq - Q_BLK
        r = lax.broadcasted_iota(jnp.int32, (Q_BLK, seq), 0)
        c = lax.broadcasted_iota(jnp.int32, (Q_BLK, seq), 1) - diag0
        dist = jnp.abs(r - c).astype(F32)
        allowed = jnp.logical_or(c < 0, lax.shift_right_arithmetic(c, chunk_bits)
                                 <= lax.shift_right_arithmetic(r, chunk_bits))
        bias_ref[...] = jnp.where(allowed, -slope2 * dist, NEG_INF)
        for i in range(n_blk):
            rows = slice(i * Q_BLK, (i + 1) * Q_BLK)
            n_t = (i + 1) * Q_BLK // K_TILE
            win = n_t * K_TILE
            outs = []
            for comp in range(2):
                cols = slice(comp * HEAD_DIM, (comp + 1) * HEAD_DIM)
                q = q_ref[rows, cols]

                def logits(t, q=q, win=win, cols=cols):
                    b0 = seq - win + t * K_TILE
                    return (lax.dot_general(q, k_ref[t * K_TILE:(t + 1) * K_TILE, cols], nt_dims,
                                            preferred_element_type=F32)
                            + bias_ref[:, b0:b0 + K_TILE])

                l = _softmax_tiles(logits, n_t, s_scr, p_scr, m_scr, l_scr)
                outs.append(weighted_values(comp, win, l))
            finish_block(i, outs)

    pl.when(one_pass_ok)(one_pass)
    pl.when(jnp.logical_not(one_pass_ok))(two_pass)


def _attn_b(u3, lq1, lk1, lq2, lk2, subln_g, qk_bound, lam_init):
    b, s, _ = u3.shape
    per = B_WIDTH // B_V_DIM
    body = functools.partial(_attn_b_body, seq=s, lam_init=lam_init)

    def col_spec(section):
        return pl.BlockSpec((None, s, B_V_DIM), lambda h, bi: (bi, 0, section * per + h))

    def vec_spec(n):
        return pl.BlockSpec((1, n), lambda h, bi: (0, 0))

    return pl.pallas_call(
        body,
        grid=(B_HEADS, b),
        in_specs=[col_spec(4), col_spec(5), col_spec(6), col_spec(7),
                  vec_spec(HEAD_DIM), vec_spec(HEAD_DIM), vec_spec(HEAD_DIM), vec_spec(HEAD_DIM),
                  vec_spec(B_V_DIM), vec_spec(1)],
        out_specs=pl.BlockSpec((None, s, B_V_DIM), lambda h, bi: (bi, 0, h)),
        out_shape=jax.ShapeDtypeStruct((b, s, B_WIDTH), BF16),
        scratch_shapes=[pltpu.VMEM((Q_BLK, s), F32), pltpu.VMEM((Q_BLK, s), F32),
                        pltpu.VMEM((Q_BLK, s), BF16), pltpu.VMEM((Q_BLK, K_TILE // 2), F32),
                        pltpu.VMEM((Q_BLK, K_TILE // 2), F32),
                        pltpu.VMEM((2, s, 2 * HEAD_DIM), BF16), pltpu.VMEM((2, s, 2 * HEAD_DIM), BF16)],
        compiler_params=_cparams(("parallel", "arbitrary")),
        name="attn_b",
    )(u3, u3, u3, u3, lq1, lk1, lq2, lk2, subln_g, qk_bound)


def _proj_residual(ya, yb, w_ref, x_ref, o_ref, wa):
    for n in range(o_ref.shape[1] // MXU_N):
        cols = slice(n * MXU_N, (n + 1) * MXU_N)
        acc = jnp.dot(ya, w_ref[0:wa, cols], preferred_element_type=F32)
        acc = acc + jnp.dot(yb, w_ref[wa:, cols], preferred_element_type=F32)
        o_ref[:, cols] = x_ref[:, cols] + acc


def _out_proj_body(ya_ref, yb_ref, w_ref, x_ref, o_ref, *, wa):
    _proj_residual(ya_ref[...], yb_ref[...], w_ref, x_ref, o_ref, wa)


def _out_proj(ya, yb, w_bf16, x2, *, tm=512):
    t, d = x2.shape
    wa, wb = ya.shape[1], yb.shape[1]
    return pl.pallas_call(
        functools.partial(_out_proj_body, wa=wa),
        grid=(t // tm,),
        in_specs=[
            pl.BlockSpec((tm, wa), lambda i: (i, 0)),
            pl.BlockSpec((tm, wb), lambda i: (i, 0)),
            pl.BlockSpec((wa + wb, d), lambda i: (0, 0)),
            pl.BlockSpec((tm, d), lambda i: (i, 0)),
        ],
        out_specs=pl.BlockSpec((tm, d), lambda i: (i, 0)),
        out_shape=jax.ShapeDtypeStruct((t, d), F32),
        compiler_params=_cparams(("parallel",)),
        name="out_proj",
    )(ya, yb, w_bf16, x2)


def _scan_pitch(seq):
    seg = -(-seq // SCAN_SEGS)
    return seg + (4 - seg) % SUBLANES


def _lru_body(x_ref, g_ref, cw_ref, cb_ref, wa_ref, ba_ref, wx_ref, bx_ref, lam_ref,
              o_ref, x_s, a_s, b_s, *, seq, pitch, nblk):
    n_slab = nblk * LRU_BLOCK_W // LANES
    slab_per_blk = LRU_BLOCK_W // LANES
    pad = SCAN_SEGS * pitch - seq

    for sl in range(n_slab):
        x_s[sl, 0:SUBLANES, :] = jnp.zeros((SUBLANES, LANES), F32)
        x_s[sl, SUBLANES:, :] = x_ref[:, sl * LANES:(sl + 1) * LANES].astype(F32)

    for blk in range(nblk):
        parts = []
        for half in range(slab_per_blk):
            sl = blk * slab_per_blk + half
            cols = slice(sl * LANES, (sl + 1) * LANES)
            xc = cb_ref[:, cols] + cw_ref[CONV_W - 1:CONV_W, cols] * x_s[sl, SUBLANES:, :]
            for d in range(1, CONV_W):
                xc = xc + (cw_ref[CONV_W - 1 - d:CONV_W - d, cols]
                           * x_s[sl, SUBLANES - d:SUBLANES - d + seq, :])
            parts.append(xc)
        xc = jnp.concatenate(parts, axis=1)
        bcols = slice(blk * LRU_BLOCK_W, (blk + 1) * LRU_BLOCK_W)

        xb = xc.astype(BF16)
        t_r = jnp.tanh(jnp.dot(xb, wa_ref[blk], preferred_element_type=F32) + ba_ref[blk])
        t_i = jnp.tanh(jnp.dot(xb, wx_ref[blk], preferred_element_type=F32) + bx_ref[blk])
        gi = t_i + 1.0
        lam = lam_ref[:, bcols]
        softplus_neg = jnp.maximum(-lam, 0.0) + jnp.log1p(jnp.exp(-jnp.abs(lam)))
        half_rate = (-0.5 * LRU_C * LOG2E) * softplus_neg
        a = jnp.exp2(t_r * half_rate + half_rate)
        y = 1.0 - a * a
        mult = jnp.where(y > 0.0, y * lax.rsqrt(y), 0.0)
        bb = mult * (gi * xc)
        for half in range(slab_per_blk):
            sl = blk * slab_per_blk + half
            cols = slice(half * LANES, (half + 1) * LANES)
            a_s[sl, 0:seq, :] = a[:, cols]
            b_s[sl, 0:seq, :] = bb[:, cols]
            a_s[sl, seq:, :] = jnp.zeros((pad, LANES), F32)
            b_s[sl, seq:, :] = jnp.zeros((pad, LANES), F32)

    def pass1(j, carry):
        new = []
        for sl in range(n_slab):
            hh, pp = carry[2 * sl], carry[2 * sl + 1]
            idx = pl.ds(j, SCAN_SEGS, stride=pitch)
            aj = a_s[sl, idx, :]
            bj = b_s[sl, idx, :]
            hh = aj * hh + bj
            pp = aj * pp
            b_s[sl, idx, :] = hh
            a_s[sl, idx, :] = pp
            new += [hh, pp]
        return tuple(new)

    init = []
    for sl in range(n_slab):
        init += [jnp.zeros((SCAN_SEGS, LANES), F32), jnp.ones((SCAN_SEGS, LANES), F32)]
    ends = lax.fori_loop(0, pitch, pass1, tuple(init), unroll=2)

    seg_idx = lax.broadcasted_iota(jnp.int32, (SCAN_SEGS, LANES), 0)
    carries = []
    for sl in range(n_slab):
        h_end, p_end = ends[2 * sl], ends[2 * sl + 1]
        c = jnp.zeros((SCAN_SEGS, LANES), F32)
        for sgm in range(1, SCAN_SEGS):
            c = jnp.where(seg_idx == sgm, pltpu.roll(h_end + p_end * c, 1, axis=0), c)
        carries.append(c)

    def pass2(j, _):
        for sl in range(n_slab):
            idx = pl.ds(j, SCAN_SEGS, stride=pitch)
            b_s[sl, idx, :] = b_s[sl, idx, :] + a_s[sl, idx, :] * carries[sl]
        return 0

    lax.fori_loop(0, pitch, pass2, 0, unroll=4)

    for sl in range(n_slab):
        cols = slice(sl * LANES, (sl + 1) * LANES)
        hg = 0.5 * g_ref[:, cols].astype(F32)
        silu = hg * jnp.tanh(hg) + hg
        o_ref[:, cols] = (b_s[sl, 0:seq, :] * silu).astype(BF16)


def _rg_lru(u3, conv_w, conv_b, w_a, b_a, w_x, b_x, lam, *, blocks_per_step=2):
    b, s, _ = u3.shape
    pitch = _scan_pitch(s)
    nblk = blocks_per_step
    width = nblk * LRU_BLOCK_W
    n_slab = width // LANES
    n_steps = LRU_BLOCKS // nblk
    body = functools.partial(_lru_body, seq=s, pitch=pitch, nblk=nblk)

    def vec_spec(rows):
        return pl.BlockSpec((rows, width), lambda bi, n: (0, n))

    def blk_spec(rows):
        return pl.BlockSpec((nblk, rows, LRU_BLOCK_W), lambda bi, n: (n, 0, 0))

    return pl.pallas_call(
        body,
        grid=(b, n_steps),
        in_specs=[
            pl.BlockSpec((None, s, width), lambda bi, n: (bi, 0, n)),
            pl.BlockSpec((None, s, width), lambda bi, n: (bi, 0, n_steps + n)),
            vec_spec(CONV_W), vec_spec(1),
            blk_spec(LRU_BLOCK_W), blk_spec(1), blk_spec(LRU_BLOCK_W), blk_spec(1),
            vec_spec(1),
        ],
        out_specs=pl.BlockSpec((None, s, width), lambda bi, n: (bi, 0, n)),
        out_shape=jax.ShapeDtypeStruct((b, s, LRU_WIDTH), BF16),
        scratch_shapes=[pltpu.VMEM((n_slab, s + SUBLANES, LANES), F32),
                        pltpu.VMEM((n_slab, SCAN_SEGS * pitch, LANES), F32),
                        pltpu.VMEM((n_slab, SCAN_SEGS * pitch, LANES), F32)],
        compiler_params=_cparams(("parallel", "arbitrary")),
        name="rg_lru",
    )(u3, u3, conv_w, conv_b, w_a, b_a, w_x, b_x, lam)


def _s5_body(u_ref, bm_ref, cm_ref, are_ref, aim_ref, dskip_ref, y_ref,
             su, ut, bre, bim, hre, him, *, tt, pitch, n_sub):
    nb = SCAN_SEGS
    n_q = SSM_WIDTH // LANES
    n_state = SSM_GROUPS * SSM_STATE
    half_w = SSM_WIDTH // 2
    half_s = n_state // 2

    @pl.when(pl.program_id(1) == 0)
    def _():
        hre[...] = jnp.zeros_like(hre)
        him[...] = jnp.zeros_like(him)

    for b in range(nb):
        ub = u_ref[b].astype(F32)
        for q in range(n_q):
            su[q, b * pitch:b * pitch + tt, :] = ub[:, q * LANES:(q + 1) * LANES]

    def to_time_major(t, _):
        rows = pl.ds(pl.multiple_of(t * nb, nb), nb)
        for q in range(n_q):
            ut[rows, q * LANES:(q + 1) * LANES] = su[q, pl.ds(t, nb, stride=pitch), :]
        return 0

    lax.fori_loop(0, tt, to_time_major, 0)

    sub_rows = (tt // n_sub) * nb
    n_slab = n_state // LANES
    group = 4

    def input_map(j):
        r = slice(j * sub_rows, (j + 1) * sub_rows)
        u_bf = ut[r, :].astype(BF16)
        for hf in range(2):
            uh = u_bf[:, hf * half_w:(hf + 1) * half_w]
            cols = slice(hf * half_s, (hf + 1) * half_s)
            bre[r, cols] = jnp.dot(uh, bm_ref[hf, :, 0:half_s], preferred_element_type=F32)
            bim[r, cols] = jnp.dot(uh, bm_ref[hf, :, half_s:], preferred_element_type=F32)

    def scan(j):
        for k0 in range(0, n_slab, group):
            sl = [slice((k0 + k) * LANES, (k0 + k + 1) * LANES) for k in range(group)]
            a_r = [jnp.broadcast_to(are_ref[:, c], (nb, LANES)) for c in sl]
            a_i = [jnp.broadcast_to(aim_ref[:, c], (nb, LANES)) for c in sl]
            if j == 0:
                x = [(hre[:, c], him[:, c]) for c in sl]
            else:
                prev = slice(j * sub_rows - nb, j * sub_rows)
                x = [(bre[prev, c], bim[prev, c]) for c in sl]
            for t in range(tt // n_sub):
                rows = slice(j * sub_rows + t * nb, j * sub_rows + (t + 1) * nb)
                for k in range(group):
                    x_r, x_i = x[k]
                    n_r = a_r[k] * x_r - a_i[k] * x_i + bre[rows, sl[k]]
                    n_i = a_r[k] * x_i + a_i[k] * x_r + bim[rows, sl[k]]
                    bre[rows, sl[k]] = n_r
                    bim[rows, sl[k]] = n_i
                    x[k] = (n_r, n_i)

    def output_map(j):
        r = slice(j * sub_rows, (j + 1) * sub_rows)
        for hf in range(2):
            cols = slice(hf * half_s, (hf + 1) * half_s)
            y = jnp.dot(bre[r, cols].astype(BF16), cm_ref[hf, 0:half_s, :], preferred_element_type=F32)
            y = y + jnp.dot(bim[r, cols].astype(BF16), cm_ref[hf, half_s:, :],
                            preferred_element_type=F32)
            oc = slice(hf * half_w, (hf + 1) * half_w)
            ut[r, oc] = y + dskip_ref[:, oc] * ut[r, oc]

    input_map(0)
    for j in range(n_sub):
        if j + 1 < n_sub:
            input_map(j + 1)
        scan(j)
        if j > 0:
            output_map(j - 1)
    last = slice(tt * nb - nb, tt * nb)
    hre[...] = bre[last, :]
    him[...] = bim[last, :]
    output_map(n_sub - 1)

    def to_batch_major(t, _):
        rows = pl.ds(pl.multiple_of(t * nb, nb), nb)
        for q in range(n_q):
            su[q, pl.ds(t, nb, stride=pitch), :] = ut[rows, q * LANES:(q + 1) * LANES]
        return 0

    lax.fori_loop(0, tt, to_batch_major, 0)
    for b in range(nb):
        for q in range(n_q):
            y_ref[b, :, q * LANES:(q + 1) * LANES] = su[q, b * pitch:b * pitch + tt, :]


def _s5_scan(u3, b_mat, c_mat, a_re, a_im, d_skip, *, tt=128, n_sub=2):
    b, s, _ = u3.shape
    nb = SCAN_SEGS
    pitch = tt + 4
    n_state = SSM_GROUPS * SSM_STATE
    din_blk = 2 * LRU_WIDTH // SSM_WIDTH
    body = functools.partial(_s5_body, tt=tt, pitch=pitch, n_sub=n_sub)
    full = lambda shape: pl.BlockSpec(shape, lambda bi, ti: (0,) * len(shape))
    return pl.pallas_call(
        body,
        grid=(b // nb, s // tt),
        in_specs=[
            pl.BlockSpec((nb, tt, SSM_WIDTH), lambda bi, ti: (bi, ti, din_blk)),
            full(b_mat.shape), full(c_mat.shape), full((1, n_state)), full((1, n_state)),
            full((1, SSM_WIDTH)),
        ],
        out_specs=pl.BlockSpec((nb, tt, SSM_WIDTH), lambda bi, ti: (bi, ti, 0)),
        out_shape=jax.ShapeDtypeStruct((b, s, SSM_WIDTH), F32),
        scratch_shapes=[
            pltpu.VMEM((SSM_WIDTH // LANES, nb * pitch, LANES), F32),
            pltpu.VMEM((tt * nb, SSM_WIDTH), F32),
            pltpu.VMEM((tt * nb, n_state), F32),
            pltpu.VMEM((tt * nb, n_state), F32),
            pltpu.VMEM((nb, n_state), F32),
            pltpu.VMEM((nb, n_state), F32),
        ],
        compiler_params=_cparams(("parallel", "arbitrary")),
        name="s5_scan",
    )(u3, b_mat, c_mat, a_re, a_im, d_skip)


def _s5_operators(a_re, a_im, b_re, b_im, c_re, c_im, log_dt):
    a_re, a_im = a_re.astype(F32), a_im.astype(F32)
    dt = jnp.exp(log_dt.astype(F32))[:, None]
    mag = jnp.exp(a_re * dt)
    ab_re, ab_im = mag * jnp.cos(a_im * dt), mag * jnp.sin(a_im * dt)
    den = a_re * a_re + a_im * a_im
    f_re = ((ab_re - 1.0) * a_re + ab_im * a_im) / den
    f_im = (ab_im * a_re - (ab_re - 1.0) * a_im) / den
    b_re, b_im = b_re.astype(F32), b_im.astype(F32)
    bb_re = f_re[..., None] * b_re - f_im[..., None] * b_im
    bb_im = f_re[..., None] * b_im + f_im[..., None] * b_re

    gh = SSM_GROUPS // 2
    eye = jnp.eye(gh, dtype=F32)

    def in_map(bb):
        t = bb.reshape(2, gh, SSM_STATE, SSM_GROUP)
        t = t.transpose(0, 1, 3, 2)[:, :, :, None, :] * eye[None, :, None, :, None]
        return t.reshape(2, gh * SSM_GROUP, gh * SSM_STATE)

    def out_map(cc):
        t = cc.reshape(2, gh, SSM_GROUP, SSM_STATE)
        t = t.transpose(0, 1, 3, 2)[:, :, :, None, :] * eye[None, :, None, :, None]
        return t.reshape(2, gh * SSM_STATE, gh * SSM_GROUP)

    b_mat = jnp.concatenate([in_map(bb_re), in_map(bb_im)], axis=2).astype(BF16)
    c_mat = jnp.concatenate([out_map(c_re.astype(F32)), out_map(-c_im.astype(F32))],
                            axis=1).astype(BF16)
    n_state = SSM_GROUPS * SSM_STATE
    return b_mat, c_mat, ab_re.reshape(1, n_state), ab_im.reshape(1, n_state)


def _rec_out_body(yc_ref, y5_ref, dg_ref, wglu_ref, bglu_ref, w_ref, x_ref, o_ref, *, wc):
    yd = y5_ref[...]
    inner = math.sqrt(2.0 / math.pi) * (yd + 0.044715 * (yd * yd * yd))
    yd = 0.5 * yd * (1.0 + jnp.tanh(inner))
    glu = jnp.dot(yd.astype(BF16), wglu_ref[...], preferred_element_type=F32) + bglu_ref[...]
    yd = yd * _sigmoid(glu)
    g = dg_ref[...].astype(F32)
    yd = (yd * (g * _sigmoid(g))).astype(BF16)
    _proj_residual(yc_ref[...], yd, w_ref, x_ref, o_ref, wc)


def _rec_out(yc, y5, u2, w_glu, b_glu, w_bf16, x2, *, tm=512):
    t, d = x2.shape
    wc = yc.shape[1]
    gate_blk = 2 * LRU_WIDTH // SSM_WIDTH + 1
    return pl.pallas_call(
        functools.partial(_rec_out_body, wc=wc),
        grid=(t // tm,),
        in_specs=[
            pl.BlockSpec((tm, wc), lambda i: (i, 0)),
            pl.BlockSpec((tm, SSM_WIDTH), lambda i: (i, 0)),
            pl.BlockSpec((tm, SSM_WIDTH), lambda i: (i, gate_blk)),
            pl.BlockSpec((SSM_WIDTH, SSM_WIDTH), lambda i: (0, 0)),
            pl.BlockSpec((1, SSM_WIDTH), lambda i: (0, 0)),
            pl.BlockSpec((wc + SSM_WIDTH, d), lambda i: (0, 0)),
            pl.BlockSpec((tm, d), lambda i: (i, 0)),
        ],
        out_specs=pl.BlockSpec((tm, d), lambda i: (i, 0)),
        out_shape=jax.ShapeDtypeStruct((t, d), F32),
        compiler_params=_cparams(("parallel",)),
        name="rec_out",
    )(yc, y5, u2, w_glu, b_glu, w_bf16, x2)


def _attention_layer(x2, b, s, norm_g, w_in, q_g_a, k_g_a, rel_bias, q_g_b, k_g_b,
                     lq1, lk1, lq2, lk2, subln_g, w_out, layer_idx):
    scale = HEAD_DIM ** -0.5 * LOG2E
    ones = jnp.ones((1024,), F32)
    col_gain = jnp.concatenate([
        jnp.tile(q_g_a.astype(F32) * scale, A_HEADS), jnp.tile(k_g_a.astype(F32), A_HEADS), ones, ones,
        jnp.tile(q_g_b.astype(F32).reshape(-1) * scale, B_HEADS),
        jnp.tile(k_g_b.astype(F32).reshape(-1), B_HEADS), ones, ones])[None, :]
    tn = 2048
    u = _norm_proj(x2, (norm_g.astype(F32)[:, None] * w_in).astype(BF16), col_gain,
                   norm_tiles=(0, 2), tn=tn)
    u3 = u.reshape(b, s, ATTN_IN)
    bound_a = (BOUND_SLACK * HEAD_DIM * scale * jnp.max(jnp.abs(q_g_a.astype(F32)))
               * jnp.max(jnp.abs(k_g_a.astype(F32))))
    ya = _attn_a(u3, *_attn_a_bias(rel_bias, bound_a))
    lam_init = 0.8 - 0.6 * math.exp(-0.3 * layer_idx)
    row = lambda v: v.astype(F32)[None, :]
    qk_bound = (BOUND_SLACK * HEAD_DIM * scale * jnp.max(jnp.abs(q_g_b.astype(F32)))
                * jnp.max(jnp.abs(k_g_b.astype(F32)))).reshape(1, 1)
    yb = _attn_b(u3, row(lq1), row(lk1), row(lq2), row(lk2), row(subln_g), qk_bound, lam_init)
    return _out_proj(ya.reshape(b * s, A_WIDTH), yb.reshape(b * s, B_WIDTH), w_out.astype(BF16), x2)


def _recurrent_layer(x2, b, s, norm_g, w_in, conv_w, conv_b, w_a, b_a, w_x, b_x, lru_lam,
                     a_re, a_im, b_re, b_im, c_re, c_im, d_skip, log_dt, w_glu, b_glu, w_out):
    u = _norm_proj(x2, (norm_g.astype(F32)[:, None] * w_in).astype(BF16),
                   jnp.ones((1, REC_IN), F32), norm_tiles=(), tn=2048)
    u3 = u.reshape(b, s, REC_IN)
    yc = _rg_lru(u3, 0.5 * conv_w.astype(F32), 0.5 * conv_b.astype(F32)[None, :],
                 w_a.astype(BF16), 0.5 * b_a.astype(F32)[:, None, :],
                 w_x.astype(BF16), 0.5 * b_x.astype(F32)[:, None, :],
                 lru_lam.astype(F32)[None, :])

    b_mat, c_mat, ab_re, ab_im = _s5_operators(a_re, a_im, b_re, b_im, c_re, c_im, log_dt)
    y5 = _s5_scan(u3, b_mat, c_mat, ab_re, ab_im, d_skip.astype(F32)[None, :])
    return _rec_out(yc.reshape(b * s, LRU_WIDTH), y5.reshape(b * s, SSM_WIDTH), u,
                    w_glu.astype(BF16), b_glu.astype(F32)[None, :], w_out.astype(BF16), x2)


def kernel(x, attn_norm_g, attn_w_in, a_q_g, a_k_g, a_rel_bias, b_q_g, b_k_g, b_lam_q1, b_lam_k1,
           b_lam_q2, b_lam_k2, b_subln_g, attn_w_out, rec_norm_g, rec_w_in, lru_conv_w, lru_conv_b,
           lru_w_a, lru_b_a, lru_w_x, lru_b_x, lru_lambda, ssm_a_re, ssm_a_im, ssm_b_re, ssm_b_im,
           ssm_c_re, ssm_c_im, ssm_d, ssm_log_dt, ssm_w_glu, ssm_b_glu, rec_w_out):
    b, s, d = x.shape
    depth = attn_norm_g.shape[0] + rec_norm_g.shape[0]
    x2 = x.reshape(b * s, d)
    for layer in range(depth):
        j = layer // 2
        if layer % 2 == 0:
            x2 = _attention_layer(x2, b, s, attn_norm_g[j], attn_w_in[j], a_q_g[j], a_k_g[j],
                                  a_rel_bias[j], b_q_g[j], b_k_g[j], b_lam_q1[j], b_lam_k1[j],
                                  b_lam_q2[j], b_lam_k2[j], b_subln_g[j], attn_w_out[j], layer)
        else:
            x2 = _recurrent_layer(x2, b, s, rec_norm_g[j], rec_w_in[j], lru_conv_w[j], lru_conv_b[j],
                                  lru_w_a[j], lru_b_a[j], lru_w_x[j], lru_b_x[j], lru_lambda[j],
                                  ssm_a_re[j], ssm_a_im[j], ssm_b_re[j], ssm_b_im[j],
                                  ssm_c_re[j], ssm_c_im[j], ssm_d[j], ssm_log_dt[j],
                                  ssm_w_glu[j], ssm_b_glu[j], rec_w_out[j])
    return x2.reshape(b, s, d)
```

```python
import functools
import math

import jax
import jax.numpy as jnp
from jax import lax
from jax.experimental import pallas as pl
from jax.experimental.pallas import tpu as pltpu

F32 = jnp.float32
BF16 = jnp.bfloat16

D_MODEL = 2048
CHUNK = 64
NEG_INF = -1e30
NORM_EPS = 1e-6

HEAD_DIM = 128
A_HEADS = 8
A_WIDTH = A_HEADS * HEAD_DIM
A_LEFT_CHUNKS = 8
A_MAX_REL = 128
B_HEADS = 4
B_V_DIM = 2 * HEAD_DIM
B_WIDTH = B_HEADS * B_V_DIM
ATTN_IN = 8 * 1024

LRU_BLOCKS = 6
LRU_BLOCK_W = 256
LRU_WIDTH = LRU_BLOCKS * LRU_BLOCK_W
CONV_W = 4
LRU_C = 8.0
SSM_GROUP = 16
SSM_GROUPS = 32
SSM_WIDTH = SSM_GROUPS * SSM_GROUP
SSM_STATE = 64
REC_IN = 2 * LRU_WIDTH + 2 * SSM_WIDTH

LANES = 128
SUBLANES = 8
MXU_N = 256
VMEM_LIMIT = 56 * 1024 * 1024

Q_BLK = 256
K_TILE = 256
LOG2E = 1.0 / math.log(2.0)
BOUND_SLACK = 1.0 + 2.0 ** -6
ONE_PASS_RANGE = 100.0
A_WIN = A_LEFT_CHUNKS * CHUNK + Q_BLK
SCAN_SEGS = SUBLANES


def _sigmoid(x):
    return 0.5 * jnp.tanh(0.5 * x) + 0.5


def _cparams(sem):
    return pltpu.CompilerParams(dimension_semantics=sem, vmem_limit_bytes=VMEM_LIMIT)


def _norm_proj_body(x_ref, w_ref, cg_ref, o_ref, h_ref, r_ref, *, norm_tiles, tn):
    j = pl.program_id(1)

    @pl.when(j == 0)
    def _():
        x = x_ref[...]
        ms = jnp.mean(x * x, axis=-1, keepdims=True)
        r_ref[...] = jnp.broadcast_to(lax.rsqrt(ms + NORM_EPS), r_ref.shape)
        h_ref[...] = x.astype(BF16)

    def step(head_norm):
        for n in range(tn // MXU_N):
            acc = jnp.dot(h_ref[...], w_ref[:, n * MXU_N:(n + 1) * MXU_N],
                          preferred_element_type=F32)
            for hh in range(MXU_N // HEAD_DIM):
                sl = slice(n * MXU_N + hh * HEAD_DIM, n * MXU_N + (hh + 1) * HEAD_DIM)
                blk = acc[:, hh * HEAD_DIM:(hh + 1) * HEAD_DIM] * r_ref[...]
                if head_norm:
                    ms = jnp.mean(blk * blk, axis=-1, keepdims=True)
                    blk = blk * lax.rsqrt(ms + NORM_EPS) * cg_ref[:, sl]
                o_ref[:, sl] = blk.astype(BF16)

    if norm_tiles:
        is_norm = functools.reduce(jnp.logical_or, [j == t for t in norm_tiles])
        pl.when(is_norm)(functools.partial(step, True))
        pl.when(jnp.logical_not(is_norm))(functools.partial(step, False))
    else:
        step(False)


def _norm_proj(x2, w_bf16, col_gain, norm_tiles, *, tm=1024, tn=1024):
    t, d = x2.shape
    n = w_bf16.shape[1]
    body = functools.partial(_norm_proj_body, norm_tiles=tuple(norm_tiles), tn=tn)
    return pl.pallas_call(
        body,
        grid=(t // tm, n // tn),
        in_specs=[
            pl.BlockSpec((tm, d), lambda i, j: (i, 0)),
            pl.BlockSpec((d, tn), lambda i, j: (0, j)),
            pl.BlockSpec((1, tn), lambda i, j: (0, j)),
        ],
        out_specs=pl.BlockSpec((tm, tn), lambda i, j: (i, j)),
        out_shape=jax.ShapeDtypeStruct((t, n), BF16),
        scratch_shapes=[pltpu.VMEM((tm, d), BF16), pltpu.VMEM((tm, LANES), F32)],
        compiler_params=_cparams(("parallel", "arbitrary")),
        name="norm_proj",
    )(x2, w_bf16, col_gain)


def _softmax_tiles(logits_fn, n_t, s_scr, p_scr, m_scr, l_scr):
    half = K_TILE // 2
    for t in range(n_t):
        s = logits_fn(t)
        s_scr[:, t * K_TILE:(t + 1) * K_TILE] = s
        hm = jnp.maximum(s[:, 0:half], s[:, half:])
        m_scr[...] = hm if t == 0 else jnp.maximum(m_scr[...], hm)
    m = jnp.max(m_scr[...], axis=-1, keepdims=True)
    m_scr[...] = jnp.broadcast_to(m, m_scr.shape)
    for t in range(n_t):
        lo = slice(t * K_TILE, t * K_TILE + half)
        hi = slice(t * K_TILE + half, (t + 1) * K_TILE)
        p_lo = jnp.exp2(s_scr[:, lo] - m_scr[...])
        p_hi = jnp.exp2(s_scr[:, hi] - m_scr[...])
        if l_scr is not None:
            l_scr[...] = (p_lo + p_hi) if t == 0 else l_scr[...] + (p_lo + p_hi)
        p_scr[:, lo] = p_lo.astype(BF16)
        p_scr[:, hi] = p_hi.astype(BF16)
    if l_scr is None:
        return None
    return jnp.sum(l_scr[...], axis=-1, keepdims=True)


def _attn_a_body(q_ref, k_ref, v_ref, g_ref, bias_ref, span_ref, o_ref, s_scr, p_scr, m_scr, v1_scr, *,
                 heads, seq):
    one_pass_ok = span_ref[0, 0] < ONE_PASS_RANGE
    nt_dims = (((1,), (1,)), ((), ()))

    def run(one_pass):
        for hh in range(heads):
            cols = slice(hh * HEAD_DIM, (hh + 1) * HEAD_DIM)
            v1_scr[hh, :, 0:HEAD_DIM] = v_ref[:, cols]
            v1_scr[hh, :, HEAD_DIM:] = jnp.ones((seq, HEAD_DIM), BF16)
            for i in range(seq // Q_BLK):
                rows = slice(i * Q_BLK, (i + 1) * Q_BLK)
                ks = max(0, i * Q_BLK - A_LEFT_CHUNKS * CHUNK)
                win = (i + 1) * Q_BLK - ks
                q = q_ref[rows, cols]

                def logits(t, q=q, ks=ks, win=win, hh=hh, cols=cols):
                    kt = k_ref[ks + t * K_TILE:ks + (t + 1) * K_TILE, cols]
                    b0 = A_WIN - win + t * K_TILE
                    return (lax.dot_general(q, kt, nt_dims, preferred_element_type=F32)
                            + bias_ref[hh, :, b0:b0 + K_TILE])

                if one_pass:
                    for t in range(win // K_TILE):
                        p_scr[:, t * K_TILE:(t + 1) * K_TILE] = jnp.exp2(logits(t)).astype(BF16)
                else:
                    _softmax_tiles(logits, win // K_TILE, s_scr, p_scr, m_scr, None)
                ol = jnp.dot(p_scr[:, 0:win], v1_scr[hh, ks:ks + win, :], preferred_element_type=F32)
                o = ol[:, 0:HEAD_DIM] / ol[:, HEAD_DIM:HEAD_DIM + 1]
                g = g_ref[rows, cols].astype(F32)
                o_ref[rows, cols] = (o * (g * _sigmoid(g))).astype(BF16)

    pl.when(one_pass_ok)(functools.partial(run, True))
    pl.when(jnp.logical_not(one_pass_ok))(functools.partial(run, False))


def _attn_a(u3, bias, span, *, heads_per_step=4):
    b, s, _ = u3.shape
    w = heads_per_step * HEAD_DIM
    per = A_WIDTH // w
    body = functools.partial(_attn_a_body, heads=heads_per_step, seq=s)

    def col_spec(section):
        return pl.BlockSpec((None, s, w), lambda bi, hg: (bi, 0, section * per + hg))

    return pl.pallas_call(
        body,
        grid=(b, per),
        in_specs=[
            col_spec(0), col_spec(1), col_spec(2), col_spec(3),
            pl.BlockSpec((heads_per_step, Q_BLK, A_WIN), lambda bi, hg: (hg, 0, 0)),
            pl.BlockSpec((1, 1), lambda bi, hg: (0, 0)),
        ],
        out_specs=pl.BlockSpec((None, s, w), lambda bi, hg: (bi, 0, hg)),
        out_shape=jax.ShapeDtypeStruct((b, s, A_WIDTH), BF16),
        scratch_shapes=[pltpu.VMEM((Q_BLK, A_WIN), F32), pltpu.VMEM((Q_BLK, A_WIN), BF16),
                        pltpu.VMEM((Q_BLK, K_TILE // 2), F32),
                        pltpu.VMEM((heads_per_step, s, 2 * HEAD_DIM), BF16)],
        compiler_params=_cparams(("parallel", "arbitrary")),
        name="attn_a",
    )(u3, u3, u3, u3, bias, span)


def _attn_a_bias(rel_bias, qk_bound):
    h = rel_bias.shape[0]
    rb = rel_bias.astype(F32)
    n_v = Q_BLK + A_WIN - 1
    lo = A_LEFT_CHUNKS * CHUNK + Q_BLK - 1 - A_MAX_REL
    v = jnp.concatenate([jnp.broadcast_to(rb[:, :1], (h, lo + 1)), rb[:, 1:2 * A_MAX_REL],
                         jnp.broadcast_to(rb[:, -1:], (h, n_v - lo - 2 * A_MAX_REL))], axis=1)
    v = jnp.roll(v, -(Q_BLK - 1), axis=1)
    flat = jnp.tile(v, (1, Q_BLK))[:, :Q_BLK * (n_v - 1)]
    toep = flat.reshape(h, Q_BLK, n_v - 1)[:, :, :A_WIN]
    qc = jnp.arange(Q_BLK)[:, None] // CHUNK
    kc = jnp.floor_divide(jnp.arange(A_WIN)[None, :] - A_LEFT_CHUNKS * CHUNK, CHUNK)
    allowed = (kc <= qc) & (kc >= qc - A_LEFT_CHUNKS)
    rb_max = jnp.max(rb, axis=1)
    shift = (qk_bound + LOG2E * rb_max)[:, None, None]
    span = jnp.max(2.0 * qk_bound + LOG2E * (rb_max - rb[:, A_MAX_REL])).reshape(1, 1)
    return jnp.where(allowed[None], toep * LOG2E - shift, NEG_INF), span


def _attn_b_body(q_ref, k_ref, v_ref, g_ref, lq1_ref, lk1_ref, lq2_ref, lk2_ref, sub_ref, mb_ref,
                 o_ref, bias_ref, s_scr, p_scr, m_scr, l_scr, qa_scr, ka_scr, *, seq, lam_init):
    h = pl.program_id(0)
    n_blk = seq // Q_BLK
    half = K_TILE // 2
    chunk_bits = CHUNK.bit_length() - 1
    nt_dims = (((1,), (1,)), ((), ()))
    slope2 = LOG2E * jnp.exp2((-8.0 / B_HEADS) * jnp.full((1, 1), h + 1, jnp.int32).astype(F32))

    lam = (jnp.exp(jnp.sum(lq1_ref[...] * lk1_ref[...], axis=-1, keepdims=True))
           - jnp.exp(jnp.sum(lq2_ref[...] * lk2_ref[...], axis=-1, keepdims=True)) + lam_init)

    m_bound = mb_ref[...]

    @pl.when(pl.program_id(1) == 0)
    def _():
        pos = slope2 * lax.broadcasted_iota(jnp.int32, (seq, 1), 0).astype(F32)
        p_hi = pos.astype(BF16).astype(F32)
        p_mid = (pos - p_hi).astype(BF16).astype(F32)
        p_lo = pos - p_hi - p_mid
        lane = lax.broadcasted_iota(jnp.int32, (seq, HEAD_DIM), 1)
        aug_q = jnp.where(lane == 0, -m_bound, jnp.where(lane == 1, -p_hi, jnp.where(
            lane == 2, -p_mid, jnp.where(lane == 3, -p_lo, jnp.where(lane < 7, 1.0, 0.0)))))
        aug_k = jnp.where(lane < 4, 1.0, jnp.where(lane == 4, p_hi, jnp.where(
            lane == 5, p_mid, jnp.where(lane == 6, p_lo, 0.0))))
        for comp in range(2):
            qa_scr[comp, :, HEAD_DIM:] = aug_q.astype(BF16)
            ka_scr[comp, :, HEAD_DIM:] = aug_k.astype(BF16)

    for comp in range(2):
        cols = slice(comp * HEAD_DIM, (comp + 1) * HEAD_DIM)
        qa_scr[comp, :, 0:HEAD_DIM] = q_ref[:, cols]
        ka_scr[comp, :, 0:HEAD_DIM] = k_ref[:, cols]
    one_pass_ok = 2.0 * m_bound[0, 0] < ONE_PASS_RANGE

    def finish_block(i, outs):
        rows = slice(i * Q_BLK, (i + 1) * Q_BLK)
        o = outs[0] - outs[1]
        ms = jnp.mean(o * o, axis=-1, keepdims=True)
        o = o * lax.rsqrt(ms + NORM_EPS) * sub_ref[...] * (1.0 - lam_init)
        g = g_ref[rows, :].astype(F32)
        o_ref[rows, :] = (o * (g * _sigmoid(g))).astype(BF16)

    def weighted_values(comp, win, l):
        pv = jnp.dot(p_scr[:, 0:win], v_ref[0:win, :], preferred_element_type=F32)
        return pv * ((1.0 if comp == 0 else lam) / l)

    def one_pass():
        r = lax.broadcasted_iota(jnp.int32, (Q_BLK, K_TILE), 0)
        c = lax.broadcasted_iota(jnp.int32, (Q_BLK, K_TILE), 1)
        ahead = jnp.maximum(c - r, 0).astype(F32)
        allowed = lax.shift_right_arithmetic(c, chunk_bits) <= lax.shift_right_arithmetic(r, chunk_bits)
        s_scr[:, 0:K_TILE] = jnp.where(allowed, -2.0 * slope2 * ahead, NEG_INF)
        for i in range(n_blk):
            rows = slice(i * Q_BLK, (i + 1) * Q_BLK)
            n_t = (i + 1) * Q_BLK // K_TILE
            outs = []
            for comp in range(2):
                qa = qa_scr[comp, rows, :]
                for t in range(n_t):
                    z = lax.dot_general(qa, ka_scr[comp, t * K_TILE:(t + 1) * K_TILE, :], nt_dims,
                                        preferred_element_type=F32)
                    if t == n_t - 1:
                        z = z + s_scr[:, 0:K_TILE]
                    e_lo = jnp.exp2(z[:, 0:half])
                    e_hi = jnp.exp2(z[:, half:])
                    l_scr[...] = (e_lo + e_hi) if t == 0 else l_scr[...] + (e_lo + e_hi)
                    p_scr[:, t * K_TILE:t * K_TILE + half] = e_lo.astype(BF16)
                    p_scr[:, t * K_TILE + half:(t + 1) * K_TILE] = e_hi.astype(BF16)
                l = jnp.sum(l_scr[...], axis=-1, keepdims=True)
                outs.append(weighted_values(comp, n_t * K_TILE, l))
            finish_block(i, outs)

    def two_pass():
        diag0 = seq - Q_BLK
        r = lax.broadcasted_iota(jnp.int32, (Q_BLK, seq), 0)
        c = lax.broadcasted_iota(jnp.int32, (Q_BLK, seq), 1) - diag0
        dist = jnp.abs(r - c).astype(F32)
        allowed = jnp.logical_or(c < 0, lax.shift_right_arithmetic(c, chunk_bits)
                                 <= lax.shift_right_arithmetic(r, chunk_bits))
        bias_ref[...] = jnp.where(allowed, -slope2 * dist, NEG_INF)
        for i in range(n_blk):
            rows = slice(i * Q_BLK, (i + 1) * Q_BLK)
            n_t = (i + 1) * Q_BLK // K_TILE
            win = n_t * K_TILE
            outs = []
            for comp in range(2):
                cols = slice(comp * HEAD_DIM, (comp + 1) * HEAD_DIM)
                q = q_ref[rows, cols]

                def logits(t, q=q, win=win, cols=cols):
                    b0 = seq - win + t * K_TILE
                    return (lax.dot_general(q, k_ref[t * K_TILE:(t + 1) * K_TILE, cols], nt_dims,
                                            preferred_element_type=F32)
                            + bias_ref[:, b0:b0 + K_TILE])

                l = _softmax_tiles(logits, n_t, s_scr, p_scr, m_scr, l_scr)
                outs.append(weighted_values(comp, win, l))
            finish_block(i, outs)

    pl.when(one_pass_ok)(one_pass)
    pl.when(jnp.logical_not(one_pass_ok))(two_pass)


def _attn_b(u3, lq1, lk1, lq2, lk2, subln_g, qk_bound, lam_init):
    b, s, _ = u3.shape
    per = B_WIDTH // B_V_DIM
    body = functools.partial(_attn_b_body, seq=s, lam_init=lam_init)

    def col_spec(section):
        return pl.BlockSpec((None, s, B_V_DIM), lambda h, bi: (bi, 0, section * per + h))

    def vec_spec(n):
        return pl.BlockSpec((1, n), lambda h, bi: (0, 0))

    return pl.pallas_call(
        body,
        grid=(B_HEADS, b),
        in_specs=[col_spec(4), col_spec(5), col_spec(6), col_spec(7),
                  vec_spec(HEAD_DIM), vec_spec(HEAD_DIM), vec_spec(HEAD_DIM), vec_spec(HEAD_DIM),
                  vec_spec(B_V_DIM), vec_spec(1)],
        out_specs=pl.BlockSpec((None, s, B_V_DIM), lambda h, bi: (bi, 0, h)),
        out_shape=jax.ShapeDtypeStruct((b, s, B_WIDTH), BF16),
        scratch_shapes=[pltpu.VMEM((Q_BLK, s), F32), pltpu.VMEM((Q_BLK, s), F32),
                        pltpu.VMEM((Q_BLK, s), BF16), pltpu.VMEM((Q_BLK, K_TILE // 2), F32),
                        pltpu.VMEM((Q_BLK, K_TILE // 2), F32),
                        pltpu.VMEM((2, s, 2 * HEAD_DIM), BF16), pltpu.VMEM((2, s, 2 * HEAD_DIM), BF16)],
        compiler_params=_cparams(("parallel", "arbitrary")),
        name="attn_b",
    )(u3, u3, u3, u3, lq1, lk1, lq2, lk2, subln_g, qk_bound)


def _proj_residual(ya, yb, w_ref, x_ref, o_ref, wa):
    for n in range(o_ref.shape[1] // MXU_N):
        cols = slice(n * MXU_N, (n + 1) * MXU_N)
        acc = jnp.dot(ya, w_ref[0:wa, cols], preferred_element_type=F32)
        acc = acc + jnp.dot(yb, w_ref[wa:, cols], preferred_element_type=F32)
        o_ref[:, cols] = x_ref[:, cols] + acc


def _out_proj_body(ya_ref, yb_ref, w_ref, x_ref, o_ref, *, wa):
    _proj_residual(ya_ref[...], yb_ref[...], w_ref, x_ref, o_ref, wa)


def _out_proj(ya, yb, w_bf16, x2, *, tm=512):
    t, d = x2.shape
    wa, wb = ya.shape[1], yb.shape[1]
    return pl.pallas_call(
        functools.partial(_out_proj_body, wa=wa),
        grid=(t // tm,),
        in_specs=[
            pl.BlockSpec((tm, wa), lambda i: (i, 0)),
            pl.BlockSpec((tm, wb), lambda i: (i, 0)),
            pl.BlockSpec((wa + wb, d), lambda i: (0, 0)),
            pl.BlockSpec((tm, d), lambda i: (i, 0)),
        ],
        out_specs=pl.BlockSpec((tm, d), lambda i: (i, 0)),
        out_shape=jax.ShapeDtypeStruct((t, d), F32),
        compiler_params=_cparams(("parallel",)),
        name="out_proj",
    )(ya, yb, w_bf16, x2)


LRU_SEG_GROUPS = 2


def _scan_pitch(seq):
    seg = -(-seq // (SCAN_SEGS * LRU_SEG_GROUPS))
    return seg + (4 - seg) % SUBLANES


def _lru_body(x_ref, g_ref, cw_ref, cb_ref, wa_ref, ba_ref, wx_ref, bx_ref, lam_ref,
              o_ref, x_s, a_s, b_s, *, seq, pitch, nblk):
    n_slab = nblk * LRU_BLOCK_W // LANES
    slab_per_blk = LRU_BLOCK_W // LANES
    pad = LRU_SEG_GROUPS * SCAN_SEGS * pitch - seq

    for sl in range(n_slab):
        x_s[sl, 0:SUBLANES, :] = jnp.zeros((SUBLANES, LANES), F32)
        x_s[sl, SUBLANES:, :] = x_ref[:, sl * LANES:(sl + 1) * LANES].astype(F32)

    for blk in range(nblk):
        parts = []
        for half in range(slab_per_blk):
            sl = blk * slab_per_blk + half
            cols = slice(sl * LANES, (sl + 1) * LANES)
            xc = cb_ref[:, cols] + cw_ref[CONV_W - 1:CONV_W, cols] * x_s[sl, SUBLANES:, :]
            for d in range(1, CONV_W):
                xc = xc + (cw_ref[CONV_W - 1 - d:CONV_W - d, cols]
                           * x_s[sl, SUBLANES - d:SUBLANES - d + seq, :])
            parts.append(xc)
        xc = jnp.concatenate(parts, axis=1)
        bcols = slice(blk * LRU_BLOCK_W, (blk + 1) * LRU_BLOCK_W)

        xb = xc.astype(BF16)
        t_r = jnp.tanh(jnp.dot(xb, wa_ref[blk], preferred_element_type=F32) + ba_ref[blk])
        t_i = jnp.tanh(jnp.dot(xb, wx_ref[blk], preferred_element_type=F32) + bx_ref[blk])
        gi = t_i + 1.0
        lam = lam_ref[:, bcols]
        softplus_neg = jnp.maximum(-lam, 0.0) + jnp.log1p(jnp.exp(-jnp.abs(lam)))
        half_rate = (-0.5 * LRU_C * LOG2E) * softplus_neg
        a = jnp.exp2(t_r * half_rate + half_rate)
        y = 1.0 - a * a
        mult = jnp.where(y > 0.0, y * lax.rsqrt(y), 0.0)
        bb = mult * (gi * xc)
        for half in range(slab_per_blk):
            sl = blk * slab_per_blk + half
            cols = slice(half * LANES, (half + 1) * LANES)
            a_s[sl, 0:seq, :] = a[:, cols]
            b_s[sl, 0:seq, :] = bb[:, cols]
            a_s[sl, seq:, :] = jnp.zeros((pad, LANES), F32)
            b_s[sl, seq:, :] = jnp.zeros((pad, LANES), F32)

    chains = [(sl, grp) for sl in range(n_slab) for grp in range(LRU_SEG_GROUPS)]

    def seg_rows(j, grp):
        return pl.ds(j + grp * SCAN_SEGS * pitch, SCAN_SEGS, stride=pitch)

    def pass1(j, carry):
        new = []
        for n, (sl, grp) in enumerate(chains):
            hh, pp = carry[2 * n], carry[2 * n + 1]
            idx = seg_rows(j, grp)
            aj = a_s[sl, idx, :]
            bj = b_s[sl, idx, :]
            hh = aj * hh + bj
            pp = aj * pp
            b_s[sl, idx, :] = hh
            a_s[sl, idx, :] = pp
            new += [hh, pp]
        return tuple(new)

    init = []
    for _ in chains:
        init += [jnp.zeros((SCAN_SEGS, LANES), F32), jnp.ones((SCAN_SEGS, LANES), F32)]
    ends = lax.fori_loop(0, pitch, pass1, tuple(init), unroll=2)

    seg_idx = lax.broadcasted_iota(jnp.int32, (SCAN_SEGS, LANES), 0)
    carries = []
    for n, (sl, grp) in enumerate(chains):
        h_end, p_end = ends[2 * n], ends[2 * n + 1]
        if grp == 0:
            c = jnp.zeros((SCAN_SEGS, LANES), F32)
        else:
            prev = (ends[2 * n - 2] + ends[2 * n - 1] * carries[-1])[SCAN_SEGS - 1:SCAN_SEGS, :]
            c = jnp.where(seg_idx == 0, jnp.broadcast_to(prev, (SCAN_SEGS, LANES)), 0.0)
        for sgm in range(1, SCAN_SEGS):
            c = jnp.where(seg_idx == sgm, pltpu.roll(h_end + p_end * c, 1, axis=0), c)
        carries.append(c)

    def pass2(j, _):
        for n, (sl, grp) in enumerate(chains):
            idx = seg_rows(j, grp)
            b_s[sl, idx, :] = b_s[sl, idx, :] + a_s[sl, idx, :] * carries[n]
        return 0

    lax.fori_loop(0, pitch, pass2, 0, unroll=4)

    for sl in range(n_slab):
        cols = slice(sl * LANES, (sl + 1) * LANES)
        hg = 0.5 * g_ref[:, cols].astype(F32)
        silu = hg * jnp.tanh(hg) + hg
        o_ref[:, cols] = (b_s[sl, 0:seq, :] * silu).astype(BF16)


def _rg_lru(u3, conv_w, conv_b, w_a, b_a, w_x, b_x, lam, *, blocks_per_step=2):
    b, s, _ = u3.shape
    pitch = _scan_pitch(s)
    nblk = blocks_per_step
    width = nblk * LRU_BLOCK_W
    n_slab = width // LANES
    n_steps = LRU_BLOCKS // nblk
    body = functools.partial(_lru_body, seq=s, pitch=pitch, nblk=nblk)

    def vec_spec(rows):
        return pl.BlockSpec((rows, width), lambda bi, n: (0, n))

    def blk_spec(rows):
        return pl.BlockSpec((nblk, rows, LRU_BLOCK_W), lambda bi, n: (n, 0, 0))

    return pl.pallas_call(
        body,
        grid=(b, n_steps),
        in_specs=[
            pl.BlockSpec((None, s, width), lambda bi, n: (bi, 0, n)),
            pl.BlockSpec((None, s, width), lambda bi, n: (bi, 0, n_steps + n)),
            vec_spec(CONV_W), vec_spec(1),
            blk_spec(LRU_BLOCK_W), blk_spec(1), blk_spec(LRU_BLOCK_W), blk_spec(1),
            vec_spec(1),
        ],
        out_specs=pl.BlockSpec((None, s, width), lambda bi, n: (bi, 0, n)),
        out_shape=jax.ShapeDtypeStruct((b, s, LRU_WIDTH), BF16),
        scratch_shapes=[pltpu.VMEM((n_slab, s + SUBLANES, LANES), F32),
                        pltpu.VMEM((n_slab, LRU_SEG_GROUPS * SCAN_SEGS * pitch, LANES), F32),
                        pltpu.VMEM((n_slab, LRU_SEG_GROUPS * SCAN_SEGS * pitch, LANES), F32)],
        compiler_params=_cparams(("parallel", "arbitrary")),
        name="rg_lru",
    )(u3, u3, conv_w, conv_b, w_a, b_a, w_x, b_x, lam)


def _s5_body(u_ref, bm_ref, cm_ref, are_ref, aim_ref, dskip_ref, y_ref,
             su, ut, bre, bim, hre, him, *, tt, pitch, n_sub):
    nb = SCAN_SEGS
    n_q = SSM_WIDTH // LANES
    n_state = SSM_GROUPS * SSM_STATE
    half_w = SSM_WIDTH // 2
    half_s = n_state // 2

    @pl.when(pl.program_id(1) == 0)
    def _():
        hre[...] = jnp.zeros_like(hre)
        him[...] = jnp.zeros_like(him)

    for b in range(nb):
        ub = u_ref[b].astype(F32)
        for q in range(n_q):
            su[q, b * pitch:b * pitch + tt, :] = ub[:, q * LANES:(q + 1) * LANES]

    def to_time_major(t, _):
        rows = pl.ds(pl.multiple_of(t * nb, nb), nb)
        for q in range(n_q):
            ut[rows, q * LANES:(q + 1) * LANES] = su[q, pl.ds(t, nb, stride=pitch), :]
        return 0

    lax.fori_loop(0, tt, to_time_major, 0)

    sub_rows = (tt // n_sub) * nb
    n_slab = n_state // LANES
    group = 4

    def input_map(j):
        r = slice(j * sub_rows, (j + 1) * sub_rows)
        u_bf = ut[r, :].astype(BF16)
        for hf in range(2):
            uh = u_bf[:, hf * half_w:(hf + 1) * half_w]
            cols = slice(hf * half_s, (hf + 1) * half_s)
            bre[r, cols] = jnp.dot(uh, bm_ref[hf, :, 0:half_s], preferred_element_type=F32)
            bim[r, cols] = jnp.dot(uh, bm_ref[hf, :, half_s:], preferred_element_type=F32)

    def scan(j):
        for k0 in range(0, n_slab, group):
            sl = [slice((k0 + k) * LANES, (k0 + k + 1) * LANES) for k in range(group)]
            a_r = [jnp.broadcast_to(are_ref[:, c], (nb, LANES)) for c in sl]
            a_i = [jnp.broadcast_to(aim_ref[:, c], (nb, LANES)) for c in sl]
            if j == 0:
                x = [(hre[:, c], him[:, c]) for c in sl]
            else:
                prev = slice(j * sub_rows - nb, j * sub_rows)
                x = [(bre[prev, c], bim[prev, c]) for c in sl]
            for t in range(tt // n_sub):
                rows = slice(j * sub_rows + t * nb, j * sub_rows + (t + 1) * nb)
                for k in range(group):
                    x_r, x_i = x[k]
                    n_r = a_r[k] * x_r - a_i[k] * x_i + bre[rows, sl[k]]
                    n_i = a_r[k] * x_i + a_i[k] * x_r + bim[rows, sl[k]]
                    bre[rows, sl[k]] = n_r
                    bim[rows, sl[k]] = n_i
                    x[k] = (n_r, n_i)

    def output_map(j):
        r = slice(j * sub_rows, (j + 1) * sub_rows)
        for hf in range(2):
            cols = slice(hf * half_s, (hf + 1) * half_s)
            y = jnp.dot(bre[r, cols].astype(BF16), cm_ref[hf, 0:half_s, :], preferred_element_type=F32)
            y = y + jnp.dot(bim[r, cols].astype(BF16), cm_ref[hf, half_s:, :],
                            preferred_element_type=F32)
            oc = slice(hf * half_w, (hf + 1) * half_w)
            ut[r, oc] = y + dskip_ref[:, oc] * ut[r, oc]

    input_map(0)
    for j in range(n_sub):
        if j + 1 < n_sub:
            input_map(j + 1)
        scan(j)
        if j > 0:
            output_map(j - 1)
    last = slice(tt * nb - nb, tt * nb)
    hre[...] = bre[last, :]
    him[...] = bim[last, :]
    output_map(n_sub - 1)

    def to_batch_major(t, _):
        rows = pl.ds(pl.multiple_of(t * nb, nb), nb)
        for q in range(n_q):
            su[q, pl.ds(t, nb, stride=pitch), :] = ut[rows, q * LANES:(q + 1) * LANES]
        return 0

    lax.fori_loop(0, tt, to_batch_major, 0)
    for b in range(nb):
        for q in range(n_q):
            y_ref[b, :, q * LANES:(q + 1) * LANES] = su[q, b * pitch:b * pitch + tt, :]


def _s5_scan(u3, b_mat, c_mat, a_re, a_im, d_skip, *, tt=128, n_sub=2):
    b, s, _ = u3.shape
    nb = SCAN_SEGS
    pitch = tt + 4
    n_state = SSM_GROUPS * SSM_STATE
    din_blk = 2 * LRU_WIDTH // SSM_WIDTH
    body = functools.partial(_s5_body, tt=tt, pitch=pitch, n_sub=n_sub)
    full = lambda shape: pl.BlockSpec(shape, lambda bi, ti: (0,) * len(shape))
    return pl.pallas_call(
        body,
        grid=(b // nb, s // tt),
        in_specs=[
            pl.BlockSpec((nb, tt, SSM_WIDTH), lambda bi, ti: (bi, ti, din_blk)),
            full(b_mat.shape), full(c_mat.shape), full((1, n_state)), full((1, n_state)),
            full((1, SSM_WIDTH)),
        ],
        out_specs=pl.BlockSpec((nb, tt, SSM_WIDTH), lambda bi, ti: (bi, ti, 0)),
        out_shape=jax.ShapeDtypeStruct((b, s, SSM_WIDTH), F32),
        scratch_shapes=[
            pltpu.VMEM((SSM_WIDTH // LANES, nb * pitch, LANES), F32),
            pltpu.VMEM((tt * nb, SSM_WIDTH), F32),
            pltpu.VMEM((tt * nb, n_state), F32),
            pltpu.VMEM((tt * nb, n_state), F32),
            pltpu.VMEM((nb, n_state), F32),
            pltpu.VMEM((nb, n_state), F32),
        ],
        compiler_params=_cparams(("parallel", "arbitrary")),
        name="s5_scan",
    )(u3, b_mat, c_mat, a_re, a_im, d_skip)


def _s5_operators(a_re, a_im, b_re, b_im, c_re, c_im, log_dt):
    a_re, a_im = a_re.astype(F32), a_im.astype(F32)
    dt = jnp.exp(log_dt.astype(F32))[:, None]
    mag = jnp.exp(a_re * dt)
    ab_re, ab_im = mag * jnp.cos(a_im * dt), mag * jnp.sin(a_im * dt)
    den = a_re * a_re + a_im * a_im
    f_re = ((ab_re - 1.0) * a_re + ab_im * a_im) / den
    f_im = (ab_im * a_re - (ab_re - 1.0) * a_im) / den
    b_re, b_im = b_re.astype(F32), b_im.astype(F32)
    bb_re = f_re[..., None] * b_re - f_im[..., None] * b_im
    bb_im = f_re[..., None] * b_im + f_im[..., None] * b_re

    gh = SSM_GROUPS // 2
    eye = jnp.eye(gh, dtype=F32)

    def in_map(bb):
        t = bb.reshape(2, gh, SSM_STATE, SSM_GROUP)
        t = t.transpose(0, 1, 3, 2)[:, :, :, None, :] * eye[None, :, None, :, None]
        return t.reshape(2, gh * SSM_GROUP, gh * SSM_STATE)

    def out_map(cc):
        t = cc.reshape(2, gh, SSM_GROUP, SSM_STATE)
        t = t.transpose(0, 1, 3, 2)[:, :, :, None, :] * eye[None, :, None, :, None]
        return t.reshape(2, gh * SSM_STATE, gh * SSM_GROUP)

    b_mat = jnp.concatenate([in_map(bb_re), in_map(bb_im)], axis=2).astype(BF16)
    c_mat = jnp.concatenate([out_map(c_re.astype(F32)), out_map(-c_im.astype(F32))],
                            axis=1).astype(BF16)
    n_state = SSM_GROUPS * SSM_STATE
    return b_mat, c_mat, ab_re.reshape(1, n_state), ab_im.reshape(1, n_state)


def _rec_out_body(yc_ref, y5_ref, dg_ref, wglu_ref, bglu_ref, w_ref, x_ref, o_ref, *, wc):
    yd = y5_ref[...]
    inner = math.sqrt(2.0 / math.pi) * (yd + 0.044715 * (yd * yd * yd))
    yd = 0.5 * yd * (1.0 + jnp.tanh(inner))
    glu = jnp.dot(yd.astype(BF16), wglu_ref[...], preferred_element_type=F32) + bglu_ref[...]
    yd = yd * _sigmoid(glu)
    g = dg_ref[...].astype(F32)
    yd = (yd * (g * _sigmoid(g))).astype(BF16)
    _proj_residual(yc_ref[...], yd, w_ref, x_ref, o_ref, wc)


def _rec_out(yc, y5, u2, w_glu, b_glu, w_bf16, x2, *, tm=512):
    t, d = x2.shape
    wc = yc.shape[1]
    gate_blk = 2 * LRU_WIDTH // SSM_WIDTH + 1
    return pl.pallas_call(
        functools.partial(_rec_out_body, wc=wc),
        grid=(t // tm,),
        in_specs=[
            pl.BlockSpec((tm, wc), lambda i: (i, 0)),
            pl.BlockSpec((tm, SSM_WIDTH), lambda i: (i, 0)),
            pl.BlockSpec((tm, SSM_WIDTH), lambda i: (i, gate_blk)),
            pl.BlockSpec((SSM_WIDTH, SSM_WIDTH), lambda i: (0, 0)),
            pl.BlockSpec((1, SSM_WIDTH), lambda i: (0, 0)),
            pl.BlockSpec((wc + SSM_WIDTH, d), lambda i: (0, 0)),
            pl.BlockSpec((tm, d), lambda i: (i, 0)),
        ],
        out_specs=pl.BlockSpec((tm, d), lambda i: (i, 0)),
        out_shape=jax.ShapeDtypeStruct((t, d), F32),
        compiler_params=_cparams(("parallel",)),
        name="rec_out",
    )(yc, y5, u2, w_glu, b_glu, w_bf16, x2)


def _attention_layer(x2, b, s, norm_g, w_in, q_g_a, k_g_a, rel_bias, q_g_b, k_g_b,
                     lq1, lk1, lq2, lk2, subln_g, w_out, layer_idx):
    scale = HEAD_DIM ** -0.5 * LOG2E
    ones = jnp.ones((1024,), F32)
    col_gain = jnp.concatenate([
        jnp.tile(q_g_a.astype(F32) * scale, A_HEADS), jnp.tile(k_g_a.astype(F32), A_HEADS), ones, ones,
        jnp.tile(q_g_b.astype(F32).reshape(-1) * scale, B_HEADS),
        jnp.tile(k_g_b.astype(F32).reshape(-1), B_HEADS), ones, ones])[None, :]
    tn = 2048
    u = _norm_proj(x2, (norm_g.astype(F32)[:, None] * w_in).astype(BF16), col_gain,
                   norm_tiles=(0, 2), tn=tn)
    u3 = u.reshape(b, s, ATTN_IN)
    bound_a = (BOUND_SLACK * HEAD_DIM * scale * jnp.max(jnp.abs(q_g_a.astype(F32)))
               * jnp.max(jnp.abs(k_g_a.astype(F32))))
    ya = _attn_a(u3, *_attn_a_bias(rel_bias, bound_a))
    lam_init = 0.8 - 0.6 * math.exp(-0.3 * layer_idx)
    row = lambda v: v.astype(F32)[None, :]
    qk_bound = (BOUND_SLACK * HEAD_DIM * scale * jnp.max(jnp.abs(q_g_b.astype(F32)))
                * jnp.max(jnp.abs(k_g_b.astype(F32)))).reshape(1, 1)
    yb = _attn_b(u3, row(lq1), row(lk1), row(lq2), row(lk2), row(subln_g), qk_bound, lam_init)
    return _out_proj(ya.reshape(b * s, A_WIDTH), yb.reshape(b * s, B_WIDTH), w_out.astype(BF16), x2)


def _recurrent_layer(x2, b, s, norm_g, w_in, conv_w, conv_b, w_a, b_a, w_x, b_x, lru_lam,
                     a_re, a_im, b_re, b_im, c_re, c_im, d_skip, log_dt, w_glu, b_glu, w_out):
    u = _norm_proj(x2, (norm_g.astype(F32)[:, None] * w_in).astype(BF16),
                   jnp.ones((1, REC_IN), F32), norm_tiles=(), tn=2048)
    u3 = u.reshape(b, s, REC_IN)
    yc = _rg_lru(u3, 0.5 * conv_w.astype(F32), 0.5 * conv_b.astype(F32)[None, :],
                 w_a.astype(BF16), 0.5 * b_a.astype(F32)[:, None, :],
                 w_x.astype(BF16), 0.5 * b_x.astype(F32)[:, None, :],
                 lru_lam.astype(F32)[None, :])

    b_mat, c_mat, ab_re, ab_im = _s5_operators(a_re, a_im, b_re, b_im, c_re, c_im, log_dt)
    y5 = _s5_scan(u3, b_mat, c_mat, ab_re, ab_im, d_skip.astype(F32)[None, :])
    return _rec_out(yc.reshape(b * s, LRU_WIDTH), y5.reshape(b * s, SSM_WIDTH), u,
                    w_glu.astype(BF16), b_glu.astype(F32)[None, :], w_out.astype(BF16), x2)


def kernel(x, attn_norm_g, attn_w_in, a_q_g, a_k_g, a_rel_bias, b_q_g, b_k_g, b_lam_q1, b_lam_k1,
           b_lam_q2, b_lam_k2, b_subln_g, attn_w_out, rec_norm_g, rec_w_in, lru_conv_w, lru_conv_b,
           lru_w_a, lru_b_a, lru_w_x, lru_b_x, lru_lambda, ssm_a_re, ssm_a_im, ssm_b_re, ssm_b_im,
           ssm_c_re, ssm_c_im, ssm_d, ssm_log_dt, ssm_w_glu, ssm_b_glu, rec_w_out):
    b, s, d = x.shape
    depth = attn_norm_g.shape[0] + rec_norm_g.shape[0]
    x2 = x.reshape(b * s, d)
    for layer in range(depth):
        j = layer // 2
        if layer % 2 == 0:
            x2 = _attention_layer(x2, b, s, attn_norm_g[j], attn_w_in[j], a_q_g[j], a_k_g[j],
                                  a_rel_bias[j], b_q_g[j], b_k_g[j], b_lam_q1[j], b_lam_k1[j],
                                  b_lam_q2[j], b_lam_k2[j], b_subln_g[j], attn_w_out[j], layer)
        else:
            x2 = _recurrent_layer(x2, b, s, rec_norm_g[j], rec_w_in[j], lru_conv_w[j], lru_conv_b[j],
                                  lru_w_a[j], lru_b_a[j], lru_w_x[j], lru_b_x[j], lru_lambda[j],
                                  ssm_a_re[j], ssm_a_im[j], ssm_b_re[j], ssm_b_im[j],
                                  ssm_c_re[j], ssm_c_im[j], ssm_d[j], ssm_log_dt[j],
                                  ssm_w_glu[j], ssm_b_glu[j], rec_w_out[j])
    return x2.reshape(b, s, d)
```

```python
import functools
import math

import jax
import jax.numpy as jnp
from jax import lax
from jax.experimental import pallas as pl
from jax.experimental.pallas import tpu as pltpu

F32 = jnp.float32
BF16 = jnp.bfloat16

D_MODEL = 2048
CHUNK = 64
NEG_INF = -1e30
NORM_EPS = 1e-6

HEAD_DIM = 128
A_HEADS = 8
A_WIDTH = A_HEADS * HEAD_DIM
A_LEFT_CHUNKS = 8
A_MAX_REL = 128
B_HEADS = 4
B_V_DIM = 2 * HEAD_DIM
B_WIDTH = B_HEADS * B_V_DIM
ATTN_IN = 8 * 1024

LRU_BLOCKS = 6
LRU_BLOCK_W = 256
LRU_WIDTH = LRU_BLOCKS * LRU_BLOCK_W
CONV_W = 4
LRU_C = 8.0
SSM_GROUP = 16
SSM_GROUPS = 32
SSM_WIDTH = SSM_GROUPS * SSM_GROUP
SSM_STATE = 64
REC_IN = 2 * LRU_WIDTH + 2 * SSM_WIDTH

LANES = 128
SUBLANES = 8
MXU_N = 256
VMEM_LIMIT = 56 * 1024 * 1024

Q_BLK = 256
K_TILE = 256
LOG2E = 1.0 / math.log(2.0)
BOUND_SLACK = 1.0 + 2.0 ** -6
ONE_PASS_RANGE = 100.0
A_WIN = A_LEFT_CHUNKS * CHUNK + Q_BLK
SCAN_SEGS = SUBLANES


def _sigmoid(x):
    return 0.5 * jnp.tanh(0.5 * x) + 0.5


def _cparams(sem):
    return pltpu.CompilerParams(dimension_semantics=sem, vmem_limit_bytes=VMEM_LIMIT)


def _norm_proj_body(x_ref, w_ref, cg_ref, o_ref, h_ref, r_ref, *, norm_tiles, tn):
    j = pl.program_id(1)

    @pl.when(j == 0)
    def _():
        x = x_ref[...]
        ms = jnp.mean(x * x, axis=-1, keepdims=True)
        r_ref[...] = jnp.broadcast_to(lax.rsqrt(ms + NORM_EPS), r_ref.shape)
        h_ref[...] = x.astype(BF16)

    def step(head_norm):
        for n in range(tn // MXU_N):
            acc = jnp.dot(h_ref[...], w_ref[:, n * MXU_N:(n + 1) * MXU_N],
                          preferred_element_type=F32)
            for hh in range(MXU_N // HEAD_DIM):
                sl = slice(n * MXU_N + hh * HEAD_DIM, n * MXU_N + (hh + 1) * HEAD_DIM)
                blk = acc[:, hh * HEAD_DIM:(hh + 1) * HEAD_DIM] * r_ref[...]
                if head_norm:
                    ms = jnp.mean(blk * blk, axis=-1, keepdims=True)
                    blk = blk * lax.rsqrt(ms + NORM_EPS) * cg_ref[:, sl]
                o_ref[:, sl] = blk.astype(BF16)

    if norm_tiles:
        is_norm = functools.reduce(jnp.logical_or, [j == t for t in norm_tiles])
        pl.when(is_norm)(functools.partial(step, True))
        pl.when(jnp.logical_not(is_norm))(functools.partial(step, False))
    else:
        step(False)


def _norm_proj(x2, w_bf16, col_gain, norm_tiles, *, tm=1024, tn=1024):
    t, d = x2.shape
    n = w_bf16.shape[1]
    body = functools.partial(_norm_proj_body, norm_tiles=tuple(norm_tiles), tn=tn)
    return pl.pallas_call(
        body,
        grid=(t // tm, n // tn),
        in_specs=[
            pl.BlockSpec((tm, d), lambda i, j: (i, 0)),
            pl.BlockSpec((d, tn), lambda i, j: (0, j)),
            pl.BlockSpec((1, tn), lambda i, j: (0, j)),
        ],
        out_specs=pl.BlockSpec((tm, tn), lambda i, j: (i, j)),
        out_shape=jax.ShapeDtypeStruct((t, n), BF16),
        scratch_shapes=[pltpu.VMEM((tm, d), BF16), pltpu.VMEM((tm, LANES), F32)],
        compiler_params=_cparams(("parallel", "arbitrary")),
        name="norm_proj",
    )(x2, w_bf16, col_gain)


def _softmax_tiles(logits_fn, n_t, s_scr, p_scr, m_scr, l_scr):
    half = K_TILE // 2
    for t in range(n_t):
        s = logits_fn(t)
        s_scr[:, t * K_TILE:(t + 1) * K_TILE] = s
        hm = jnp.maximum(s[:, 0:half], s[:, half:])
        m_scr[...] = hm if t == 0 else jnp.maximum(m_scr[...], hm)
    m = jnp.max(m_scr[...], axis=-1, keepdims=True)
    m_scr[...] = jnp.broadcast_to(m, m_scr.shape)
    for t in range(n_t):
        lo = slice(t * K_TILE, t * K_TILE + half)
        hi = slice(t * K_TILE + half, (t + 1) * K_TILE)
        p_lo = jnp.exp2(s_scr[:, lo] - m_scr[...])
        p_hi = jnp.exp2(s_scr[:, hi] - m_scr[...])
        if l_scr is not None:
            l_scr[...] = (p_lo + p_hi) if t == 0 else l_scr[...] + (p_lo + p_hi)
        p_scr[:, lo] = p_lo.astype(BF16)
        p_scr[:, hi] = p_hi.astype(BF16)
    if l_scr is None:
        return None
    return jnp.sum(l_scr[...], axis=-1, keepdims=True)


def _attn_a_body(q_ref, k_ref, v_ref, g_ref, bias_ref, span_ref, o_ref, s_scr, p_scr, m_scr, v1_scr, *,
                 heads, seq):
    one_pass_ok = span_ref[0, 0] < ONE_PASS_RANGE
    nt_dims = (((1,), (1,)), ((), ()))

    def run(one_pass):
        for hh in range(heads):
            cols = slice(hh * HEAD_DIM, (hh + 1) * HEAD_DIM)
            v1_scr[hh, :, 0:HEAD_DIM] = v_ref[:, cols]
            v1_scr[hh, :, HEAD_DIM:] = jnp.ones((seq, HEAD_DIM), BF16)
            for i in range(seq // Q_BLK):
                rows = slice(i * Q_BLK, (i + 1) * Q_BLK)
                ks = max(0, i * Q_BLK - A_LEFT_CHUNKS * CHUNK)
                win = (i + 1) * Q_BLK - ks
                q = q_ref[rows, cols]

                def logits(t, q=q, ks=ks, win=win, hh=hh, cols=cols):
                    kt = k_ref[ks + t * K_TILE:ks + (t + 1) * K_TILE, cols]
                    b0 = A_WIN - win + t * K_TILE
                    return (lax.dot_general(q, kt, nt_dims, preferred_element_type=F32)
                            + bias_ref[hh, :, b0:b0 + K_TILE])

                if one_pass:
                    for t in range(win // K_TILE):
                        p_scr[:, t * K_TILE:(t + 1) * K_TILE] = jnp.exp2(logits(t)).astype(BF16)
                else:
                    _softmax_tiles(logits, win // K_TILE, s_scr, p_scr, m_scr, None)
                ol = jnp.dot(p_scr[:, 0:win], v1_scr[hh, ks:ks + win, :], preferred_element_type=F32)
                o = ol[:, 0:HEAD_DIM] / ol[:, HEAD_DIM:HEAD_DIM + 1]
                g = g_ref[rows, cols].astype(F32)
                o_ref[rows, cols] = (o * (g * _sigmoid(g))).astype(BF16)

    pl.when(one_pass_ok)(functools.partial(run, True))
    pl.when(jnp.logical_not(one_pass_ok))(functools.partial(run, False))


def _attn_a(u3, bias, span, *, heads_per_step=4):
    b, s, _ = u3.shape
    w = heads_per_step * HEAD_DIM
    per = A_WIDTH // w
    body = functools.partial(_attn_a_body, heads=heads_per_step, seq=s)

    def col_spec(section):
        return pl.BlockSpec((None, s, w), lambda bi, hg: (bi, 0, section * per + hg))

    return pl.pallas_call(
        body,
        grid=(b, per),
        in_specs=[
            col_spec(0), col_spec(1), col_spec(2), col_spec(3),
            pl.BlockSpec((heads_per_step, Q_BLK, A_WIN), lambda bi, hg: (hg, 0, 0)),
            pl.BlockSpec((1, 1), lambda bi, hg: (0, 0)),
        ],
        out_specs=pl.BlockSpec((None, s, w), lambda bi, hg: (bi, 0, hg)),
        out_shape=jax.ShapeDtypeStruct((b, s, A_WIDTH), BF16),
        scratch_shapes=[pltpu.VMEM((Q_BLK, A_WIN), F32), pltpu.VMEM((Q_BLK, A_WIN), BF16),
                        pltpu.VMEM((Q_BLK, K_TILE // 2), F32),
                        pltpu.VMEM((heads_per_step, s, 2 * HEAD_DIM), BF16)],
        compiler_params=_cparams(("parallel", "arbitrary")),
        name="attn_a",
    )(u3, u3, u3, u3, bias, span)


def _attn_a_bias(rel_bias, qk_bound):
    h = rel_bias.shape[0]
    rb = rel_bias.astype(F32)
    n_v = Q_BLK + A_WIN - 1
    lo = A_LEFT_CHUNKS * CHUNK + Q_BLK - 1 - A_MAX_REL
    v = jnp.concatenate([jnp.broadcast_to(rb[:, :1], (h, lo + 1)), rb[:, 1:2 * A_MAX_REL],
                         jnp.broadcast_to(rb[:, -1:], (h, n_v - lo - 2 * A_MAX_REL))], axis=1)
    v = jnp.roll(v, -(Q_BLK - 1), axis=1)
    flat = jnp.tile(v, (1, Q_BLK))[:, :Q_BLK * (n_v - 1)]
    toep = flat.reshape(h, Q_BLK, n_v - 1)[:, :, :A_WIN]
    qc = jnp.arange(Q_BLK)[:, None] // CHUNK
    kc = jnp.floor_divide(jnp.arange(A_WIN)[None, :] - A_LEFT_CHUNKS * CHUNK, CHUNK)
    allowed = (kc <= qc) & (kc >= qc - A_LEFT_CHUNKS)
    rb_max = jnp.max(rb, axis=1)
    shift = (qk_bound + LOG2E * rb_max)[:, None, None]
    span = jnp.max(2.0 * qk_bound + LOG2E * (rb_max - rb[:, A_MAX_REL])).reshape(1, 1)
    return jnp.where(allowed[None], toep * LOG2E - shift, NEG_INF), span


def _attn_b_body(q_ref, k_ref, v_ref, g_ref, lq1_ref, lk1_ref, lq2_ref, lk2_ref, sub_ref, mb_ref,
                 o_ref, bias_ref, s_scr, p_scr, m_scr, l_scr, qa_scr, ka_scr, *, seq, lam_init):
    h = pl.program_id(0)
    n_blk = seq // Q_BLK
    half = K_TILE // 2
    chunk_bits = CHUNK.bit_length() - 1
    nt_dims = (((1,), (1,)), ((), ()))
    slope2 = LOG2E * jnp.exp2((-8.0 / B_HEADS) * jnp.full((1, 1), h + 1, jnp.int32).astype(F32))

    lam = (jnp.exp(jnp.sum(lq1_ref[...] * lk1_ref[...], axis=-1, keepdims=True))
           - jnp.exp(jnp.sum(lq2_ref[...] * lk2_ref[...], axis=-1, keepdims=True)) + lam_init)

    m_bound = mb_ref[...]

    @pl.when(pl.program_id(1) == 0)
    def _():
        pos = slope2 * lax.broadcasted_iota(jnp.int32, (seq, 1), 0).astype(F32)
        p_hi = pos.astype(BF16).astype(F32)
        p_mid = (pos - p_hi).astype(BF16).astype(F32)
        p_lo = pos - p_hi - p_mid
        lane = lax.broadcasted_iota(jnp.int32, (seq, HEAD_DIM), 1)
        aug_q = jnp.where(lane == 0, -m_bound, jnp.where(lane == 1, -p_hi, jnp.where(
            lane == 2, -p_mid, jnp.where(lane == 3, -p_lo, jnp.where(lane < 7, 1.0, 0.0)))))
        aug_k = jnp.where(lane < 4, 1.0, jnp.where(lane == 4, p_hi, jnp.where(
            lane == 5, p_mid, jnp.where(lane == 6, p_lo, 0.0))))
        for comp in range(2):
            qa_scr[comp, :, HEAD_DIM:] = aug_q.astype(BF16)
            ka_scr[comp, :, HEAD_DIM:] = aug_k.astype(BF16)

    for comp in range(2):
        cols = slice(comp * HEAD_DIM, (comp + 1) * HEAD_DIM)
        qa_scr[comp, :, 0:HEAD_DIM] = q_ref[:, cols]
        ka_scr[comp, :, 0:HEAD_DIM] = k_ref[:, cols]
    one_pass_ok = 2.0 * m_bound[0, 0] < ONE_PASS_RANGE

    def finish_block(i, outs):
        rows = slice(i * Q_BLK, (i + 1) * Q_BLK)
        o = outs[0] - outs[1]
        ms = jnp.mean(o * o, axis=-1, keepdims=True)
        o = o * lax.rsqrt(ms + NORM_EPS) * sub_ref[...] * (1.0 - lam_init)
        g = g_ref[rows, :].astype(F32)
        o_ref[rows, :] = (o * (g * _sigmoid(g))).astype(BF16)

    def weighted_values(comp, win, l):
        pv = jnp.dot(p_scr[:, 0:win], v_ref[0:win, :], preferred_element_type=F32)
        return pv * ((1.0 if comp == 0 else lam) / l)

    def one_pass():
        r = lax.broadcasted_iota(jnp.int32, (Q_BLK, K_TILE), 0)
        c = lax.broadcasted_iota(jnp.int32, (Q_BLK, K_TILE), 1)
        ahead = jnp.maximum(c - r, 0).astype(F32)
        allowed = lax.shift_right_arithmetic(c, chunk_bits) <= lax.shift_right_arithmetic(r, chunk_bits)
        s_scr[:, 0:K_TILE] = jnp.where(allowed, -2.0 * slope2 * ahead, NEG_INF)
        for i in range(n_blk):
            rows = slice(i * Q_BLK, (i + 1) * Q_BLK)
            n_t = (i + 1) * Q_BLK // K_TILE
            outs = []
            for comp in range(2):
                qa = qa_scr[comp, rows, :]
                for t in range(n_t):
                    z = lax.dot_general(qa, ka_scr[comp, t * K_TILE:(t + 1) * K_TILE, :], nt_dims,
                                        preferred_element_type=F32)
                    if t == n_t - 1:
                        z = z + s_scr[:, 0:K_TILE]
                    e_lo = jnp.exp2(z[:, 0:half])
                    e_hi = jnp.exp2(z[:, half:])
                    l_scr[...] = (e_lo + e_hi) if t == 0 else l_scr[...] + (e_lo + e_hi)
                    p_scr[:, t * K_TILE:t * K_TILE + half] = e_lo.astype(BF16)
                    p_scr[:, t * K_TILE + half:(t + 1) * K_TILE] = e_hi.astype(BF16)
                l = jnp.sum(l_scr[...], axis=-1, keepdims=True)
                outs.append(weighted_values(comp, n_t * K_TILE, l))
            finish_block(i, outs)

    def two_pass():
        diag0 = seq - Q_BLK
        r = lax.broadcasted_iota(jnp.int32, (Q_BLK, seq), 0)
        c = lax.broadcasted_iota(jnp.int32, (Q_BLK, seq), 1) - diag0
        dist = jnp.abs(r - c).astype(F32)
        allowed = jnp.logical_or(c < 0, lax.shift_right_arithmetic(c, chunk_bits)
                                 <= lax.shift_right_arithmetic(r, chunk_bits))
        bias_ref[...] = jnp.where(allowed, -slope2 * dist, NEG_INF)
        for i in range(n_blk):
            rows = slice(i * Q_BLK, (i + 1) * Q_BLK)
            n_t = (i + 1) * Q_BLK // K_TILE
            win = n_t * K_TILE
            outs = []
            for comp in range(2):
                cols = slice(comp * HEAD_DIM, (comp + 1) * HEAD_DIM)
                q = q_ref[rows, cols]

                def logits(t, q=q, win=win, cols=cols):
                    b0 = seq - win + t * K_TILE
                    return (lax.dot_general(q, k_ref[t * K_TILE:(t + 1) * K_TILE, cols], nt_dims,
                                            preferred_element_type=F32)
                            + bias_ref[:, b0:b0 + K_TILE])

                l = _softmax_tiles(logits, n_t, s_scr, p_scr, m_scr, l_scr)
                outs.append(weighted_values(comp, win, l))
            finish_block(i, outs)

    pl.when(one_pass_ok)(one_pass)
    pl.when(jnp.logical_not(one_pass_ok))(two_pass)


def _attn_b(u3, lq1, lk1, lq2, lk2, subln_g, qk_bound, lam_init):
    b, s, _ = u3.shape
    per = B_WIDTH // B_V_DIM
    body = functools.partial(_attn_b_body, seq=s, lam_init=lam_init)

    def col_spec(section):
        return pl.BlockSpec((None, s, B_V_DIM), lambda h, bi: (bi, 0, section * per + h))

    def vec_spec(n):
        return pl.BlockSpec((1, n), lambda h, bi: (0, 0))

    return pl.pallas_call(
        body,
        grid=(B_HEADS, b),
        in_specs=[col_spec(4), col_spec(5), col_spec(6), col_spec(7),
                  vec_spec(HEAD_DIM), vec_spec(HEAD_DIM), vec_spec(HEAD_DIM), vec_spec(HEAD_DIM),
                  vec_spec(B_V_DIM), vec_spec(1)],
        out_specs=pl.BlockSpec((None, s, B_V_DIM), lambda h, bi: (bi, 0, h)),
        out_shape=jax.ShapeDtypeStruct((b, s, B_WIDTH), BF16),
        scratch_shapes=[pltpu.VMEM((Q_BLK, s), F32), pltpu.VMEM((Q_BLK, s), F32),
                        pltpu.VMEM((Q_BLK, s), BF16), pltpu.VMEM((Q_BLK, K_TILE // 2), F32),
                        pltpu.VMEM((Q_BLK, K_TILE // 2), F32),
                        pltpu.VMEM((2, s, 2 * HEAD_DIM), BF16), pltpu.VMEM((2, s, 2 * HEAD_DIM), BF16)],
        compiler_params=_cparams(("parallel", "arbitrary")),
        name="attn_b",
    )(u3, u3, u3, u3, lq1, lk1, lq2, lk2, subln_g, qk_bound)


def _proj_residual(ya, yb, w_ref, x_ref, o_ref, wa):
    for n in range(o_ref.shape[1] // MXU_N):
        cols = slice(n * MXU_N, (n + 1) * MXU_N)
        acc = jnp.dot(ya, w_ref[0:wa, cols], preferred_element_type=F32)
        acc = acc + jnp.dot(yb, w_ref[wa:, cols], preferred_element_type=F32)
        o_ref[:, cols] = x_ref[:, cols] + acc


def _out_proj_body(ya_ref, yb_ref, w_ref, x_ref, o_ref, *, wa):
    _proj_residual(ya_ref[...], yb_ref[...], w_ref, x_ref, o_ref, wa)


def _out_proj(ya, yb, w_bf16, x2, *, tm=512):
    t, d = x2.shape
    wa, wb = ya.shape[1], yb.shape[1]
    return pl.pallas_call(
        functools.partial(_out_proj_body, wa=wa),
        grid=(t // tm,),
        in_specs=[
            pl.BlockSpec((tm, wa), lambda i: (i, 0)),
            pl.BlockSpec((tm, wb), lambda i: (i, 0)),
            pl.BlockSpec((wa + wb, d), lambda i: (0, 0)),
            pl.BlockSpec((tm, d), lambda i: (i, 0)),
        ],
        out_specs=pl.BlockSpec((tm, d), lambda i: (i, 0)),
        out_shape=jax.ShapeDtypeStruct((t, d), F32),
        compiler_params=_cparams(("parallel",)),
        name="out_proj",
    )(ya, yb, w_bf16, x2)


LRU_SEG_GROUPS = 4


def _scan_pitch(seq):
    seg = -(-seq // (SCAN_SEGS * LRU_SEG_GROUPS))
    return seg + (4 - seg) % SUBLANES


def _lru_body(x_ref, g_ref, cw_ref, cb_ref, wa_ref, ba_ref, wx_ref, bx_ref, lam_ref,
              o_ref, x_s, a_s, b_s, *, seq, pitch, nblk):
    n_slab = nblk * LRU_BLOCK_W // LANES
    slab_per_blk = LRU_BLOCK_W // LANES
    pad = LRU_SEG_GROUPS * SCAN_SEGS * pitch - seq

    for sl in range(n_slab):
        x_s[sl, 0:SUBLANES, :] = jnp.zeros((SUBLANES, LANES), F32)
        x_s[sl, SUBLANES:, :] = x_ref[:, sl * LANES:(sl + 1) * LANES].astype(F32)

    for blk in range(nblk):
        parts = []
        for half in range(slab_per_blk):
            sl = blk * slab_per_blk + half
            cols = slice(sl * LANES, (sl + 1) * LANES)
            xc = cb_ref[:, cols] + cw_ref[CONV_W - 1:CONV_W, cols] * x_s[sl, SUBLANES:, :]
            for d in range(1, CONV_W):
                xc = xc + (cw_ref[CONV_W - 1 - d:CONV_W - d, cols]
                           * x_s[sl, SUBLANES - d:SUBLANES - d + seq, :])
            parts.append(xc)
        xc = jnp.concatenate(parts, axis=1)
        bcols = slice(blk * LRU_BLOCK_W, (blk + 1) * LRU_BLOCK_W)

        xb = xc.astype(BF16)
        t_r = jnp.tanh(jnp.dot(xb, wa_ref[blk], preferred_element_type=F32) + ba_ref[blk])
        t_i = jnp.tanh(jnp.dot(xb, wx_ref[blk], preferred_element_type=F32) + bx_ref[blk])
        gi = t_i + 1.0
        lam = lam_ref[:, bcols]
        softplus_neg = jnp.maximum(-lam, 0.0) + jnp.log1p(jnp.exp(-jnp.abs(lam)))
        half_rate = (-0.5 * LRU_C * LOG2E) * softplus_neg
        a = jnp.exp2(t_r * half_rate + half_rate)
        y = 1.0 - a * a
        mult = jnp.where(y > 0.0, y * lax.rsqrt(y), 0.0)
        bb = mult * (gi * xc)
        for half in range(slab_per_blk):
            sl = blk * slab_per_blk + half
            cols = slice(half * LANES, (half + 1) * LANES)
            a_s[sl, 0:seq, :] = a[:, cols]
            b_s[sl, 0:seq, :] = bb[:, cols]
            a_s[sl, seq:, :] = jnp.zeros((pad, LANES), F32)
            b_s[sl, seq:, :] = jnp.zeros((pad, LANES), F32)

    chains = [(sl, grp) for sl in range(n_slab) for grp in range(LRU_SEG_GROUPS)]

    def seg_rows(j, grp):
        return pl.ds(j + grp * SCAN_SEGS * pitch, SCAN_SEGS, stride=pitch)

    def pass1(j, carry):
        new = []
        for n, (sl, grp) in enumerate(chains):
            hh, pp = carry[2 * n], carry[2 * n + 1]
            idx = seg_rows(j, grp)
            aj = a_s[sl, idx, :]
            bj = b_s[sl, idx, :]
            hh = aj * hh + bj
            pp = aj * pp
            b_s[sl, idx, :] = hh
            a_s[sl, idx, :] = pp
            new += [hh, pp]
        return tuple(new)

    init = []
    for _ in chains:
        init += [jnp.zeros((SCAN_SEGS, LANES), F32), jnp.ones((SCAN_SEGS, LANES), F32)]
    ends = lax.fori_loop(0, pitch, pass1, tuple(init), unroll=2)

    seg_idx = lax.broadcasted_iota(jnp.int32, (SCAN_SEGS, LANES), 0)
    carries = []
    for n, (sl, grp) in enumerate(chains):
        h_end, p_end = ends[2 * n], ends[2 * n + 1]
        if grp == 0:
            c = jnp.zeros((SCAN_SEGS, LANES), F32)
        else:
            prev = (ends[2 * n - 2] + ends[2 * n - 1] * carries[-1])[SCAN_SEGS - 1:SCAN_SEGS, :]
            c = jnp.where(seg_idx == 0, jnp.broadcast_to(prev, (SCAN_SEGS, LANES)), 0.0)
        for sgm in range(1, SCAN_SEGS):
            c = jnp.where(seg_idx == sgm, pltpu.roll(h_end + p_end * c, 1, axis=0), c)
        carries.append(c)

    def pass2(j, _):
        for n, (sl, grp) in enumerate(chains):
            idx = seg_rows(j, grp)
            b_s[sl, idx, :] = b_s[sl, idx, :] + a_s[sl, idx, :] * carries[n]
        return 0

    lax.fori_loop(0, pitch, pass2, 0, unroll=4)

    for sl in range(n_slab):
        cols = slice(sl * LANES, (sl + 1) * LANES)
        hg = 0.5 * g_ref[:, cols].astype(F32)
        silu = hg * jnp.tanh(hg) + hg
        o_ref[:, cols] = (b_s[sl, 0:seq, :] * silu).astype(BF16)


def _rg_lru(u3, conv_w, conv_b, w_a, b_a, w_x, b_x, lam, *, blocks_per_step=2):
    b, s, _ = u3.shape
    pitch = _scan_pitch(s)
    nblk = blocks_per_step
    width = nblk * LRU_BLOCK_W
    n_slab = width // LANES
    n_steps = LRU_BLOCKS // nblk
    body = functools.partial(_lru_body, seq=s, pitch=pitch, nblk=nblk)

    def vec_spec(rows):
        return pl.BlockSpec((rows, width), lambda bi, n: (0, n))

    def blk_spec(rows):
        return pl.BlockSpec((nblk, rows, LRU_BLOCK_W), lambda bi, n: (n, 0, 0))

    return pl.pallas_call(
        body,
        grid=(b, n_steps),
        in_specs=[
            pl.BlockSpec((None, s, width), lambda bi, n: (bi, 0, n)),
            pl.BlockSpec((None, s, width), lambda bi, n: (bi, 0, n_steps + n)),
            vec_spec(CONV_W), vec_spec(1),
            blk_spec(LRU_BLOCK_W), blk_spec(1), blk_spec(LRU_BLOCK_W), blk_spec(1),
            vec_spec(1),
        ],
        out_specs=pl.BlockSpec((None, s, width), lambda bi, n: (bi, 0, n)),
        out_shape=jax.ShapeDtypeStruct((b, s, LRU_WIDTH), BF16),
        scratch_shapes=[pltpu.VMEM((n_slab, s + SUBLANES, LANES), F32),
                        pltpu.VMEM((n_slab, LRU_SEG_GROUPS * SCAN_SEGS * pitch, LANES), F32),
                        pltpu.VMEM((n_slab, LRU_SEG_GROUPS * SCAN_SEGS * pitch, LANES), F32)],
        compiler_params=_cparams(("parallel", "arbitrary")),
        name="rg_lru",
    )(u3, u3, conv_w, conv_b, w_a, b_a, w_x, b_x, lam)


def _s5_body(u_ref, bm_ref, cm_ref, are_ref, aim_ref, dskip_ref, y_ref,
             su, ut, bre, bim, hre, him, *, tt, pitch, n_sub):
    nb = SCAN_SEGS
    n_q = SSM_WIDTH // LANES
    n_state = SSM_GROUPS * SSM_STATE
    half_w = SSM_WIDTH // 2
    half_s = n_state // 2

    @pl.when(pl.program_id(1) == 0)
    def _():
        hre[...] = jnp.zeros_like(hre)
        him[...] = jnp.zeros_like(him)

    for b in range(nb):
        ub = u_ref[b].astype(F32)
        for q in range(n_q):
            su[q, b * pitch:b * pitch + tt, :] = ub[:, q * LANES:(q + 1) * LANES]

    def to_time_major(t, _):
        rows = pl.ds(pl.multiple_of(t * nb, nb), nb)
        for q in range(n_q):
            ut[rows, q * LANES:(q + 1) * LANES] = su[q, pl.ds(t, nb, stride=pitch), :]
        return 0

    lax.fori_loop(0, tt, to_time_major, 0)

    sub_rows = (tt // n_sub) * nb
    n_slab = n_state // LANES
    group = 4

    def input_map(j):
        r = slice(j * sub_rows, (j + 1) * sub_rows)
        u_bf = ut[r, :].astype(BF16)
        for hf in range(2):
            uh = u_bf[:, hf * half_w:(hf + 1) * half_w]
            cols = slice(hf * half_s, (hf + 1) * half_s)
            bre[r, cols] = jnp.dot(uh, bm_ref[hf, :, 0:half_s], preferred_element_type=F32)
            bim[r, cols] = jnp.dot(uh, bm_ref[hf, :, half_s:], preferred_element_type=F32)

    def scan(j):
        for k0 in range(0, n_slab, group):
            sl = [slice((k0 + k) * LANES, (k0 + k + 1) * LANES) for k in range(group)]
            a_r = [jnp.broadcast_to(are_ref[:, c], (nb, LANES)) for c in sl]
            a_i = [jnp.broadcast_to(aim_ref[:, c], (nb, LANES)) for c in sl]
            if j == 0:
                x = [(hre[:, c], him[:, c]) for c in sl]
            else:
                prev = slice(j * sub_rows - nb, j * sub_rows)
                x = [(bre[prev, c], bim[prev, c]) for c in sl]
            for t in range(tt // n_sub):
                rows = slice(j * sub_rows + t * nb, j * sub_rows + (t + 1) * nb)
                for k in range(group):
                    x_r, x_i = x[k]
                    n_r = a_r[k] * x_r - a_i[k] * x_i + bre[rows, sl[k]]
                    n_i = a_r[k] * x_i + a_i[k] * x_r + bim[rows, sl[k]]
                    bre[rows, sl[k]] = n_r
                    bim[rows, sl[k]] = n_i
                    x[k] = (n_r, n_i)

    def output_map(j):
        r = slice(j * sub_rows, (j + 1) * sub_rows)
        for hf in range(2):
            cols = slice(hf * half_s, (hf + 1) * half_s)
            y = jnp.dot(bre[r, cols].astype(BF16), cm_ref[hf, 0:half_s, :], preferred_element_type=F32)
            y = y + jnp.dot(bim[r, cols].astype(BF16), cm_ref[hf, half_s:, :],
                            preferred_element_type=F32)
            oc = slice(hf * half_w, (hf + 1) * half_w)
            ut[r, oc] = y + dskip_ref[:, oc] * ut[r, oc]

    input_map(0)
    for j in range(n_sub):
        if j + 1 < n_sub:
            input_map(j + 1)
        scan(j)
        if j > 0:
            output_map(j - 1)
    last = slice(tt * nb - nb, tt * nb)
    hre[...] = bre[last, :]
    him[...] = bim[last, :]
    output_map(n_sub - 1)

    def to_batch_major(t, _):
        rows = pl.ds(pl.multiple_of(t * nb, nb), nb)
        for q in range(n_q):
            su[q, pl.ds(t, nb, stride=pitch), :] = ut[rows, q * LANES:(q + 1) * LANES]
        return 0

    lax.fori_loop(0, tt, to_batch_major, 0)
    for b in range(nb):
        for q in range(n_q):
            y_ref[b, :, q * LANES:(q + 1) * LANES] = su[q, b * pitch:b * pitch + tt, :]


def _s5_scan(u3, b_mat, c_mat, a_re, a_im, d_skip, *, tt=128, n_sub=2):
    b, s, _ = u3.shape
    nb = SCAN_SEGS
    pitch = tt + 4
    n_state = SSM_GROUPS * SSM_STATE
    din_blk = 2 * LRU_WIDTH // SSM_WIDTH
    body = functools.partial(_s5_body, tt=tt, pitch=pitch, n_sub=n_sub)
    full = lambda shape: pl.BlockSpec(shape, lambda bi, ti: (0,) * len(shape))
    return pl.pallas_call(
        body,
        grid=(b // nb, s // tt),
        in_specs=[
            pl.BlockSpec((nb, tt, SSM_WIDTH), lambda bi, ti: (bi, ti, din_blk)),
            full(b_mat.shape), full(c_mat.shape), full((1, n_state)), full((1, n_state)),
            full((1, SSM_WIDTH)),
        ],
        out_specs=pl.BlockSpec((nb, tt, SSM_WIDTH), lambda bi, ti: (bi, ti, 0)),
        out_shape=jax.ShapeDtypeStruct((b, s, SSM_WIDTH), F32),
        scratch_shapes=[
            pltpu.VMEM((SSM_WIDTH // LANES, nb * pitch, LANES), F32),
            pltpu.VMEM((tt * nb, SSM_WIDTH), F32),
            pltpu.VMEM((tt * nb, n_state), F32),
            pltpu.VMEM((tt * nb, n_state), F32),
            pltpu.VMEM((nb, n_state), F32),
            pltpu.VMEM((nb, n_state), F32),
        ],
        compiler_params=_cparams(("parallel", "arbitrary")),
        name="s5_scan",
    )(u3, b_mat, c_mat, a_re, a_im, d_skip)


def _s5_operators(a_re, a_im, b_re, b_im, c_re, c_im, log_dt):
    a_re, a_im = a_re.astype(F32), a_im.astype(F32)
    dt = jnp.exp(log_dt.astype(F32))[:, None]
    mag = jnp.exp(a_re * dt)
    ab_re, ab_im = mag * jnp.cos(a_im * dt), mag * jnp.sin(a_im * dt)
    den = a_re * a_re + a_im * a_im
    f_re = ((ab_re - 1.0) * a_re + ab_im * a_im) / den
    f_im = (ab_im * a_re - (ab_re - 1.0) * a_im) / den
    b_re, b_im = b_re.astype(F32), b_im.astype(F32)
    bb_re = f_re[..., None] * b_re - f_im[..., None] * b_im
    bb_im = f_re[..., None] * b_im + f_im[..., None] * b_re

    gh = SSM_GROUPS // 2
    eye = jnp.eye(gh, dtype=F32)

    def in_map(bb):
        t = bb.reshape(2, gh, SSM_STATE, SSM_GROUP)
        t = t.transpose(0, 1, 3, 2)[:, :, :, None, :] * eye[None, :, None, :, None]
        return t.reshape(2, gh * SSM_GROUP, gh * SSM_STATE)

    def out_map(cc):
        t = cc.reshape(2, gh, SSM_GROUP, SSM_STATE)
        t = t.transpose(0, 1, 3, 2)[:, :, :, None, :] * eye[None, :, None, :, None]
        return t.reshape(2, gh * SSM_STATE, gh * SSM_GROUP)

    b_mat = jnp.concatenate([in_map(bb_re), in_map(bb_im)], axis=2).astype(BF16)
    c_mat = jnp.concatenate([out_map(c_re.astype(F32)), out_map(-c_im.astype(F32))],
                            axis=1).astype(BF16)
    n_state = SSM_GROUPS * SSM_STATE
    return b_mat, c_mat, ab_re.reshape(1, n_state), ab_im.reshape(1, n_state)


def _rec_out_body(yc_ref, y5_ref, dg_ref, wglu_ref, bglu_ref, w_ref, x_ref, o_ref, *, wc):
    yd = y5_ref[...]
    inner = math.sqrt(2.0 / math.pi) * (yd + 0.044715 * (yd * yd * yd))
    yd = 0.5 * yd * (1.0 + jnp.tanh(inner))
    glu = jnp.dot(yd.astype(BF16), wglu_ref[...], preferred_element_type=F32) + bglu_ref[...]
    yd = yd * _sigmoid(glu)
    g = dg_ref[...].astype(F32)
    yd = (yd * (g * _sigmoid(g))).astype(BF16)
    _proj_residual(yc_ref[...], yd, w_ref, x_ref, o_ref, wc)


def _rec_out(yc, y5, u2, w_glu, b_glu, w_bf16, x2, *, tm=512):
    t, d = x2.shape
    wc = yc.shape[1]
    gate_blk = 2 * LRU_WIDTH // SSM_WIDTH + 1
    return pl.pallas_call(
        functools.partial(_rec_out_body, wc=wc),
        grid=(t // tm,),
        in_specs=[
            pl.BlockSpec((tm, wc), lambda i: (i, 0)),
            pl.BlockSpec((tm, SSM_WIDTH), lambda i: (i, 0)),
            pl.BlockSpec((tm, SSM_WIDTH), lambda i: (i, gate_blk)),
            pl.BlockSpec((SSM_WIDTH, SSM_WIDTH), lambda i: (0, 0)),
            pl.BlockSpec((1, SSM_WIDTH), lambda i: (0, 0)),
            pl.BlockSpec((wc + SSM_WIDTH, d), lambda i: (0, 0)),
            pl.BlockSpec((tm, d), lambda i: (i, 0)),
        ],
        out_specs=pl.BlockSpec((tm, d), lambda i: (i, 0)),
        out_shape=jax.ShapeDtypeStruct((t, d), F32),
        compiler_params=_cparams(("parallel",)),
        name="rec_out",
    )(yc, y5, u2, w_glu, b_glu, w_bf16, x2)


def _attention_layer(x2, b, s, norm_g, w_in, q_g_a, k_g_a, rel_bias, q_g_b, k_g_b,
                     lq1, lk1, lq2, lk2, subln_g, w_out, layer_idx):
    scale = HEAD_DIM ** -0.5 * LOG2E
    ones = jnp.ones((1024,), F32)
    col_gain = jnp.concatenate([
        jnp.tile(q_g_a.astype(F32) * scale, A_HEADS), jnp.tile(k_g_a.astype(F32), A_HEADS), ones, ones,
        jnp.tile(q_g_b.astype(F32).reshape(-1) * scale, B_HEADS),
        jnp.tile(k_g_b.astype(F32).reshape(-1), B_HEADS), ones, ones])[None, :]
    tn = 2048
    u = _norm_proj(x2, (norm_g.astype(F32)[:, None] * w_in).astype(BF16), col_gain,
                   norm_tiles=(0, 2), tn=tn)
    u3 = u.reshape(b, s, ATTN_IN)
    bound_a = (BOUND_SLACK * HEAD_DIM * scale * jnp.max(jnp.abs(q_g_a.astype(F32)))
               * jnp.max(jnp.abs(k_g_a.astype(F32))))
    ya = _attn_a(u3, *_attn_a_bias(rel_bias, bound_a))
    lam_init = 0.8 - 0.6 * math.exp(-0.3 * layer_idx)
    row = lambda v: v.astype(F32)[None, :]
    qk_bound = (BOUND_SLACK * HEAD_DIM * scale * jnp.max(jnp.abs(q_g_b.astype(F32)))
                * jnp.max(jnp.abs(k_g_b.astype(F32)))).reshape(1, 1)
    yb = _attn_b(u3, row(lq1), row(lk1), row(lq2), row(lk2), row(subln_g), qk_bound, lam_init)
    return _out_proj(ya.reshape(b * s, A_WIDTH), yb.reshape(b * s, B_WIDTH), w_out.astype(BF16), x2)


def _recurrent_layer(x2, b, s, norm_g, w_in, conv_w, conv_b, w_a, b_a, w_x, b_x, lru_lam,
                     a_re, a_im, b_re, b_im, c_re, c_im, d_skip, log_dt, w_glu, b_glu, w_out):
    u = _norm_proj(x2, (norm_g.astype(F32)[:, None] * w_in).astype(BF16),
                   jnp.ones((1, REC_IN), F32), norm_tiles=(), tn=2048)
    u3 = u.reshape(b, s, REC_IN)
    yc = _rg_lru(u3, 0.5 * conv_w.astype(F32), 0.5 * conv_b.astype(F32)[None, :],
                 w_a.astype(BF16), 0.5 * b_a.astype(F32)[:, None, :],
                 w_x.astype(BF16), 0.5 * b_x.astype(F32)[:, None, :],
                 lru_lam.astype(F32)[None, :])

    b_mat, c_mat, ab_re, ab_im = _s5_operators(a_re, a_im, b_re, b_im, c_re, c_im, log_dt)
    y5 = _s5_scan(u3, b_mat, c_mat, ab_re, ab_im, d_skip.astype(F32)[None, :])
    return _rec_out(yc.reshape(b * s, LRU_WIDTH), y5.reshape(b * s, SSM_WIDTH), u,
                    w_glu.astype(BF16), b_glu.astype(F32)[None, :], w_out.astype(BF16), x2)


def kernel(x, attn_norm_g, attn_w_in, a_q_g, a_k_g, a_rel_bias, b_q_g, b_k_g, b_lam_q1, b_lam_k1,
           b_lam_q2, b_lam_k2, b_subln_g, attn_w_out, rec_norm_g, rec_w_in, lru_conv_w, lru_conv_b,
           lru_w_a, lru_b_a, lru_w_x, lru_b_x, lru_lambda, ssm_a_re, ssm_a_im, ssm_b_re, ssm_b_im,
           ssm_c_re, ssm_c_im, ssm_d, ssm_log_dt, ssm_w_glu, ssm_b_glu, rec_w_out):
    b, s, d = x.shape
    depth = attn_norm_g.shape[0] + rec_norm_g.shape[0]
    x2 = x.reshape(b * s, d)
    for layer in range(depth):
        j = layer // 2
        if layer % 2 == 0:
            x2 = _attention_layer(x2, b, s, attn_norm_g[j], attn_w_in[j], a_q_g[j], a_k_g[j],
                                  a_rel_bias[j], b_q_g[j], b_k_g[j], b_lam_q1[j], b_lam_k1[j],
                                  b_lam_q2[j], b_lam_k2[j], b_subln_g[j], attn_w_out[j], layer)
        else:
            x2 = _recurrent_layer(x2, b, s, rec_norm_g[j], rec_w_in[j], lru_conv_w[j], lru_conv_b[j],
                                  lru_w_a[j], lru_b_a[j], lru_w_x[j], lru_b_x[j], lru_lambda[j],
                                  ssm_a_re[j], ssm_a_im[j], ssm_b_re[j], ssm_b_im[j],
                                  ssm_c_re[j], ssm_c_im[j], ssm_d[j], ssm_log_dt[j],
                                  ssm_w_glu[j], ssm_b_glu[j], rec_w_out[j])
    return x2.reshape(b, s, d)
```
